```python
import jax, jax.numpy as jnp
from jax import lax
import numpy as np

D_MODEL = 2048
BATCH = 4
SEQ = 4096
DEPTH = 2

N_META = 16
MIX_WIDTH = 1024
N_BRANCH = 3
MLSTM_HEADS = 8
MLSTM_HEAD_DIM = MIX_WIDTH // MLSTM_HEADS
MLSTM_CHUNK = 64
CONV_WIDTH = 3
POOL_WINDOWS = (2, 4, 8, 16)
POOL_GROUP = MIX_WIDTH // len(POOL_WINDOWS)
D_FF = 5632
RMS_EPS = 1e-6
MIXER_SPLITS = (MIX_WIDTH, MIX_WIDTH, MIX_WIDTH, MIX_WIDTH, MLSTM_HEADS, MLSTM_HEADS,
                MIX_WIDTH, MIX_WIDTH, MIX_WIDTH, MIX_WIDTH, N_BRANCH * D_MODEL)
IN_COLS = 8 * MIX_WIDTH + 2 * MLSTM_HEADS + N_BRANCH * D_MODEL

kernel_name = "hybrid_mlstm_shortconv_pool_gated_block"


def rms_norm(x, g):
    xf = x.astype(jnp.float32)
    y = xf * lax.rsqrt(jnp.mean(xf * xf, axis=-1, keepdims=True) + RMS_EPS)
    return (y * g.astype(jnp.float32)).astype(x.dtype)


def causal_dwconv(u, w):
    K = w.shape[0]
    L = u.shape[1]
    up = jnp.pad(u, ((0, 0), (K - 1, 0), (0, 0)))
    y = up[:, 0:L] * w[0]
    for j in range(1, K):
        y = y + up[:, j:j + L] * w[j]
    return y


def mlstm_chunk(state, q, k, v, logi, logf):
    C, n, m = state
    T = q.shape[2]
    b = jnp.cumsum(logf, axis=-1)
    D = b[..., :, None] - b[..., None, :] + logi[..., None, :]
    causal = jnp.tril(jnp.ones((T, T), dtype=bool))
    D = jnp.where(causal, D, -jnp.inf)
    inter = b + m[..., None]
    m_t = jnp.maximum(inter, jnp.max(D, axis=-1))
    inter_w = jnp.exp(inter - m_t)
    s = jnp.einsum('bhtd,bhsd->bhts', q, k) * jnp.exp(D - m_t[..., None])
    num = inter_w[..., None] * jnp.einsum('bhtd,bhde->bhte', q, C) + jnp.einsum('bhts,bhse->bhte', s, v)
    den = inter_w * jnp.einsum('bhtd,bhd->bht', q, n) + jnp.sum(s, axis=-1)
    h = num / jnp.maximum(jnp.abs(den), jnp.exp(-m_t))[..., None]
    b_T = b[..., -1]
    w_log = b_T[..., None] - b + logi
    m_new = jnp.maximum(b_T + m, jnp.max(w_log, axis=-1))
    decay = jnp.exp(b_T + m - m_new)
    ws = jnp.exp(w_log - m_new[..., None])
    C_new = decay[..., None, None] * C + jnp.einsum('bhs,bhsd,bhse->bhde', ws, k, v)
    n_new = decay[..., None] * n + jnp.einsum('bhs,bhsd->bhd', ws, k)
    return (C_new, n_new, m_new), h


def mlstm(q, k, v, i_pre, f_pre, o_pre, head_gain):
    B, L, _ = q.shape
    H, Dh, CH = MLSTM_HEADS, MLSTM_HEAD_DIM, MLSTM_CHUNK
    n_chunks = (L - N_META) // CH

    def heads(t):
        return t.astype(jnp.float32).reshape(B, L, H, Dh).transpose(0, 2, 1, 3)

    qh = heads(q) * (Dh ** -0.5)
    kh = heads(k)
    vh = heads(v)
    logi = i_pre.astype(jnp.float32).transpose(0, 2, 1)
    logf = jax.nn.log_sigmoid(f_pre.astype(jnp.float32)).transpose(0, 2, 1)

    def meta(t):
        return t[:, :, :N_META]

    def chunks(t):
        t = t[:, :, N_META:]
        t = t.reshape((B, H, n_chunks, CH) + t.shape[3:])
        return jnp.moveaxis(t, 2, 0)

    init = (jnp.zeros((B, H, Dh, Dh), jnp.float32), jnp.zeros((B, H, Dh), jnp.float32),
            jnp.zeros((B, H), jnp.float32))
    state, h_meta = mlstm_chunk(init, meta(qh), meta(kh), meta(vh), meta(logi), meta(logf))
    xs = (chunks(qh), chunks(kh), chunks(vh), chunks(logi), chunks(logf))
    _, h_real = lax.scan(lambda st, c: mlstm_chunk(st, *c), state, xs)
    h_real = jnp.moveaxis(h_real, 0, 2).reshape(B, H, n_chunks * CH, Dh)
    h = jnp.concatenate([h_meta, h_real], axis=2)
    h = h * lax.rsqrt(jnp.mean(h * h, axis=-1, keepdims=True) + RMS_EPS)
    h = h.transpose(0, 2, 1, 3).reshape(B, L, H * Dh) * head_gain.astype(jnp.float32)
    return (h * jax.nn.sigmoid(o_pre.astype(jnp.float32))).astype(q.dtype)


def multiscale_pool(u, pool_w, pool_scale):
    B, L, _ = u.shape
    uf = u.astype(jnp.float32)
    cs = jnp.pad(jnp.cumsum(uf, axis=1), ((0, 0), (1, 0), (0, 0)))
    t = jnp.arange(L)
    outs = []
    for g, w in enumerate(POOL_WINDOWS):
        csg = cs[..., g * POOL_GROUP:(g + 1) * POOL_GROUP]
        lagged = jnp.pad(csg, ((0, 0), (w - 1, 0), (0, 0)))[:, :L]
        count = jnp.minimum(t + 1, w).astype(jnp.float32)[None, :, None]
        outs.append((csg[:, 1:] - lagged) / count - uf[..., g * POOL_GROUP:(g + 1) * POOL_GROUP])
    p = jnp.stack(outs, axis=2)
    y = jnp.einsum('blgc,gcd->blgd', p, pool_w.astype(jnp.float32)).reshape(B, L, MIX_WIDTH)
    return (y * pool_scale.astype(jnp.float32)).astype(u.dtype)


def token_mixer(h, w_in, b_if, head_gain, conv_w, pool_w, pool_scale, w_branch, w_out):
    B, L, _ = h.shape
    offsets = [int(o) for o in np.cumsum(MIXER_SPLITS)[:-1]]
    q, k, v, o, ip, fp, cb, cc, cx, pu, gates = jnp.split(h @ w_in, offsets, axis=-1)
    y_a = mlstm(q, k, v, ip + b_if[0], fp + b_if[1], o, head_gain)
    y_b = cb * causal_dwconv(cc * cx, conv_w)
    y_c = multiscale_pool(pu, pool_w, pool_scale)
    ys = jnp.stack([y_a, y_b, y_c], axis=2)
    branch = jnp.einsum('blnc,ncd->blnd', ys, w_branch)
    g = jax.nn.sigmoid(gates.reshape(B, L, N_BRANCH, D_MODEL))
    merged = jnp.sum(g * branch, axis=2)
    return merged @ w_out


def conv_ffn(h, w_ffn_in, conv_w, w_ffn_out):
    a, u = jnp.split(h @ w_ffn_in, 2, axis=-1)
    a = causal_dwconv(a, conv_w)
    return (jax.nn.gelu(a, approximate=True) * u) @ w_ffn_out


def setup_inputs(seed: int = 0) -> dict:
    key = jax.random.key(seed)
    ks = jax.random.split(key, 20)
    nrm = jax.random.normal
    f_bias = jnp.linspace(3.0, 6.0, MLSTM_HEADS)[None, :] + 0.1 * nrm(ks[8], (DEPTH, MLSTM_HEADS))
    i_bias = 0.1 * nrm(ks[7], (DEPTH, MLSTM_HEADS))
    return {
        "x": nrm(ks[0], (BATCH, SEQ, D_MODEL)),
        "meta_tokens": nrm(ks[1], (N_META, D_MODEL)),
        "norm_pre_mix": 1.0 + 0.05 * nrm(ks[2], (DEPTH, D_MODEL)),
        "norm_post_mix": 1.0 + 0.05 * nrm(ks[3], (DEPTH, D_MODEL)),
        "norm_pre_ffn": 1.0 + 0.05 * nrm(ks[4], (DEPTH, D_MODEL)),
        "norm_post_ffn": 1.0 + 0.05 * nrm(ks[5], (DEPTH, D_MODEL)),
        "w_in": nrm(ks[6], (DEPTH, D_MODEL, IN_COLS)) * D_MODEL ** -0.5,
        "b_if": jnp.stack([i_bias, f_bias], axis=1),
        "mlstm_head_gain": 1.0 + 0.05 * nrm(ks[9], (DEPTH, MIX_WIDTH)),
        "conv_mix_w": nrm(ks[10], (DEPTH, CONV_WIDTH, MIX_WIDTH)) * CONV_WIDTH ** -0.5,
        "pool_w": nrm(ks[11], (DEPTH, len(POOL_WINDOWS), POOL_GROUP, POOL_GROUP)) * POOL_GROUP ** -0.5,
        "pool_scale": 1.0 + 0.1 * nrm(ks[12], (DEPTH, MIX_WIDTH)),
        "w_branch": nrm(ks[13], (DEPTH, N_BRANCH, MIX_WIDTH, D_MODEL)) * MIX_WIDTH ** -0.5,
        "w_out": nrm(ks[14], (DEPTH, D_MODEL, D_MODEL)) * D_MODEL ** -0.5,
        "w_ffn_in": nrm(ks[15], (DEPTH, D_MODEL, 2 * D_FF)) * D_MODEL ** -0.5,
        "ffn_conv_w": nrm(ks[16], (DEPTH, CONV_WIDTH, D_FF)) * CONV_WIDTH ** -0.5,
        "w_ffn_out": nrm(ks[17], (DEPTH, D_FF, D_MODEL)) * D_FF ** -0.5,
    }


def reference(x, meta_tokens, norm_pre_mix, norm_post_mix, norm_pre_ffn, norm_post_ffn, w_in, b_if,
              mlstm_head_gain, conv_mix_w, pool_w, pool_scale, w_branch, w_out, w_ffn_in, ffn_conv_w,
              w_ffn_out):
    B = x.shape[0]
    meta = jnp.broadcast_to(meta_tokens.astype(x.dtype)[None], (B, N_META, D_MODEL))
    x = jnp.concatenate([meta, x], axis=1)
    for l in range(DEPTH):
        h = rms_norm(x, norm_pre_mix[l])
        mix = token_mixer(h, w_in[l], b_if[l], mlstm_head_gain[l], conv_mix_w[l], pool_w[l],
                          pool_scale[l], w_branch[l], w_out[l])
        x = x + rms_norm(mix, norm_post_mix[l])
        h = rms_norm(x, norm_pre_ffn[l])
        x = x + rms_norm(conv_ffn(h, w_ffn_in[l], ffn_conv_w[l], w_ffn_out[l]), norm_post_ffn[l])
    return x[:, N_META:]
```

```python
import functools

import jax
import jax.numpy as jnp
from jax import lax
from jax.experimental import pallas as pl
from jax.experimental.pallas import tpu as pltpu

D_MODEL = 2048
N_META = 16
MIX_WIDTH = 1024
N_BRANCH = 3
HEADS = 8
HEAD_DIM = MIX_WIDTH // HEADS
POOL_WINDOWS = (2, 4, 8, 16)
POOL_GROUP = MIX_WIDTH // len(POOL_WINDOWS)
D_FF = 5632
RMS_EPS = 1e-6
CONV_WIDTH = 3

COL_Q, COL_K, COL_V, COL_O, COL_CB, COL_CC, COL_CX, COL_PU = (n * MIX_WIDTH for n in range(8))
COL_GATES = 8 * MIX_WIDTH
BIG_COLS = COL_GATES + N_BRANCH * D_MODEL
LANES = 128
HALO = 16
META_ROWS = 128
VMEM_LIMIT_CAP = 56 * 1024 * 1024

F32 = jnp.float32
BF16 = jnp.bfloat16


def _params(semantics, vmem_bytes):
    return pltpu.CompilerParams(dimension_semantics=semantics,
                                vmem_limit_bytes=min(int(vmem_bytes), VMEM_LIMIT_CAP))


def _rms(x, g):
    return x * lax.rsqrt(jnp.mean(x * x, axis=-1, keepdims=True) + RMS_EPS) * g


def _norm_kernel(x_ref, g_ref, h_ref):
    h_ref[...] = _rms(x_ref[...], g_ref[...]).astype(BF16)


def _norm(x, g, bm):
    m = x.shape[0]
    return pl.pallas_call(
        _norm_kernel,
        grid=(m // bm,),
        in_specs=[pl.BlockSpec((bm, D_MODEL), lambda i: (i, 0)),
                  pl.BlockSpec((1, D_MODEL), lambda i: (0, 0))],
        out_specs=pl.BlockSpec((bm, D_MODEL), lambda i: (i, 0)),
        out_shape=jax.ShapeDtypeStruct((m, D_MODEL), BF16),
        compiler_params=_params(("parallel",), 10 * bm * D_MODEL * 4),
        name="pre_norm",
    )(x, g)


def _matmul_kernel(h_ref, w_ref, o_ref):
    o_ref[...] = jnp.dot(h_ref[...], w_ref[...], preferred_element_type=F32).astype(o_ref.dtype)


def _matmul(h, w, bm, bn, out_dtype, name):
    m, k = h.shape
    n = w.shape[1]
    est = 2 * (bm * k * 2 + k * bn * 2 + bm * bn * 4) + 2 * bm * bn * 4
    return pl.pallas_call(
        _matmul_kernel,
        grid=(n // bn, m // bm),
        in_specs=[pl.BlockSpec((bm, k), lambda j, i: (i, 0)),
                  pl.BlockSpec((k, bn), lambda j, i: (0, j))],
        out_specs=pl.BlockSpec((bm, bn), lambda j, i: (i, j)),
        out_shape=jax.ShapeDtypeStruct((m, n), out_dtype),
        compiler_params=_params(("parallel", "parallel"), est),
        name=name,
    )(h, w)


def _shift_rows(x, d):
    return pltpu.roll(x, d, axis=0)


def _cumsum_rows(x):
    rows = x.shape[0]
    row = lax.broadcasted_iota(jnp.int32, x.shape, 0)
    d = 1
    while d < rows:
        x = x + jnp.where(row >= d, _shift_rows(x, d), 0.0)
        d *= 2
    return x


def _cummax_rows(x):
    rows = x.shape[0]
    row = lax.broadcasted_iota(jnp.int32, x.shape, 0)
    d = 1
    while d < rows:
        x = jnp.maximum(x, jnp.where(row >= d, _shift_rows(x, d), -jnp.inf))
        d *= 2
    return x


def _log_sigmoid(x):
    return jnp.minimum(x, 0.0) - jnp.log1p(jnp.exp(-jnp.abs(x)))


def _mlstm_kernel(*refs, chunk, n_valid, zero_init, emit_state):
    q_ref, k_ref, v_ref, o_ref, gl_ref, bias_ref, gain_ref = refs[:7]
    refs = refs[7:]
    if not zero_init:
        c0_ref, m0_ref = refs[:2]
        refs = refs[2:]
    y_ref = refs[0]
    refs = refs[1:]
    if emit_state:
        cout_ref, mout_ref = refs[:2]
        refs = refs[2:]
    c_scr, m_scr = refs

    @pl.when(pl.program_id(1) == 0)
    def _():
        if zero_init:
            c_scr[...] = jnp.zeros_like(c_scr)
            m_scr[...] = jnp.zeros_like(m_scr)
        else:
            c_scr[...] = c0_ref[...]
            m_scr[...] = m0_ref[...]

    scale = HEAD_DIM ** -0.5
    last = n_valid - 1
    gl = gl_ref[...] + bias_ref[...]
    logf = pltpu.roll(_log_sigmoid(gl), LANES - HEADS, axis=1)
    b = _cumsum_rows(logf)
    a = gl - b
    m_prev = m_scr[...]
    g = jnp.maximum(_cummax_rows(a), m_prev)
    inter_w = jnp.exp(m_prev - g) * scale
    exp_neg_m = jnp.exp(-(b + g))
    g_last = g[last:last + 1, :]
    decay = jnp.exp(m_prev - g_last)
    row1 = lax.broadcasted_iota(jnp.int32, a.shape, 0)
    ws = jnp.where(row1 <= last, jnp.exp(a - g_last), 0.0)
    m_scr[...] = b[last:last + 1, :] + g_last
    a_t = a.T

    rows = lax.broadcasted_iota(jnp.int32, (chunk, chunk), 0)
    cols = lax.broadcasted_iota(jnp.int32, (chunk, chunk), 1)
    causal = rows >= cols
    ones = jnp.ones((chunk, HEAD_DIM), BF16)

    for h in range(HEADS):
        sl = slice(h * HEAD_DIM, (h + 1) * HEAD_DIM)
        qh, kh, vh = q_ref[:, sl], k_ref[:, sl], v_ref[:, sl]
        w = jnp.where(causal, jnp.exp(a_t[h:h + 1, :] - g[:, h:h + 1]) * scale, 0.0)
        s = lax.dot_general(qh, kh, (((1,), (1,)), ((), ())), preferred_element_type=F32) * w
        v_aug = jnp.concatenate([vh, ones], axis=1)
        c_h = c_scr[h]
        r = (jnp.dot(qh, c_h.astype(BF16), preferred_element_type=F32) * inter_w[:, h:h + 1]
             + jnp.dot(s.astype(BF16), v_aug, preferred_element_type=F32))
        num, den = r[:, :HEAD_DIM], r[:, HEAD_DIM:]
        hh = num / jnp.maximum(jnp.abs(den), exp_neg_m[:, h:h + 1])
        hn = hh * lax.rsqrt(jnp.mean(hh * hh, axis=-1, keepdims=True) + RMS_EPS)
        y_ref[:, sl] = (hn * gain_ref[:, sl] * jax.nn.sigmoid(o_ref[:, sl].astype(F32))).astype(BF16)
        kw = (kh.astype(F32) * ws[:, h:h + 1]).astype(BF16)
        upd = lax.dot_general(kw, v_aug, (((0,), (0,)), ((), ())), preferred_element_type=F32)
        c_scr[h] = decay[:, h:h + 1] * c_h + upd

    if emit_state:
        cout_ref[...] = c_scr[...]
        mout_ref[...] = m_scr[...]


def _mlstm(big, logits, bias_row, gain, state, *, batches, chunk, n_valid):
    m = big.shape[0]
    n_chunks = m // (batches * chunk)
    zero_init = state is None

    def col(cblk):
        return pl.BlockSpec((chunk, MIX_WIDTH), lambda b, c: (b * n_chunks + c, cblk))

    in_specs = [col(0), col(1), col(2), col(3),
                pl.BlockSpec((chunk, LANES), lambda b, c: (b * n_chunks + c, 0)),
                pl.BlockSpec((1, LANES), lambda b, c: (0, 0)),
                pl.BlockSpec((1, MIX_WIDTH), lambda b, c: (0, 0))]
    args = [big, big, big, big, logits, bias_row, gain]
    c_shape = (HEADS, HEAD_DIM, 2 * HEAD_DIM)
    if not zero_init:
        in_specs += [pl.BlockSpec(c_shape, lambda b, c: (0, 0, 0)),
                     pl.BlockSpec((1, LANES), lambda b, c: (0, 0))]
        args += list(state)
    out_specs = [pl.BlockSpec((chunk, MIX_WIDTH), lambda b, c: (b * n_chunks + c, 0))]
    out_shape = [jax.ShapeDtypeStruct((m, MIX_WIDTH), BF16)]
    if zero_init:
        out_specs += [pl.BlockSpec(c_shape, lambda b, c: (0, 0, 0)),
                      pl.BlockSpec((1, LANES), lambda b, c: (0, 0))]
        out_shape += [jax.ShapeDtypeStruct(c_shape, F32), jax.ShapeDtypeStruct((1, LANES), F32)]
    est = 10 * chunk * MIX_WIDTH * 2 + 24 * chunk * chunk * 4 + 6 * HEADS * HEAD_DIM * 2 * HEAD_DIM * 4
    outs = pl.pallas_call(
        functools.partial(_mlstm_kernel, chunk=chunk, n_valid=n_valid, zero_init=zero_init,
                          emit_state=zero_init),
        grid=(batches, n_chunks),
        in_specs=in_specs,
        out_specs=out_specs,
        out_shape=out_shape,
        scratch_shapes=[pltpu.VMEM(c_shape, F32), pltpu.VMEM((1, LANES), F32)],
        compiler_params=_params(("arbitrary", "arbitrary"), est + 16 * 1024 * 1024),
        name="mlstm",
    )(*args)
    return outs


def _conv_pool_kernel(*refs, bm, tiles_per_batch, is_meta):
    cb_ref, cc_ref, cx_ref, pu_ref = refs[:4]
    refs = refs[4:]
    if not is_meta:
        ccp_ref, cxp_ref, pup_ref, ccm_ref, cxm_ref, pum_ref = refs[:6]
        refs = refs[6:]
    cw_ref, pw_ref, ps_ref, yb_ref, yc_ref, p_scr, u_scr = refs

    if is_meta:
        p_scr[0:HALO, :] = jnp.zeros((HALO, MIX_WIDTH), F32)
        u_scr[0:HALO, :] = jnp.zeros((HALO, MIX_WIDTH), F32)
    else:
        first = pl.program_id(0) % tiles_per_batch == 0
        cch = jnp.where(first, ccm_ref[...], ccp_ref[...]).astype(F32)
        cxh = jnp.where(first, cxm_ref[...], cxp_ref[...]).astype(F32)
        p_scr[0:HALO, :] = cch * cxh
        u_scr[0:HALO, :] = jnp.where(first, pum_ref[...], pup_ref[...]).astype(F32)
    p_scr[HALO:HALO + bm, :] = cc_ref[...].astype(F32) * cx_ref[...].astype(F32)
    u_scr[HALO:HALO + bm, :] = pu_ref[...].astype(F32)

    conv = cw_ref[2:3, :] * p_scr[HALO:HALO + bm, :]
    conv = conv + cw_ref[1:2, :] * p_scr[HALO - 1:HALO - 1 + bm, :]
    conv = conv + cw_ref[0:1, :] * p_scr[HALO - 2:HALO - 2 + bm, :]
    yb_ref[...] = (cb_ref[...].astype(F32) * conv).astype(BF16)

    for grp, win in enumerate(POOL_WINDOWS):
        sl = slice(grp * POOL_GROUP, (grp + 1) * POOL_GROUP)
        cur = u_scr[HALO:HALO + bm, sl]
        tot = cur
        for lag in range(1, win):
            tot = tot + u_scr[HALO - lag:HALO - lag + bm, sl]
        if is_meta:
            t = lax.broadcasted_iota(jnp.int32, (bm, 1), 0)
            pooled = tot / jnp.minimum(t + 1, win).astype(F32) - cur
        else:
            pooled = tot * (1.0 / win) - cur
        y = jnp.dot(pooled.astype(BF16), pw_ref[grp], preferred_element_type=F32)
        yc_ref[:, sl] = (y * ps_ref[:, sl]).astype(BF16)


def _conv_pool(big, big_meta, conv_w, pool_w, pool_scale, *, bm, tiles_per_batch):
    m = big.shape[0]
    is_meta = big_meta is None
    cblk = [COL_CB // MIX_WIDTH, COL_CC // MIX_WIDTH, COL_CX // MIX_WIDTH, COL_PU // MIX_WIDTH]
    in_specs = [pl.BlockSpec((bm, MIX_WIDTH), functools.partial(lambda i, c: (i, c), c=c)) for c in cblk]
    args = [big] * 4
    if not is_meta:
        per = bm // HALO
        in_specs += [pl.BlockSpec((HALO, MIX_WIDTH),
                                  functools.partial(lambda i, c: (jnp.maximum(i * per - 1, 0), c), c=c))
                     for c in cblk[1:]]
        in_specs += [pl.BlockSpec((HALO, MIX_WIDTH), functools.partial(lambda i, c: (0, c), c=c))
                     for c in cblk[1:]]
        args += [big] * 3 + [big_meta] * 3
    in_specs += [pl.BlockSpec((CONV_WIDTH, MIX_WIDTH), lambda i: (0, 0)),
                 pl.BlockSpec((len(POOL_WINDOWS), POOL_GROUP, POOL_GROUP), lambda i: (0, 0, 0)),
                 pl.BlockSpec((1, MIX_WIDTH), lambda i: (0, 0))]
    args += [conv_w, pool_w, pool_scale]
    out_spec = pl.BlockSpec((bm, MIX_WIDTH), lambda i: (i, 0))
    return pl.pallas_call(
        functools.partial(_conv_pool_kernel, bm=bm, tiles_per_batch=tiles_per_batch, is_meta=is_meta),
        grid=(m // bm,),
        in_specs=in_specs,
        out_specs=[out_spec, out_spec],
        out_shape=[jax.ShapeDtypeStruct((m, MIX_WIDTH), BF16)] * 2,
        scratch_shapes=[pltpu.VMEM((HALO + bm, MIX_WIDTH), F32)] * 2,
        compiler_params=_params(("arbitrary",), 16 * bm * MIX_WIDTH * 4 + 8 * 1024 * 1024),
        name="conv_pool",
    )(*args)


def _branch_kernel(ya_ref, yb_ref, yc_ref, w_ref, g0_ref, g1_ref, g2_ref, o_ref):
    acc = None
    for y_ref, g_ref, n in ((ya_ref, g0_ref, 0), (yb_ref, g1_ref, 1), (yc_ref, g2_ref, 2)):
        br = jnp.dot(y_ref[...], w_ref[n], preferred_element_type=F32)
        term = jax.nn.sigmoid(g_ref[...].astype(F32)) * br
        acc = term if acc is None else acc + term
    o_ref[...] = acc.astype(BF16)


def _branch(ya, yb, yc, w_branch, big, *, bm, bn):
    m = ya.shape[0]
    y_spec = pl.BlockSpec((bm, MIX_WIDTH), lambda j, i: (i, 0))
    g_specs = [pl.BlockSpec((bm, bn),
                            functools.partial(lambda j, i, off: (i, off + j), off=(COL_GATES + n * D_MODEL) // bn))
               for n in range(N_BRANCH)]
    est = 2 * (3 * bm * MIX_WIDTH * 2 + 3 * MIX_WIDTH * bn * 2 + 4 * bm * bn * 2) + 6 * bm * bn * 4
    return pl.pallas_call(
        _branch_kernel,
        grid=(D_MODEL // bn, m // bm),
        in_specs=[y_spec, y_spec, y_spec,
                  pl.BlockSpec((N_BRANCH, MIX_WIDTH, bn), lambda j, i: (0, 0, j))] + g_specs,
        out_specs=pl.BlockSpec((bm, bn), lambda j, i: (i, j)),
        out_shape=jax.ShapeDtypeStruct((m, D_MODEL), BF16),
        compiler_params=_params(("parallel", "parallel"), est),
        name="branch_merge",
    )(ya, yb, yc, w_branch, big, big, big)


def _proj_residual_kernel(*refs, n_k, emit_h):
    a_ref, w_ref, x_ref, gpost_ref = refs[:4]
    refs = refs[4:]
    if emit_h:
        gnext_ref = refs[0]
        refs = refs[1:]
    xo_ref = refs[0]
    refs = refs[1:]
    if emit_h:
        ho_ref = refs[0]
        refs = refs[1:]
    acc_ref = refs[0] if n_k > 1 else None

    def finish(y):
        xn = x_ref[...] + _rms(y, gpost_ref[...])
        xo_ref[...] = xn
        if emit_h:
            ho_ref[...] = _rms(xn, gnext_ref[...]).astype(BF16)

    part = jnp.dot(a_ref[...], w_ref[...], preferred_element_type=F32)
    if n_k == 1:
        finish(part)
        return
    k = pl.program_id(1)

    @pl.when(k == 0)
    def _():
        acc_ref[...] = part

    @pl.when(jnp.logical_and(k > 0, k < n_k - 1))
    def _():
        acc_ref[...] += part

    @pl.when(k == n_k - 1)
    def _():
        finish(acc_ref[...] + part)


def _proj_residual(a, w, x, g_post, g_next, *, bm, bk, name):
    m, kdim = a.shape
    n_k = kdim // bk
    emit_h = g_next is not None
    row = lambda i, k: (i, 0)
    vec = pl.BlockSpec((1, D_MODEL), lambda i, k: (0, 0))
    in_specs = [pl.BlockSpec((bm, bk), lambda i, k: (i, k)),
                pl.BlockSpec((bk, D_MODEL), lambda i, k: (k, 0)),
                pl.BlockSpec((bm, D_MODEL), row), vec]
    args = [a, w, x, g_post]
    out_specs = [pl.BlockSpec((bm, D_MODEL), row)]
    out_shape = [jax.ShapeDtypeStruct((m, D_MODEL), F32)]
    if emit_h:
        in_specs.append(vec)
        args.append(g_next)
        out_specs.append(pl.BlockSpec((bm, D_MODEL), row))
        out_shape.append(jax.ShapeDtypeStruct((m, D_MODEL), BF16))
    scratch = [pltpu.VMEM((bm, D_MODEL), F32)] if n_k > 1 else []
    est = (2 * (bm * bk * 2 + bk * D_MODEL * 2 + bm * D_MODEL * (4 + 4 + 2)) + 5 * bm * D_MODEL * 4)
    outs = pl.pallas_call(
        functools.partial(_proj_residual_kernel, n_k=n_k, emit_h=emit_h),
        grid=(m // bm, n_k),
        in_specs=in_specs,
        out_specs=out_specs,
        out_shape=out_shape,
        scratch_shapes=scratch,
        compiler_params=_params(("parallel", "arbitrary"), est),
        name=name,
    )(*args)
    return outs if emit_h else (outs[0], None)


def _gelu_tanh(x):
    return 0.5 * x * (1.0 + jnp.tanh(0.7978845608028654 * (x + 0.044715 * (x * x * x))))


def _ffn_in_kernel(*refs, bm, tiles_per_batch, is_meta):
    h_ref, wa_ref, wu_ref, cw_ref = refs[:4]
    refs = refs[4:]
    if not is_meta:
        am_ref = refs[0]
        refs = refs[1:]
    o_ref = refs[0]
    refs = refs[1:]
    if is_meta:
        ao_ref = refs[0]
        refs = refs[1:]
    a_scr, tail_scr = refs
    sub = 8

    h = h_ref[...]
    a = jnp.dot(h, wa_ref[...], preferred_element_type=F32)
    u = jnp.dot(h, wu_ref[...], preferred_element_type=F32)
    if is_meta:
        a_scr[0:sub, :] = jnp.zeros((sub, a.shape[1]), F32)
        ao_ref[...] = a
    else:
        first = pl.program_id(1) % tiles_per_batch == 0
        a_scr[0:sub, :] = jnp.where(first, am_ref[...], tail_scr[...])
    a_scr[sub:sub + bm, :] = a
    tail_scr[...] = a[bm - sub:bm, :]
    conv = cw_ref[2:3, :] * a
    conv = conv + cw_ref[1:2, :] * a_scr[sub - 1:sub - 1 + bm, :]
    conv = conv + cw_ref[0:1, :] * a_scr[sub - 2:sub - 2 + bm, :]
    o_ref[...] = (_gelu_tanh(conv) * u).astype(BF16)


def _ffn_in(h, w_ffn_in, conv_w, a_meta, *, bm, bn, tiles_per_batch):
    m = h.shape[0]
    is_meta = a_meta is None
    n_j = D_FF // bn
    in_specs = [pl.BlockSpec((bm, D_MODEL), lambda j, i: (i, 0)),
                pl.BlockSpec((D_MODEL, bn), lambda j, i: (0, j)),
                pl.BlockSpec((D_MODEL, bn), lambda j, i: (0, j + n_j)),
                pl.BlockSpec((CONV_WIDTH, bn), lambda j, i: (0, j))]
    args = [h, w_ffn_in, w_ffn_in, conv_w]
    if not is_meta:
        in_specs.append(pl.BlockSpec((8, bn), lambda j, i: (N_META // 8 - 1, j)))
        args.append(a_meta)
    out_specs = [pl.BlockSpec((bm, bn), lambda j, i: (i, j))]
    out_shape = [jax.ShapeDtypeStruct((m, D_FF), BF16)]
    if is_meta:
        out_specs.append(pl.BlockSpec((bm, bn), lambda j, i: (i, j)))
        out_shape.append(jax.ShapeDtypeStruct((m, D_FF), F32))
    est = 2 * (bm * D_MODEL * 2 + 2 * D_MODEL * bn * 2 + bm * bn * 6) + 8 * bm * bn * 4
    outs = pl.pallas_call(
        functools.partial(_ffn_in_kernel, bm=bm, tiles_per_batch=tiles_per_batch, is_meta=is_meta),
        grid=(n_j, m // bm),
        in_specs=in_specs,
        out_specs=out_specs,
        out_shape=out_shape,
        scratch_shapes=[pltpu.VMEM((8 + bm, bn), F32), pltpu.VMEM((8, bn), F32)],
        compiler_params=_params(("arbitrary", "arbitrary"), est),
        name="ffn_in",
    )(*args)
    return outs if is_meta else (outs[0], None)


def kernel(x, meta_tokens, norm_pre_mix, norm_post_mix, norm_pre_ffn, norm_post_ffn, w_in, b_if,
           mlstm_head_gain, conv_mix_w, pool_w, pool_scale, w_branch, w_out, w_ffn_in, ffn_conv_w,
           w_ffn_out):
    batch, seq, _ = x.shape
    depth = w_in.shape[0]
    m = batch * seq
    xr = x.reshape(m, D_MODEL)
    xm = jnp.pad(meta_tokens.astype(F32), ((0, META_ROWS - N_META), (0, 0)))

    if_lo = 4 * MIX_WIDTH
    if_hi = if_lo + 2 * HEADS

    def vec(v):
        return v.reshape(1, -1).astype(F32)

    hr = _norm(xr, vec(norm_pre_mix[0]), 512)
    hm = _norm(xm, vec(norm_pre_mix[0]), META_ROWS)
    for l in range(depth):
        w_big = jnp.concatenate([w_in[l][:, :if_lo], w_in[l][:, if_hi:]], axis=1).astype(BF16)
        w_if = jnp.pad(w_in[l][:, if_lo:if_hi], ((0, 0), (0, LANES - 2 * HEADS))).astype(BF16)
        bias_row = jnp.pad(b_if[l].reshape(1, 2 * HEADS).astype(F32), ((0, 0), (0, LANES - 2 * HEADS)))
        gain = vec(mlstm_head_gain[l])
        cw = conv_mix_w[l].astype(F32)
        pw = pool_w[l].astype(BF16)
        ps = vec(pool_scale[l])
        wbr = w_branch[l].astype(BF16)
        wo = w_out[l].astype(BF16)
        wfi = w_ffn_in[l].astype(BF16)
        fcw = ffn_conv_w[l].astype(F32)
        wfo = w_ffn_out[l].astype(BF16)
        g_post_mix, g_pre_ffn, g_post_ffn = vec(norm_post_mix[l]), vec(norm_pre_ffn[l]), vec(norm_post_ffn[l])
        g_next = vec(norm_pre_mix[l + 1]) if l + 1 < depth else None

        big_m = _matmul(hm, w_big, META_ROWS, 1024, BF16, "in_proj")
        log_m = _matmul(hm, w_if, META_ROWS, LANES, F32, "gate_logits")
        ya_m, c0, m0 = _mlstm(big_m, log_m, bias_row, gain, None, batches=1, chunk=META_ROWS, n_valid=N_META)
        yb_m, yc_m = _conv_pool(big_m, None, cw, pw, ps, bm=META_ROWS, tiles_per_batch=1)
        mg_m = _branch(ya_m, yb_m, yc_m, wbr, big_m, bm=META_ROWS, bn=512)
        xm, hf_m = _proj_residual(mg_m, wo, xm, g_post_mix, g_pre_ffn, bm=META_ROWS, bk=D_MODEL, name="out_proj")
        act_m, a_meta = _ffn_in(hf_m, wfi, fcw, None, bm=META_ROWS, bn=512, tiles_per_batch=1)

        big = _matmul(hr, w_big, 1024, 1024, BF16, "in_proj")
        logits = _matmul(hr, w_if, 1024, LANES, F32, "gate_logits")
        ya, = _mlstm(big, logits, bias_row, gain, (c0, m0), batches=batch, chunk=256, n_valid=256)
        yb, yc = _conv_pool(big, big_m, cw, pw, ps, bm=512, tiles_per_batch=seq // 512)
        mg = _branch(ya, yb, yc, wbr, big, bm=1024, bn=512)
        xr, hf = _proj_residual(mg, wo, xr, g_post_mix, g_pre_ffn, bm=256, bk=D_MODEL, name="out_proj")
        act, _ = _ffn_in(hf, wfi, fcw, a_meta, bm=1024, bn=512, tiles_per_batch=seq // 1024)
        xr, hr = _proj_residual(act, wfo, xr, g_post_ffn, g_next, bm=512, bk=D_FF // 4, name="ffn_out")
        if g_next is not None:
            xm, hm = _proj_residual(act_m, wfo, xm, g_post_ffn, g_next, bm=META_ROWS, bk=D_FF // 4,
                                    name="ffn_out")
    return xr.reshape(batch, seq, D_MODEL)
```

```python
import functools
import math

import jax
import jax.numpy as jnp
from jax import lax
from jax.experimental import pallas as pl
from jax.experimental.pallas import tpu as pltpu

D_MODEL = 2048
N_META = 16
MIX_WIDTH = 1024
N_BRANCH = 3
HEADS = 8
HEAD_DIM = MIX_WIDTH // HEADS
POOL_WINDOWS = (2, 4, 8, 16)
POOL_GROUP = MIX_WIDTH // len(POOL_WINDOWS)
D_FF = 5632
RMS_EPS = 1e-6
CONV_WIDTH = 3

COL_Q, COL_K, COL_V, COL_O, COL_CB, COL_CC, COL_CX, COL_PU = (n * MIX_WIDTH for n in range(8))
COL_GATES = 8 * MIX_WIDTH
BIG_COLS = COL_GATES + N_BRANCH * D_MODEL
LANES = 128
HALO = 16
META_ROWS = 128
VMEM_LIMIT_CAP = 56 * 1024 * 1024

F32 = jnp.float32
BF16 = jnp.bfloat16


def _params(semantics, vmem_bytes):
    return pltpu.CompilerParams(dimension_semantics=semantics,
                                vmem_limit_bytes=min(int(vmem_bytes), VMEM_LIMIT_CAP))


def _rms(x, g):
    return x * lax.rsqrt(jnp.mean(x * x, axis=-1, keepdims=True) + RMS_EPS) * g


def _norm_kernel(x_ref, g_ref, h_ref):
    h_ref[...] = _rms(x_ref[...], g_ref[...]).astype(BF16)


def _norm(x, g, bm):
    m = x.shape[0]
    return pl.pallas_call(
        _norm_kernel,
        grid=(m // bm,),
        in_specs=[pl.BlockSpec((bm, D_MODEL), lambda i: (i, 0)),
                  pl.BlockSpec((1, D_MODEL), lambda i: (0, 0))],
        out_specs=pl.BlockSpec((bm, D_MODEL), lambda i: (i, 0)),
        out_shape=jax.ShapeDtypeStruct((m, D_MODEL), BF16),
        compiler_params=_params(("parallel",), 10 * bm * D_MODEL * 4),
        name="pre_norm",
    )(x, g)


def _matmul_kernel(h_ref, w_ref, o_ref):
    o_ref[...] = jnp.dot(h_ref[...], w_ref[...], preferred_element_type=F32).astype(o_ref.dtype)


def _matmul(h, w, layer, bm, bn, out_dtype, name):
    m, k = h.shape
    n = w.shape[2]
    est = 2 * (bm * k * 2 + k * bn * 2 + bm * bn * 4) + 2 * bm * bn * 4
    return pl.pallas_call(
        _matmul_kernel,
        grid=(n // bn, m // bm),
        in_specs=[pl.BlockSpec((bm, k), lambda j, i: (i, 0)),
                  pl.BlockSpec((None, k, bn), lambda j, i: (layer, 0, j))],
        out_specs=pl.BlockSpec((bm, bn), lambda j, i: (i, j)),
        out_shape=jax.ShapeDtypeStruct((m, n), out_dtype),
        compiler_params=_params(("parallel", "parallel"), est),
        name=name,
    )(h, w)


def _shift_rows(x, d):
    return pltpu.roll(x, d, axis=0)


def _cumsum_rows(x):
    rows = x.shape[0]
    row = lax.broadcasted_iota(jnp.int32, x.shape, 0)
    d = 1
    while d < rows:
        x = x + jnp.where(row >= d, _shift_rows(x, d), 0.0)
        d *= 2
    return x


def _cummax_rows(x):
    rows = x.shape[0]
    row = lax.broadcasted_iota(jnp.int32, x.shape, 0)
    d = 1
    while d < rows:
        x = jnp.maximum(x, jnp.where(row >= d, _shift_rows(x, d), -jnp.inf))
        d *= 2
    return x


def _log_sigmoid(x):
    return jnp.minimum(x, 0.0) - jnp.log1p(jnp.exp(-jnp.abs(x)))


def _mlstm_kernel(*refs, chunk, n_valid, zero_init, emit_state):
    q_ref, k_ref, v_ref, o_ref, gl_ref, bias_ref, gain_ref = refs[:7]
    refs = refs[7:]
    if not zero_init:
        c0_ref, m0_ref = refs[:2]
        refs = refs[2:]
    y_ref = refs[0]
    refs = refs[1:]
    if emit_state:
        cout_ref, mout_ref = refs[:2]
        refs = refs[2:]
    c_scr, m_scr, s_scr, r_scr = refs

    @pl.when(pl.program_id(1) == 0)
    def _():
        if zero_init:
            c_scr[...] = jnp.zeros_like(c_scr)
            m_scr[...] = jnp.zeros_like(m_scr)
        else:
            c_scr[...] = c0_ref[...]
            m_scr[...] = m0_ref[...]

    scale = HEAD_DIM ** -0.5
    last = n_valid - 1
    gl = gl_ref[...] + bias_ref[...]
    logf = pltpu.roll(_log_sigmoid(gl), LANES - HEADS, axis=1)
    b = _cumsum_rows(logf)
    a = gl - b
    m_prev = m_scr[...]
    g = jnp.maximum(_cummax_rows(a), m_prev)
    inter_w = jnp.exp(m_prev - g) * scale
    exp_neg_m = jnp.exp(-(b + g))
    g_last = g[last:last + 1, :]
    decay = jnp.exp(m_prev - g_last)
    row1 = lax.broadcasted_iota(jnp.int32, a.shape, 0)
    ws = jnp.where(row1 <= last, jnp.exp(a - g_last), 0.0)
    m_scr[...] = b[last:last + 1, :] + g_last
    a_t = (a + math.log(scale)).T

    rows = lax.broadcasted_iota(jnp.int32, (chunk, chunk), 0)
    cols = lax.broadcasted_iota(jnp.int32, (chunk, chunk), 1)
    causal = rows >= cols
    ones = jnp.ones((chunk, HEAD_DIM), BF16)
    head = [slice(h * HEAD_DIM, (h + 1) * HEAD_DIM) for h in range(HEADS)]
    wide = [slice(h * 2 * HEAD_DIM, (h + 1) * 2 * HEAD_DIM) for h in range(HEADS)]

    for h in range(HEADS):
        w = jnp.where(causal, jnp.exp(a_t[h:h + 1, :] - g[:, h:h + 1]), 0.0)
        s = lax.dot_general(q_ref[:, head[h]], k_ref[:, head[h]], (((1,), (1,)), ((), ())),
                            preferred_element_type=F32)
        s_scr[h] = (s * w).astype(BF16)
    for h in range(HEADS):
        v_aug = jnp.concatenate([v_ref[:, head[h]], ones], axis=1)
        r_scr[:, wide[h]] = (
            jnp.dot(q_ref[:, head[h]], c_scr[h].astype(BF16), preferred_element_type=F32)
            * inter_w[:, h:h + 1]
            + jnp.dot(s_scr[h], v_aug, preferred_element_type=F32))
    for h in range(HEADS):
        v_aug = jnp.concatenate([v_ref[:, head[h]], ones], axis=1)
        kw = (k_ref[:, head[h]].astype(F32) * ws[:, h:h + 1]).astype(BF16)
        upd = lax.dot_general(kw, v_aug, (((0,), (0,)), ((), ())), preferred_element_type=F32)
        c_scr[h] = decay[:, h:h + 1] * c_scr[h] + upd
    for h in range(HEADS):
        r = r_scr[:, wide[h]]
        num, den = r[:, :HEAD_DIM], r[:, HEAD_DIM:]
        hh = num / jnp.maximum(jnp.abs(den), exp_neg_m[:, h:h + 1])
        hn = hh * lax.rsqrt(jnp.mean(hh * hh, axis=-1, keepdims=True) + RMS_EPS)
        y_ref[:, head[h]] = (hn * gain_ref[:, head[h]]
                             * jax.nn.sigmoid(o_ref[:, head[h]].astype(F32))).astype(BF16)

    if emit_state:
        cout_ref[...] = c_scr[...]
        mout_ref[...] = m_scr[...]


def _mlstm(big, logits, bias_row, gain, state, *, batches, chunk, n_valid):
    m = big.shape[0]
    n_chunks = m // (batches * chunk)
    zero_init = state is None

    def col(cblk):
        return pl.BlockSpec((chunk, MIX_WIDTH), lambda b, c: (b * n_chunks + c, cblk))

    in_specs = [col(0), col(1), col(2), col(3),
                pl.BlockSpec((chunk, LANES), lambda b, c: (b * n_chunks + c, 0)),
                pl.BlockSpec((1, LANES), lambda b, c: (0, 0)),
                pl.BlockSpec((1, MIX_WIDTH), lambda b, c: (0, 0))]
    args = [big, big, big, big, logits, bias_row, gain]
    c_shape = (HEADS, HEAD_DIM, 2 * HEAD_DIM)
    if not zero_init:
        in_specs += [pl.BlockSpec(c_shape, lambda b, c: (0, 0, 0)),
                     pl.BlockSpec((1, LANES), lambda b, c: (0, 0))]
        args += list(state)
    out_specs = [pl.BlockSpec((chunk, MIX_WIDTH), lambda b, c: (b * n_chunks + c, 0))]
    out_shape = [jax.ShapeDtypeStruct((m, MIX_WIDTH), BF16)]
    if zero_init:
        out_specs += [pl.BlockSpec(c_shape, lambda b, c: (0, 0, 0)),
                      pl.BlockSpec((1, LANES), lambda b, c: (0, 0))]
        out_shape += [jax.ShapeDtypeStruct(c_shape, F32), jax.ShapeDtypeStruct((1, LANES), F32)]
    est = 10 * chunk * MIX_WIDTH * 2 + 24 * chunk * chunk * 4 + 6 * HEADS * HEAD_DIM * 2 * HEAD_DIM * 4
    outs = pl.pallas_call(
        functools.partial(_mlstm_kernel, chunk=chunk, n_valid=n_valid, zero_init=zero_init,
                          emit_state=zero_init),
        grid=(batches, n_chunks),
        in_specs=in_specs,
        out_specs=out_specs,
        out_shape=out_shape,
        scratch_shapes=[pltpu.VMEM(c_shape, F32), pltpu.VMEM((1, LANES), F32),
                        pltpu.VMEM((HEADS, chunk, chunk), BF16),
                        pltpu.VMEM((chunk, HEADS * 2 * HEAD_DIM), F32)],
        compiler_params=_params(("arbitrary", "arbitrary"), est + 16 * 1024 * 1024),
        name="mlstm",
    )(*args)
    return outs


def _conv_pool_kernel(*refs, bm, tiles_per_batch, is_meta):
    cb_ref, cc_ref, cx_ref, pu_ref = refs[:4]
    refs = refs[4:]
    if not is_meta:
        ccp_ref, cxp_ref, pup_ref, ccm_ref, cxm_ref, pum_ref = refs[:6]
        refs = refs[6:]
    cw_ref, pw_ref, ps_ref, yb_ref, yc_ref, p_scr, u_scr = refs

    if is_meta:
        p_scr[0:HALO, :] = jnp.zeros((HALO, MIX_WIDTH), F32)
        u_scr[0:HALO, :] = jnp.zeros((HALO, MIX_WIDTH), F32)
    else:
        first = pl.program_id(0) % tiles_per_batch == 0
        cch = jnp.where(first, ccm_ref[...], ccp_ref[...]).astype(F32)
        cxh = jnp.where(first, cxm_ref[...], cxp_ref[...]).astype(F32)
        p_scr[0:HALO, :] = cch * cxh
        u_scr[0:HALO, :] = jnp.where(first, pum_ref[...], pup_ref[...]).astype(F32)
    p_scr[HALO:HALO + bm, :] = cc_ref[...].astype(F32) * cx_ref[...].astype(F32)
    u_scr[HALO:HALO + bm, :] = pu_ref[...].astype(F32)

    conv = cw_ref[2:3, :] * p_scr[HALO:HALO + bm, :]
    conv = conv + cw_ref[1:2, :] * p_scr[HALO - 1:HALO - 1 + bm, :]
    conv = conv + cw_ref[0:1, :] * p_scr[HALO - 2:HALO - 2 + bm, :]
    yb_ref[...] = (cb_ref[...].astype(F32) * conv).astype(BF16)

    for grp, win in enumerate(POOL_WINDOWS):
        sl = slice(grp * POOL_GROUP, (grp + 1) * POOL_GROUP)
        cur = u_scr[HALO:HALO + bm, sl]
        tot = cur
        for lag in range(1, win):
            tot = tot + u_scr[HALO - lag:HALO - lag + bm, sl]
        if is_meta:
            t = lax.broadcasted_iota(jnp.int32, (bm, 1), 0)
            pooled = tot / jnp.minimum(t + 1, win).astype(F32) - cur
        else:
            pooled = tot * (1.0 / win) - cur
        y = jnp.dot(pooled.astype(BF16), pw_ref[grp], preferred_element_type=F32)
        yc_ref[:, sl] = (y * ps_ref[:, sl]).astype(BF16)


def _conv_pool(big, big_meta, conv_w, pool_w, pool_scale, layer, *, bm, tiles_per_batch):
    m = big.shape[0]
    is_meta = big_meta is None
    cblk = [COL_CB // MIX_WIDTH, COL_CC // MIX_WIDTH, COL_CX // MIX_WIDTH, COL_PU // MIX_WIDTH]
    in_specs = [pl.BlockSpec((bm, MIX_WIDTH), functools.partial(lambda i, c: (i, c), c=c)) for c in cblk]
    args = [big] * 4
    if not is_meta:
        per = bm // HALO
        in_specs += [pl.BlockSpec((HALO, MIX_WIDTH),
                                  functools.partial(lambda i, c: (jnp.maximum(i * per - 1, 0), c), c=c))
                     for c in cblk[1:]]
        in_specs += [pl.BlockSpec((HALO, MIX_WIDTH), functools.partial(lambda i, c: (0, c), c=c))
                     for c in cblk[1:]]
        args += [big] * 3 + [big_meta] * 3
    in_specs += [pl.BlockSpec((None, CONV_WIDTH, MIX_WIDTH), lambda i: (layer, 0, 0)),
                 pl.BlockSpec((None, len(POOL_WINDOWS), POOL_GROUP, POOL_GROUP), lambda i: (layer, 0, 0, 0)),
                 pl.BlockSpec((1, MIX_WIDTH), lambda i: (0, 0))]
    args += [conv_w, pool_w, pool_scale]
    out_spec = pl.BlockSpec((bm, MIX_WIDTH), lambda i: (i, 0))
    return pl.pallas_call(
        functools.partial(_conv_pool_kernel, bm=bm, tiles_per_batch=tiles_per_batch, is_meta=is_meta),
        grid=(m // bm,),
        in_specs=in_specs,
        out_specs=[out_spec, out_spec],
        out_shape=[jax.ShapeDtypeStruct((m, MIX_WIDTH), BF16)] * 2,
        scratch_shapes=[pltpu.VMEM((HALO + bm, MIX_WIDTH), F32)] * 2,
        compiler_params=_params(("arbitrary",), 16 * bm * MIX_WIDTH * 4 + 8 * 1024 * 1024),
        name="conv_pool",
    )(*args)


def _branch_kernel(ya_ref, yb_ref, yc_ref, w_ref, g0_ref, g1_ref, g2_ref, o_ref):
    acc = None
    for y_ref, g_ref, n in ((ya_ref, g0_ref, 0), (yb_ref, g1_ref, 1), (yc_ref, g2_ref, 2)):
        br = jnp.dot(y_ref[...], w_ref[n], preferred_element_type=F32)
        term = jax.nn.sigmoid(g_ref[...].astype(F32)) * br
        acc = term if acc is None else acc + term
    o_ref[...] = acc.astype(BF16)


def _branch(ya, yb, yc, w_branch, big, layer, *, bm, bn):
    m = ya.shape[0]
    y_spec = pl.BlockSpec((bm, MIX_WIDTH), lambda j, i: (i, 0))
    g_specs = [pl.BlockSpec((bm, bn),
                            functools.partial(lambda j, i, off: (i, off + j), off=(COL_GATES + n * D_MODEL) // bn))
               for n in range(N_BRANCH)]
    est = 2 * (3 * bm * MIX_WIDTH * 2 + 3 * MIX_WIDTH * bn * 2 + 4 * bm * bn * 2) + 6 * bm * bn * 4
    return pl.pallas_call(
        _branch_kernel,
        grid=(D_MODEL // bn, m // bm),
        in_specs=[y_spec, y_spec, y_spec,
                  pl.BlockSpec((None, N_BRANCH, MIX_WIDTH, bn), lambda j, i: (layer, 0, 0, j))] + g_specs,
        out_specs=pl.BlockSpec((bm, bn), lambda j, i: (i, j)),
        out_shape=jax.ShapeDtypeStruct((m, D_MODEL), BF16),
        compiler_params=_params(("parallel", "parallel"), est),
        name="branch_merge",
    )(ya, yb, yc, w_branch, big, big, big)


def _proj_residual_kernel(*refs, n_n, sub, emit_h):
    a_ref, w_ref, x_ref, gpost_ref = refs[:4]
    refs = refs[4:]
    if emit_h:
        gnext_ref = refs[0]
        refs = refs[1:]
    xo_ref = refs[0]
    refs = refs[1:]
    if emit_h:
        ho_ref = refs[0]
        refs = refs[1:]
    y_scr = refs[0]
    n = pl.program_id(1)
    y_scr[n] = jnp.dot(a_ref[...], w_ref[...], preferred_element_type=F32)

    @pl.when(n == n_n - 1)
    def _():
        def rows_step(r, carry):
            rs = pl.ds(pl.multiple_of(r * sub, sub), sub)
            y = jnp.concatenate([y_scr[c, rs, :] for c in range(n_n)], axis=1)
            xn = x_ref[rs, :] + _rms(y, gpost_ref[...])
            xo_ref[rs, :] = xn
            if emit_h:
                ho_ref[rs, :] = _rms(xn, gnext_ref[...]).astype(BF16)
            return carry
        lax.fori_loop(0, y_scr.shape[1] // sub, rows_step, 0, unroll=4)


def _proj_residual(a, w, layer, x, g_post, g_next, *, bm, bn, name):
    m, kdim = a.shape
    n_n = D_MODEL // bn
    emit_h = g_next is not None
    row = lambda i, n: (i, 0)
    vec = pl.BlockSpec((1, D_MODEL), lambda i, n: (0, 0))
    in_specs = [pl.BlockSpec((bm, kdim), row),
                pl.BlockSpec((None, kdim, bn), lambda i, n: (layer, 0, n)),
                pl.BlockSpec((bm, D_MODEL), row), vec]
    args = [a, w, x, g_post]
    out_specs = [pl.BlockSpec((bm, D_MODEL), row)]
    out_shape = [jax.ShapeDtypeStruct((m, D_MODEL), F32)]
    if emit_h:
        in_specs.append(vec)
        args.append(g_next)
        out_specs.append(pl.BlockSpec((bm, D_MODEL), row))
        out_shape.append(jax.ShapeDtypeStruct((m, D_MODEL), BF16))
    est = (2 * (bm * kdim * 2 + kdim * bn * 2 + bm * D_MODEL * (4 + 4 + 2)) + bm * D_MODEL * 4
           + 2 * bm * bn * 4 + 2 * 1024 * 1024)
    outs = pl.pallas_call(
        functools.partial(_proj_residual_kernel, n_n=n_n, sub=min(bm, 64), emit_h=emit_h),
        grid=(m // bm, n_n),
        in_specs=in_specs,
        out_specs=out_specs,
        out_shape=out_shape,
        scratch_shapes=[pltpu.VMEM((n_n, bm, bn), F32)],
        compiler_params=_params(("parallel", "arbitrary"), est),
        name=name,
    )(*args)
    return outs if emit_h else (outs[0], None)


def _gelu_tanh(x):
    return 0.5 * x * (1.0 + jnp.tanh(0.7978845608028654 * (x + 0.044715 * (x * x * x))))


def _ffn_in_kernel(*refs, bm, tiles_per_batch, is_meta):
    h_ref, wa_ref, wu_ref, cw_ref = refs[:4]
    refs = refs[4:]
    if not is_meta:
        am_ref = refs[0]
        refs = refs[1:]
    o_ref = refs[0]
    refs = refs[1:]
    if is_meta:
        ao_ref = refs[0]
        refs = refs[1:]
    a_scr, tail_scr = refs
    sub = 8

    h = h_ref[...]
    a = jnp.dot(h, wa_ref[...], preferred_element_type=F32)
    u = jnp.dot(h, wu_ref[...], preferred_element_type=F32)
    if is_meta:
        a_scr[0:sub, :] = jnp.zeros((sub, a.shape[1]), F32)
        ao_ref[...] = a
    else:
        first = pl.program_id(1) % tiles_per_batch == 0
        a_scr[0:sub, :] = jnp.where(first, am_ref[...], tail_scr[...])
    a_scr[sub:sub + bm, :] = a
    tail_scr[...] = a[bm - sub:bm, :]
    conv = cw_ref[2:3, :] * a
    conv = conv + cw_ref[1:2, :] * a_scr[sub - 1:sub - 1 + bm, :]
    conv = conv + cw_ref[0:1, :] * a_scr[sub - 2:sub - 2 + bm, :]
    o_ref[...] = (_gelu_tanh(conv) * u).astype(BF16)


def _ffn_in(h, w_ffn_in, conv_w, a_meta, layer, *, bm, bn, tiles_per_batch):
    m = h.shape[0]
    is_meta = a_meta is None
    n_j = D_FF // bn
    in_specs = [pl.BlockSpec((bm, D_MODEL), lambda j, i: (i, 0)),
                pl.BlockSpec((None, D_MODEL, bn), lambda j, i: (layer, 0, j)),
                pl.BlockSpec((None, D_MODEL, bn), lambda j, i: (layer, 0, j + n_j)),
                pl.BlockSpec((None, CONV_WIDTH, bn), lambda j, i: (layer, 0, j))]
    args = [h, w_ffn_in, w_ffn_in, conv_w]
    if not is_meta:
        in_specs.append(pl.BlockSpec((8, bn), lambda j, i: (N_META // 8 - 1, j)))
        args.append(a_meta)
    out_specs = [pl.BlockSpec((bm, bn), lambda j, i: (i, j))]
    out_shape = [jax.ShapeDtypeStruct((m, D_FF), BF16)]
    if is_meta:
        out_specs.append(pl.BlockSpec((bm, bn), lambda j, i: (i, j)))
        out_shape.append(jax.ShapeDtypeStruct((m, D_FF), F32))
    est = 2 * (bm * D_MODEL * 2 + 2 * D_MODEL * bn * 2 + bm * bn * 6) + 8 * bm * bn * 4
    outs = pl.pallas_call(
        functools.partial(_ffn_in_kernel, bm=bm, tiles_per_batch=tiles_per_batch, is_meta=is_meta),
        grid=(n_j, m // bm),
        in_specs=in_specs,
        out_specs=out_specs,
        out_shape=out_shape,
        scratch_shapes=[pltpu.VMEM((8 + bm, bn), F32), pltpu.VMEM((8, bn), F32)],
        compiler_params=_params(("arbitrary", "arbitrary"), est),
        name="ffn_in",
    )(*args)
    return outs if is_meta else (outs[0], None)


def kernel(x, meta_tokens, norm_pre_mix, norm_post_mix, norm_pre_ffn, norm_post_ffn, w_in, b_if,
           mlstm_head_gain, conv_mix_w, pool_w, pool_scale, w_branch, w_out, w_ffn_in, ffn_conv_w,
           w_ffn_out):
    batch, seq, _ = x.shape
    depth = w_in.shape[0]
    m = batch * seq
    xr = x.reshape(m, D_MODEL)
    xm = jnp.pad(meta_tokens.astype(F32), ((0, META_ROWS - N_META), (0, 0)))

    if_lo = 4 * MIX_WIDTH
    if_hi = if_lo + 2 * HEADS
    w_big = jnp.concatenate([w_in[:, :, :if_lo].astype(BF16), w_in[:, :, if_hi:].astype(BF16)], axis=2)
    w_if = jnp.pad(w_in[:, :, if_lo:if_hi].astype(BF16), ((0, 0), (0, 0), (0, LANES - 2 * HEADS)))
    pw = pool_w.astype(BF16)
    wbr = w_branch.astype(BF16)
    wo = w_out.astype(BF16)
    wfi = w_ffn_in.astype(BF16)
    wfo = w_ffn_out.astype(BF16)
    cw = conv_mix_w.astype(F32)
    fcw = ffn_conv_w.astype(F32)

    def vec(v):
        return v.reshape(1, -1).astype(F32)

    hr = _norm(xr, vec(norm_pre_mix[0]), 512)
    hm = _norm(xm, vec(norm_pre_mix[0]), META_ROWS)
    for l in range(depth):
        bias_row = jnp.pad(b_if[l].reshape(1, 2 * HEADS).astype(F32), ((0, 0), (0, LANES - 2 * HEADS)))
        gain = vec(mlstm_head_gain[l])
        ps = vec(pool_scale[l])
        g_post_mix, g_pre_ffn, g_post_ffn = vec(norm_post_mix[l]), vec(norm_pre_ffn[l]), vec(norm_post_ffn[l])
        g_next = vec(norm_pre_mix[l + 1]) if l + 1 < depth else None

        big_m = _matmul(hm, w_big, l, META_ROWS, 1024, BF16, "in_proj")
        log_m = _matmul(hm, w_if, l, META_ROWS, LANES, F32, "gate_logits")
        ya_m, c0, m0 = _mlstm(big_m, log_m, bias_row, gain, None, batches=1, chunk=META_ROWS, n_valid=N_META)
        yb_m, yc_m = _conv_pool(big_m, None, cw, pw, ps, l, bm=META_ROWS, tiles_per_batch=1)
        mg_m = _branch(ya_m, yb_m, yc_m, wbr, big_m, l, bm=META_ROWS, bn=512)
        xm, hf_m = _proj_residual(mg_m, wo, l, xm, g_post_mix, g_pre_ffn, bm=META_ROWS, bn=1024,
                                  name="out_proj")
        act_m, a_meta = _ffn_in(hf_m, wfi, fcw, None, l, bm=META_ROWS, bn=512, tiles_per_batch=1)

        big = _matmul(hr, w_big, l, 1024, 1024, BF16, "in_proj")
        logits = _matmul(hr, w_if, l, 1024, LANES, F32, "gate_logits")
        ya, = _mlstm(big, logits, bias_row, gain, (c0, m0), batches=batch, chunk=256, n_valid=256)
        yb, yc = _conv_pool(big, big_m, cw, pw, ps, l, bm=512, tiles_per_batch=seq // 512)
        mg = _branch(ya, yb, yc, wbr, big, l, bm=1024, bn=512)
        xr, hf = _proj_residual(mg, wo, l, xr, g_post_mix, g_pre_ffn, bm=512, bn=1024, name="out_proj")
        act, _ = _ffn_in(hf, wfi, fcw, a_meta, l, bm=1024, bn=512, tiles_per_batch=seq // 1024)
        xr, hr = _proj_residual(act, wfo, l, xr, g_post_ffn, g_next, bm=512, bn=512, name="ffn_out")
        if g_next is not None:
            xm, hm = _proj_residual(act_m, wfo, l, xm, g_post_ffn, g_next, bm=META_ROWS, bn=512,
                                    name="ffn_out")
    return xr.reshape(batch, seq, D_MODEL)
```

```python
import functools
import math

import jax
import jax.numpy as jnp
from jax import lax
from jax.experimental import pallas as pl
from jax.experimental.pallas import tpu as pltpu

D_MODEL = 2048
N_META = 16
MIX_WIDTH = 1024
N_BRANCH = 3
HEADS = 8
HEAD_DIM = MIX_WIDTH // HEADS
POOL_WINDOWS = (2, 4, 8, 16)
POOL_GROUP = MIX_WIDTH // len(POOL_WINDOWS)
D_FF = 5632
RMS_EPS = 1e-6
CONV_WIDTH = 3

COL_Q, COL_K, COL_V, COL_O, COL_CB, COL_CC, COL_CX, COL_PU = (n * MIX_WIDTH for n in range(8))
COL_GATES = 8 * MIX_WIDTH
BIG_COLS = COL_GATES + N_BRANCH * D_MODEL
N_LOGITS = 2 * HEADS
LANES = 128
HALO = 16
SUBLANES = 8
META_ROWS = 128
MLSTM_CHUNK = 256
VMEM_LIMIT_CAP = 56 * 1024 * 1024

F32 = jnp.float32
BF16 = jnp.bfloat16


def _params(semantics, vmem_bytes):
    return pltpu.CompilerParams(dimension_semantics=semantics,
                                vmem_limit_bytes=min(int(vmem_bytes), VMEM_LIMIT_CAP))


def _rms(x, g):
    return x * lax.rsqrt(jnp.mean(x * x, axis=-1, keepdims=True) + RMS_EPS) * g


def _norm_kernel(x_ref, g_ref, h_ref):
    h_ref[...] = _rms(x_ref[...], g_ref[...]).astype(BF16)


def _norm(x, g, bm):
    m = x.shape[0]
    return pl.pallas_call(
        _norm_kernel,
        grid=(m // bm,),
        in_specs=[pl.BlockSpec((bm, D_MODEL), lambda i: (i, 0)),
                  pl.BlockSpec((1, D_MODEL), lambda i: (0, 0))],
        out_specs=pl.BlockSpec((bm, D_MODEL), lambda i: (i, 0)),
        out_shape=jax.ShapeDtypeStruct((m, D_MODEL), BF16),
        compiler_params=_params(("parallel",), 10 * bm * D_MODEL * 4),
        name="pre_norm",
    )(x, g)


def _in_proj_kernel(h_ref, hm_ref, wa_ref, wb_ref, o_ref, om_ref, w_scr, *, bn):
    j = pl.program_id(0)

    @pl.when(pl.program_id(1) == 0)
    def _():
        @pl.when(j < COL_CB // bn)
        def _():
            w_scr[...] = wa_ref[...].astype(BF16)

        @pl.when(j >= COL_CB // bn)
        def _():
            keep = LANES - N_LOGITS
            lane = lax.broadcasted_iota(jnp.int32, (wa_ref.shape[0], LANES), 1)
            n_t = bn // LANES
            rolled = pltpu.roll(wa_ref[:, 0:LANES], keep, axis=1)
            for t in range(n_t):
                nxt = wa_ref[:, (t + 1) * LANES:(t + 2) * LANES] if t + 1 < n_t else wb_ref[...]
                nxt_rolled = pltpu.roll(nxt, keep, axis=1)
                w_scr[:, t * LANES:(t + 1) * LANES] = jnp.where(lane < keep, rolled, nxt_rolled).astype(BF16)
                rolled = nxt_rolled

        om_ref[...] = jnp.dot(hm_ref[...], w_scr[...], preferred_element_type=F32).astype(BF16)

    o_ref[...] = jnp.dot(h_ref[...], w_scr[...], preferred_element_type=F32).astype(BF16)


def _in_proj(h, hm, w_in, layer, *, bm, bn):
    m, k = h.shape
    per = bn // LANES
    est = 2 * (bm * k * 2 + k * bn * 4 + k * LANES * 4 + bm * bn * 2) + k * bn * 2 + 2 * bm * bn * 4 + (4 << 20)
    return pl.pallas_call(
        functools.partial(_in_proj_kernel, bn=bn),
        grid=(BIG_COLS // bn, m // bm),
        in_specs=[pl.BlockSpec((bm, k), lambda j, i: (i, 0)),
                  pl.BlockSpec((META_ROWS, k), lambda j, i: (0, 0)),
                  pl.BlockSpec((None, k, bn), lambda j, i: (layer, 0, j)),
                  pl.BlockSpec((None, k, LANES), lambda j, i: (layer, 0, (j + 1) * per))],
        out_specs=[pl.BlockSpec((bm, bn), lambda j, i: (i, j)),
                   pl.BlockSpec((META_ROWS, bn), lambda j, i: (0, j))],
        out_shape=[jax.ShapeDtypeStruct((m, BIG_COLS), BF16),
                   jax.ShapeDtypeStruct((META_ROWS, BIG_COLS), BF16)],
        scratch_shapes=[pltpu.VMEM((k, bn), BF16)],
        compiler_params=_params(("arbitrary", "arbitrary"), est),
        name="in_proj",
    )(h, hm, w_in, w_in)


def _gate_logits_kernel(h_ref, hm_ref, w_ref, o_ref, om_ref, w_scr):
    @pl.when(pl.program_id(0) == 0)
    def _():
        w_scr[...] = w_ref[...].astype(BF16)
        om_ref[...] = jnp.dot(hm_ref[...], w_scr[...], preferred_element_type=F32)

    o_ref[...] = jnp.dot(h_ref[...], w_scr[...], preferred_element_type=F32)


def _gate_logits(h, hm, w_in, layer, *, bm):
    m, k = h.shape
    return pl.pallas_call(
        _gate_logits_kernel,
        grid=(m // bm,),
        in_specs=[pl.BlockSpec((bm, k), lambda i: (i, 0)),
                  pl.BlockSpec((META_ROWS, k), lambda i: (0, 0)),
                  pl.BlockSpec((None, k, LANES), lambda i: (layer, 0, COL_CB // LANES))],
        out_specs=[pl.BlockSpec((bm, LANES), lambda i: (i, 0)),
                   pl.BlockSpec((META_ROWS, LANES), lambda i: (0, 0))],
        out_shape=[jax.ShapeDtypeStruct((m, LANES), F32), jax.ShapeDtypeStruct((META_ROWS, LANES), F32)],
        scratch_shapes=[pltpu.VMEM((k, LANES), BF16)],
        compiler_params=_params(("arbitrary",), 4 * bm * k * 2 + (8 << 20)),
        name="gate_logits",
    )(h, hm, w_in)


def _shift_rows(x, d):
    return pltpu.roll(x, d, axis=0)


def _cumsum_rows(x):
    rows = x.shape[0]
    row = lax.broadcasted_iota(jnp.int32, x.shape, 0)
    d = 1
    while d < rows:
        x = x + jnp.where(row >= d, _shift_rows(x, d), 0.0)
        d *= 2
    return x


def _cummax_rows(x):
    rows = x.shape[0]
    row = lax.broadcasted_iota(jnp.int32, x.shape, 0)
    d = 1
    while d < rows:
        x = jnp.maximum(x, jnp.where(row >= d, _shift_rows(x, d), -jnp.inf))
        d *= 2
    return x


def _log_sigmoid(x):
    return jnp.minimum(x, 0.0) - jnp.log1p(jnp.exp(-jnp.abs(x)))


def _mlstm_chunk(q_ref, k_ref, v_ref, o_ref, gl_ref, bias_ref, gain_ref, y_ref,
                 c_scr, m_scr, s_scr, r_scr, *, chunk, n_valid):
    scale = HEAD_DIM ** -0.5
    last = n_valid - 1
    gl = gl_ref[...] + bias_ref[...]
    logf = pltpu.roll(_log_sigmoid(gl), LANES - HEADS, axis=1)
    b = _cumsum_rows(logf)
    a = gl - b
    m_prev = m_scr[...]
    g = jnp.maximum(_cummax_rows(a), m_prev)
    inter_w = jnp.exp(m_prev - g) * scale
    exp_neg_m = jnp.exp(-(b + g))
    g_last = g[last:last + 1, :]
    decay = jnp.exp(m_prev - g_last)
    row1 = lax.broadcasted_iota(jnp.int32, a.shape, 0)
    ws = jnp.where(row1 <= last, jnp.exp(a - g_last), 0.0)
    m_scr[...] = b[last:last + 1, :] + g_last
    a_t = (a + math.log(scale)).T

    rows = lax.broadcasted_iota(jnp.int32, (chunk, chunk), 0)
    cols = lax.broadcasted_iota(jnp.int32, (chunk, chunk), 1)
    causal = rows >= cols
    ones = jnp.ones((chunk, HEAD_DIM), BF16)
    head = [slice(h * HEAD_DIM, (h + 1) * HEAD_DIM) for h in range(HEADS)]
    wide = [slice(h * 2 * HEAD_DIM, (h + 1) * 2 * HEAD_DIM) for h in range(HEADS)]

    for h in range(HEADS):
        w = jnp.where(causal, jnp.exp(a_t[h:h + 1, :] - g[:, h:h + 1]), 0.0)
        s = lax.dot_general(q_ref[:, head[h]], k_ref[:, head[h]], (((1,), (1,)), ((), ())),
                            preferred_element_type=F32)
        s_scr[h, 0:chunk, 0:chunk] = (s * w).astype(BF16)
    for h in range(HEADS):
        v_aug = jnp.concatenate([v_ref[:, head[h]], ones], axis=1)
        r_scr[0:chunk, wide[h]] = (
            jnp.dot(q_ref[:, head[h]], c_scr[h].astype(BF16), preferred_element_type=F32)
            * inter_w[:, h:h + 1]
            + jnp.dot(s_scr[h, 0:chunk, 0:chunk], v_aug, preferred_element_type=F32))
    for h in range(HEADS):
        v_aug = jnp.concatenate([v_ref[:, head[h]], ones], axis=1)
        kw = (k_ref[:, head[h]].astype(F32) * ws[:, h:h + 1]).astype(BF16)
        upd = lax.dot_general(kw, v_aug, (((0,), (0,)), ((), ())), preferred_element_type=F32)
        c_scr[h] = decay[:, h:h + 1] * c_scr[h] + upd
    for h in range(HEADS):
        r = r_scr[0:chunk, wide[h]]
        num, den = r[:, :HEAD_DIM], r[:, HEAD_DIM:]
        hh = num / jnp.maximum(jnp.abs(den), exp_neg_m[:, h:h + 1])
        hn = hh * lax.rsqrt(jnp.mean(hh * hh, axis=-1, keepdims=True) + RMS_EPS)
        y_ref[:, head[h]] = (hn * gain_ref[:, head[h]]
                             * jax.nn.sigmoid(o_ref[:, head[h]].astype(F32))).astype(BF16)


def _mlstm_kernel(q_ref, k_ref, v_ref, o_ref, gl_ref, qm_ref, km_ref, vm_ref, om_ref, glm_ref,
                  bias_ref, gain_ref, y_ref, ym_ref, c_scr, m_scr, c0_scr, m0_scr, s_scr, r_scr, *, chunk):
    first_chunk = pl.program_id(1) == 0

    @pl.when(jnp.logical_and(pl.program_id(0) == 0, first_chunk))
    def _():
        c_scr[...] = jnp.zeros_like(c_scr)
        m_scr[...] = jnp.zeros_like(m_scr)
        _mlstm_chunk(qm_ref, km_ref, vm_ref, om_ref, glm_ref, bias_ref, gain_ref, ym_ref,
                     c_scr, m_scr, s_scr, r_scr, chunk=META_ROWS, n_valid=N_META)
        c0_scr[...] = c_scr[...]
        m0_scr[...] = m_scr[...]

    @pl.when(first_chunk)
    def _():
        c_scr[...] = c0_scr[...]
        m_scr[...] = m0_scr[...]

    _mlstm_chunk(q_ref, k_ref, v_ref, o_ref, gl_ref, bias_ref, gain_ref, y_ref,
                 c_scr, m_scr, s_scr, r_scr, chunk=chunk, n_valid=chunk)


def _mlstm(big, big_m, logits, logits_m, bias_row, gain, *, batches, chunk):
    m = big.shape[0]
    n_chunks = m // (batches * chunk)
    cblk = [COL_Q // MIX_WIDTH, COL_K // MIX_WIDTH, COL_V // MIX_WIDTH, COL_O // MIX_WIDTH]
    main = [pl.BlockSpec((chunk, MIX_WIDTH), functools.partial(lambda b, c, cb: (b * n_chunks + c, cb), cb=cb))
            for cb in cblk]
    meta = [pl.BlockSpec((META_ROWS, MIX_WIDTH), functools.partial(lambda b, c, cb: (0, cb), cb=cb))
            for cb in cblk]
    in_specs = (main + [pl.BlockSpec((chunk, LANES), lambda b, c: (b * n_chunks + c, 0))]
                + meta + [pl.BlockSpec((META_ROWS, LANES), lambda b, c: (0, 0)),
                          pl.BlockSpec((1, LANES), lambda b, c: (0, 0)),
                          pl.BlockSpec((1, MIX_WIDTH), lambda b, c: (0, 0))])
    c_shape = (HEADS, HEAD_DIM, 2 * HEAD_DIM)
    est = 14 * chunk * MIX_WIDTH * 2 + 24 * chunk * chunk * 4 + 8 * HEADS * HEAD_DIM * 2 * HEAD_DIM * 4
    return pl.pallas_call(
        functools.partial(_mlstm_kernel, chunk=chunk),
        grid=(batches, n_chunks),
        in_specs=in_specs,
        out_specs=[pl.BlockSpec((chunk, MIX_WIDTH), lambda b, c: (b * n_chunks + c, 0)),
                   pl.BlockSpec((META_ROWS, MIX_WIDTH), lambda b, c: (0, 0))],
        out_shape=[jax.ShapeDtypeStruct((m, MIX_WIDTH), BF16),
                   jax.ShapeDtypeStruct((META_ROWS, MIX_WIDTH), BF16)],
        scratch_shapes=[pltpu.VMEM(c_shape, F32), pltpu.VMEM((1, LANES), F32),
                        pltpu.VMEM(c_shape, F32), pltpu.VMEM((1, LANES), F32),
                        pltpu.VMEM((HEADS, chunk, chunk), BF16),
                        pltpu.VMEM((chunk, HEADS * 2 * HEAD_DIM), F32)],
        compiler_params=_params(("arbitrary", "arbitrary"), est + (16 << 20)),
        name="mlstm",
    )(big, big, big, big, logits, big_m, big_m, big_m, big_m, logits_m, bias_row, gain)


def _conv_pool_tile(cb_ref, cc_ref, cx_ref, pu_ref, halo, cw_ref, pw_ref, ps_ref, yb_ref, yc_ref,
                    p_scr, u_scr, *, rows, from_start):
    if halo is None:
        p_scr[0:HALO, :] = jnp.zeros((HALO, MIX_WIDTH), F32)
        u_scr[0:HALO, :] = jnp.zeros((HALO, MIX_WIDTH), F32)
    else:
        p_scr[0:HALO, :] = halo[0]
        u_scr[0:HALO, :] = halo[1]
    p_scr[HALO:HALO + rows, :] = cc_ref[...].astype(F32) * cx_ref[...].astype(F32)
    u_scr[HALO:HALO + rows, :] = pu_ref[...].astype(F32)

    conv = cw_ref[2:3, :] * p_scr[HALO:HALO + rows, :]
    conv = conv + cw_ref[1:2, :] * p_scr[HALO - 1:HALO - 1 + rows, :]
    conv = conv + cw_ref[0:1, :] * p_scr[HALO - 2:HALO - 2 + rows, :]
    yb_ref[...] = (cb_ref[...].astype(F32) * conv).astype(BF16)

    for grp, win in enumerate(POOL_WINDOWS):
        sl = slice(grp * POOL_GROUP, (grp + 1) * POOL_GROUP)
        cur = u_scr[HALO:HALO + rows, sl]
        tot = cur
        for lag in range(1, win):
            tot = tot + u_scr[HALO - lag:HALO - lag + rows, sl]
        if from_start:
            t = lax.broadcasted_iota(jnp.int32, (rows, 1), 0)
            pooled = tot / jnp.minimum(t + 1, win).astype(F32) - cur
        else:
            pooled = tot * (1.0 / win) - cur
        y = jnp.dot(pooled.astype(BF16), pw_ref[grp], preferred_element_type=F32)
        yc_ref[:, sl] = (y * ps_ref[:, sl]).astype(BF16)


def _conv_pool_kernel(cb_ref, cc_ref, cx_ref, pu_ref, ccp_ref, cxp_ref, pup_ref,
                      cbm_ref, ccm_ref, cxm_ref, pum_ref, cw_ref, pw_ref, ps_ref,
                      yb_ref, yc_ref, ybm_ref, ycm_ref, pw_scr, p_scr, u_scr, *, bm, tiles_per_batch):
    @pl.when(pl.program_id(0) == 0)
    def _():
        pw_scr[...] = pw_ref[...].astype(BF16)
        _conv_pool_tile(cbm_ref, ccm_ref, cxm_ref, pum_ref, None, cw_ref, pw_scr, ps_ref,
                        ybm_ref, ycm_ref, p_scr, u_scr, rows=META_ROWS, from_start=True)

    first = pl.program_id(0) % tiles_per_batch == 0
    cch = jnp.where(first, ccm_ref[0:HALO, :], ccp_ref[...]).astype(F32)
    cxh = jnp.where(first, cxm_ref[0:HALO, :], cxp_ref[...]).astype(F32)
    puh = jnp.where(first, pum_ref[0:HALO, :], pup_ref[...]).astype(F32)
    _conv_pool_tile(cb_ref, cc_ref, cx_ref, pu_ref, (cch * cxh, puh), cw_ref, pw_scr, ps_ref,
                    yb_ref, yc_ref, p_scr, u_scr, rows=bm, from_start=False)


def _conv_pool(big, big_m, conv_w, pool_w, pool_scale, layer, *, bm, tiles_per_batch):
    m = big.shape[0]
    cblk = [COL_CB // MIX_WIDTH, COL_CC // MIX_WIDTH, COL_CX // MIX_WIDTH, COL_PU // MIX_WIDTH]
    per = bm // HALO
    in_specs = [pl.BlockSpec((bm, MIX_WIDTH), functools.partial(lambda i, c: (i, c), c=c)) for c in cblk]
    in_specs += [pl.BlockSpec((HALO, MIX_WIDTH),
                              functools.partial(lambda i, c: (jnp.maximum(i * per - 1, 0), c), c=c))
                 for c in cblk[1:]]
    in_specs += [pl.BlockSpec((META_ROWS, MIX_WIDTH), functools.partial(lambda i, c: (0, c), c=c))
                 for c in cblk]
    in_specs += [pl.BlockSpec((None, CONV_WIDTH, MIX_WIDTH), lambda i: (layer, 0, 0)),
                 pl.BlockSpec((None, len(POOL_WINDOWS), POOL_GROUP, POOL_GROUP), lambda i: (layer, 0, 0, 0)),
                 pl.BlockSpec((1, MIX_WIDTH), lambda i: (0, 0))]
    out_spec = pl.BlockSpec((bm, MIX_WIDTH), lambda i: (i, 0))
    meta_spec = pl.BlockSpec((META_ROWS, MIX_WIDTH), lambda i: (0, 0))
    return pl.pallas_call(
        functools.partial(_conv_pool_kernel, bm=bm, tiles_per_batch=tiles_per_batch),
        grid=(m // bm,),
        in_specs=in_specs,
        out_specs=[out_spec, out_spec, meta_spec, meta_spec],
        out_shape=[jax.ShapeDtypeStruct((m, MIX_WIDTH), BF16)] * 2
        + [jax.ShapeDtypeStruct((META_ROWS, MIX_WIDTH), BF16)] * 2,
        scratch_shapes=[pltpu.VMEM((len(POOL_WINDOWS), POOL_GROUP, POOL_GROUP), BF16),
                        pltpu.VMEM((HALO + bm, MIX_WIDTH), F32), pltpu.VMEM((HALO + bm, MIX_WIDTH), F32)],
        compiler_params=_params(("arbitrary",), 16 * bm * MIX_WIDTH * 4 + (10 << 20)),
        name="conv_pool",
    )(big, big, big, big, big, big, big, big_m, big_m, big_m, big_m, conv_w, pool_w, pool_scale)


def _branch_tile(ya_ref, yb_ref, yc_ref, w_scr, g0_ref, g1_ref, g2_ref, o_ref):
    acc = None
    for y_ref, g_ref, n in ((ya_ref, g0_ref, 0), (yb_ref, g1_ref, 1), (yc_ref, g2_ref, 2)):
        br = jnp.dot(y_ref[...], w_scr[n], preferred_element_type=F32)
        term = jax.nn.sigmoid(g_ref[...].astype(F32)) * br
        acc = term if acc is None else acc + term
    o_ref[...] = acc.astype(BF16)


def _branch_kernel(ya_ref, yb_ref, yc_ref, g0_ref, g1_ref, g2_ref,
                   yam_ref, ybm_ref, ycm_ref, g0m_ref, g1m_ref, g2m_ref, w_ref, o_ref, om_ref, w_scr):
    @pl.when(pl.program_id(1) == 0)
    def _():
        w_scr[...] = w_ref[...].astype(BF16)
        _branch_tile(yam_ref, ybm_ref, ycm_ref, w_scr, g0m_ref, g1m_ref, g2m_ref, om_ref)

    _branch_tile(ya_ref, yb_ref, yc_ref, w_scr, g0_ref, g1_ref, g2_ref, o_ref)


def _branch(ys, ys_m, w_branch, big, big_m, layer, *, bm, bn):
    m = big.shape[0]
    offs = [(COL_GATES + n * D_MODEL) // bn for n in range(N_BRANCH)]
    y_spec = pl.BlockSpec((bm, MIX_WIDTH), lambda j, i: (i, 0))
    ym_spec = pl.BlockSpec((META_ROWS, MIX_WIDTH), lambda j, i: (0, 0))
    g_specs = [pl.BlockSpec((bm, bn), functools.partial(lambda j, i, off: (i, off + j), off=off)) for off in offs]
    gm_specs = [pl.BlockSpec((META_ROWS, bn), functools.partial(lambda j, i, off: (0, off + j), off=off))
                for off in offs]
    est = (2 * (3 * bm * MIX_WIDTH * 2 + 3 * MIX_WIDTH * bn * 4 + 4 * bm * bn * 2) + 3 * MIX_WIDTH * bn * 2
           + 6 * bm * bn * 4 + (4 << 20))
    return pl.pallas_call(
        _branch_kernel,
        grid=(D_MODEL // bn, m // bm),
        in_specs=[y_spec] * 3 + g_specs + [ym_spec] * 3 + gm_specs
        + [pl.BlockSpec((None, N_BRANCH, MIX_WIDTH, bn), lambda j, i: (layer, 0, 0, j))],
        out_specs=[pl.BlockSpec((bm, bn), lambda j, i: (i, j)),
                   pl.BlockSpec((META_ROWS, bn), lambda j, i: (0, j))],
        out_shape=[jax.ShapeDtypeStruct((m, D_MODEL), BF16), jax.ShapeDtypeStruct((META_ROWS, D_MODEL), BF16)],
        scratch_shapes=[pltpu.VMEM((N_BRANCH, MIX_WIDTH, bn), BF16)],
        compiler_params=_params(("arbitrary", "arbitrary"), est),
        name="branch_merge",
    )(*ys, big, big, big, *ys_m, big_m, big_m, big_m, w_branch)


def _finish_rows(y_scr, x_ref, gpost_ref, gnext_ref, xo_ref, ho_ref, *, sub):
    n_n, rows, _ = y_scr.shape

    def rows_step(r, carry):
        rs = pl.ds(pl.multiple_of(r * sub, sub), sub)
        y = jnp.concatenate([y_scr[c, rs, :] for c in range(n_n)], axis=1)
        xn = x_ref[rs, :] + _rms(y, gpost_ref[...])
        xo_ref[rs, :] = xn
        if ho_ref is not None:
            ho_ref[rs, :] = _rms(xn, gnext_ref[...]).astype(BF16)
        return carry
    lax.fori_loop(0, rows // sub, rows_step, 0, unroll=min(4, rows // sub))


def _proj_residual_kernel(*refs, n_n, emit_h, with_meta):
    a_ref, w_ref, x_ref, gpost_ref = refs[:4]
    refs = refs[4:]
    gnext_ref = None
    if emit_h:
        gnext_ref = refs[0]
        refs = refs[1:]
    if with_meta:
        am_ref, xm_ref = refs[:2]
        refs = refs[2:]
    xo_ref = refs[0]
    refs = refs[1:]
    ho_ref = None
    if emit_h:
        ho_ref = refs[0]
        refs = refs[1:]
    if with_meta:
        xmo_ref = refs[0]
        refs = refs[1:]
        hmo_ref = None
        if emit_h:
            hmo_ref = refs[0]
            refs = refs[1:]
        y_scr, ym_scr = refs
    else:
        y_scr, = refs
    n = pl.program_id(1)
    sub = 64

    if with_meta:
        @pl.when(pl.program_id(0) == 0)
        def _():
            ym_scr[n] = jnp.dot(am_ref[...], w_ref[...], preferred_element_type=F32)

            @pl.when(n == n_n - 1)
            def _():
                _finish_rows(ym_scr, xm_ref, gpost_ref, gnext_ref, xmo_ref, hmo_ref, sub=sub)

    y_scr[n] = jnp.dot(a_ref[...], w_ref[...], preferred_element_type=F32)

    @pl.when(n == n_n - 1)
    def _():
        _finish_rows(y_scr, x_ref, gpost_ref, gnext_ref, xo_ref, ho_ref, sub=sub)


def _proj_residual(a, a_m, w, layer, x, x_m, g_post, g_next, *, bm, bn, name):
    m, kdim = a.shape
    n_n = D_MODEL // bn
    emit_h = g_next is not None
    with_meta = a_m is not None
    row = lambda i, n: (i, 0)
    const = lambda i, n: (0, 0)
    vec = pl.BlockSpec((1, D_MODEL), const)
    w_mode = dict(pipeline_mode=pl.Buffered(1)) if n_n == 1 else {}
    in_specs = [pl.BlockSpec((bm, kdim), row),
                pl.BlockSpec((None, kdim, bn), lambda i, n: (layer, 0, n), **w_mode),
                pl.BlockSpec((bm, D_MODEL), row), vec]
    args = [a, w, x, g_post]
    if emit_h:
        in_specs.append(vec)
        args.append(g_next)
    if with_meta:
        in_specs += [pl.BlockSpec((META_ROWS, kdim), const), pl.BlockSpec((META_ROWS, D_MODEL), const)]
        args += [a_m, x_m]
    out_specs = [pl.BlockSpec((bm, D_MODEL), row)]
    out_shape = [jax.ShapeDtypeStruct((m, D_MODEL), F32)]
    if emit_h:
        out_specs.append(pl.BlockSpec((bm, D_MODEL), row))
        out_shape.append(jax.ShapeDtypeStruct((m, D_MODEL), BF16))
    scratch = [pltpu.VMEM((n_n, bm, bn), F32)]
    if with_meta:
        out_specs.append(pl.BlockSpec((META_ROWS, D_MODEL), const))
        out_shape.append(jax.ShapeDtypeStruct((META_ROWS, D_MODEL), F32))
        if emit_h:
            out_specs.append(pl.BlockSpec((META_ROWS, D_MODEL), const))
            out_shape.append(jax.ShapeDtypeStruct((META_ROWS, D_MODEL), BF16))
        scratch.append(pltpu.VMEM((n_n, META_ROWS, bn), F32))
    w_bufs = 1 if n_n == 1 else 2
    est = (2 * (bm * kdim * 2 + bm * D_MODEL * (4 + 4 + 2)) + w_bufs * kdim * bn * 2 + bm * D_MODEL * 4
           + 2 * bm * bn * 4 + 2 * META_ROWS * (kdim * 2 + 3 * D_MODEL * 4) + (2 << 20))
    outs = pl.pallas_call(
        functools.partial(_proj_residual_kernel, n_n=n_n, emit_h=emit_h, with_meta=with_meta),
        grid=(m // bm, n_n),
        in_specs=in_specs,
        out_specs=out_specs,
        out_shape=out_shape,
        scratch_shapes=scratch,
        compiler_params=_params(("arbitrary", "arbitrary"), est),
        name=name,
    )(*args)
    outs = list(outs)
    x_new = outs.pop(0)
    h_new = outs.pop(0) if emit_h else None
    xm_new = outs.pop(0) if with_meta else None
    hm_new = outs.pop(0) if (with_meta and emit_h) else None
    return x_new, h_new, xm_new, hm_new


def _gelu_tanh(x):
    return 0.5 * x * (1.0 + jnp.tanh(0.7978845608028654 * (x + 0.044715 * (x * x * x))))


def _ffn_in_tile(h_ref, wa_scr, wu_scr, cw_ref, halo, o_ref, a_scr, *, rows):
    h = h_ref[...]
    a = jnp.dot(h, wa_scr[...], preferred_element_type=F32)
    u = jnp.dot(h, wu_scr[...], preferred_element_type=F32)
    a_scr[0:SUBLANES, :] = jnp.zeros((SUBLANES, a.shape[1]), F32) if halo is None else halo
    a_scr[SUBLANES:SUBLANES + rows, :] = a
    conv = cw_ref[2:3, :] * a
    conv = conv + cw_ref[1:2, :] * a_scr[SUBLANES - 1:SUBLANES - 1 + rows, :]
    conv = conv + cw_ref[0:1, :] * a_scr[SUBLANES - 2:SUBLANES - 2 + rows, :]
    o_ref[...] = (_gelu_tanh(conv) * u).astype(BF16)
    return a


def _ffn_in_kernel(h_ref, hm_ref, wa_ref, wu_ref, cw_ref, o_ref, om_ref,
                   wa_scr, wu_scr, a_scr, tail_scr, mtail_scr, *, bm, tiles_per_batch):
    @pl.when(pl.program_id(1) == 0)
    def _():
        wa_scr[...] = wa_ref[...].astype(BF16)
        wu_scr[...] = wu_ref[...].astype(BF16)
        a_m = _ffn_in_tile(hm_ref, wa_scr, wu_scr, cw_ref, None, om_ref, a_scr, rows=META_ROWS)
        mtail_scr[...] = a_m[N_META - SUBLANES:N_META, :]

    first = pl.program_id(1) % tiles_per_batch == 0
    halo = jnp.where(first, mtail_scr[...], tail_scr[...])
    a = _ffn_in_tile(h_ref, wa_scr, wu_scr, cw_ref, halo, o_ref, a_scr, rows=bm)
    tail_scr[...] = a[bm - SUBLANES:bm, :]


def _ffn_in(h, hm, w_ffn_in, conv_w, layer, *, bm, bn, tiles_per_batch):
    m = h.shape[0]
    n_j = D_FF // bn
    est = (2 * (bm * D_MODEL * 2 + 2 * D_MODEL * bn * 4 + bm * bn * 2) + 2 * D_MODEL * bn * 2
           + 9 * bm * bn * 4 + (4 << 20))
    return pl.pallas_call(
        functools.partial(_ffn_in_kernel, bm=bm, tiles_per_batch=tiles_per_batch),
        grid=(n_j, m // bm),
        in_specs=[pl.BlockSpec((bm, D_MODEL), lambda j, i: (i, 0)),
                  pl.BlockSpec((META_ROWS, D_MODEL), lambda j, i: (0, 0)),
                  pl.BlockSpec((None, D_MODEL, bn), lambda j, i: (layer, 0, j)),
                  pl.BlockSpec((None, D_MODEL, bn), lambda j, i: (layer, 0, j + n_j)),
                  pl.BlockSpec((None, CONV_WIDTH, bn), lambda j, i: (layer, 0, j))],
        out_specs=[pl.BlockSpec((bm, bn), lambda j, i: (i, j)),
                   pl.BlockSpec((META_ROWS, bn), lambda j, i: (0, j))],
        out_shape=[jax.ShapeDtypeStruct((m, D_FF), BF16), jax.ShapeDtypeStruct((META_ROWS, D_FF), BF16)],
        scratch_shapes=[pltpu.VMEM((D_MODEL, bn), BF16), pltpu.VMEM((D_MODEL, bn), BF16),
                        pltpu.VMEM((SUBLANES + bm, bn), F32),
                        pltpu.VMEM((SUBLANES, bn), F32), pltpu.VMEM((SUBLANES, bn), F32)],
        compiler_params=_params(("arbitrary", "arbitrary"), est),
        name="ffn_in",
    )(h, hm, w_ffn_in, w_ffn_in, conv_w)


def kernel(x, meta_tokens, norm_pre_mix, norm_post_mix, norm_pre_ffn, norm_post_ffn, w_in, b_if,
           mlstm_head_gain, conv_mix_w, pool_w, pool_scale, w_branch, w_out, w_ffn_in, ffn_conv_w,
           w_ffn_out):
    batch, seq, _ = x.shape
    depth = w_in.shape[0]
    m = batch * seq
    xr = x.reshape(m, D_MODEL)
    xm = jnp.pad(meta_tokens.astype(F32), ((0, META_ROWS - N_META), (0, 0)))

    wo = w_out.astype(BF16)
    wfo = w_ffn_out.astype(BF16)
    w_in, w_branch, w_ffn_in, pool_w = (t.astype(F32) for t in (w_in, w_branch, w_ffn_in, pool_w))
    cw = conv_mix_w.astype(F32)
    fcw = ffn_conv_w.astype(F32)

    def vec(v):
        return v.reshape(1, -1).astype(F32)

    hr = _norm(xr, vec(norm_pre_mix[0]), 512)
    hm = _norm(xm, vec(norm_pre_mix[0]), META_ROWS)
    for l in range(depth):
        bias_row = jnp.pad(b_if[l].reshape(1, N_LOGITS).astype(F32), ((0, 0), (0, LANES - N_LOGITS)))
        gain = vec(mlstm_head_gain[l])
        ps = vec(pool_scale[l])
        g_post_mix, g_pre_ffn, g_post_ffn = vec(norm_post_mix[l]), vec(norm_pre_ffn[l]), vec(norm_post_ffn[l])
        last_layer = l + 1 == depth
        g_next = None if last_layer else vec(norm_pre_mix[l + 1])

        big, big_m = _in_proj(hr, hm, w_in, l, bm=1024, bn=1024)
        logits, logits_m = _gate_logits(hr, hm, w_in, l, bm=1024)
        ya, ya_m = _mlstm(big, big_m, logits, logits_m, bias_row, gain, batches=batch, chunk=MLSTM_CHUNK)
        yb, yc, yb_m, yc_m = _conv_pool(big, big_m, cw, pool_w, ps, l, bm=512, tiles_per_batch=seq // 512)
        mg, mg_m = _branch((ya, yb, yc), (ya_m, yb_m, yc_m), w_branch, big, big_m, l, bm=1024, bn=512)
        xr, hf, xm, hf_m = _proj_residual(mg, mg_m, wo, l, xr, xm, g_post_mix, g_pre_ffn,
                                          bm=512, bn=D_MODEL, name="out_proj")
        act, act_m = _ffn_in(hf, hf_m, w_ffn_in, fcw, l, bm=1024, bn=512, tiles_per_batch=seq // 1024)
        xr, hr, xm, hm = _proj_residual(act, None if last_layer else act_m, wfo, l, xr, xm, g_post_ffn, g_next,
                                        bm=512, bn=512, name="ffn_out")
    return xr.reshape(batch, seq, D_MODEL)
```

```python
import functools
import math

import jax
import jax.numpy as jnp
from jax import lax
from jax.experimental import pallas as pl
from jax.experimental.pallas import tpu as pltpu

D_MODEL = 2048
N_META = 16
MIX_WIDTH = 1024
N_BRANCH = 3
HEADS = 8
HEAD_DIM = MIX_WIDTH // HEADS
POOL_WINDOWS = (2, 4, 8, 16)
POOL_GROUP = MIX_WIDTH // len(POOL_WINDOWS)
D_FF = 5632
RMS_EPS = 1e-6
CONV_WIDTH = 3

COL_Q, COL_K, COL_V, COL_O, COL_CB, COL_CC, COL_CX, COL_PU = (n * MIX_WIDTH for n in range(8))
COL_GATES = 8 * MIX_WIDTH
BIG_COLS = COL_GATES + N_BRANCH * D_MODEL
N_LOGITS = 2 * HEADS
LANES = 128
HALO = 16
SUBLANES = 8
META_ROWS = 128
MLSTM_CHUNK = 256
VMEM_LIMIT_CAP = 56 * 1024 * 1024

F32 = jnp.float32
BF16 = jnp.bfloat16


def _params(semantics, vmem_bytes):
    return pltpu.CompilerParams(dimension_semantics=semantics,
                                vmem_limit_bytes=min(int(vmem_bytes), VMEM_LIMIT_CAP))


def _rms(x, g):
    return x * lax.rsqrt(jnp.mean(x * x, axis=-1, keepdims=True) + RMS_EPS) * g


def _norm_kernel(x_ref, g_ref, h_ref):
    h_ref[...] = _rms(x_ref[...], g_ref[...]).astype(BF16)


def _norm(x, g, bm):
    m = x.shape[0]
    return pl.pallas_call(
        _norm_kernel,
        grid=(m // bm,),
        in_specs=[pl.BlockSpec((bm, D_MODEL), lambda i: (i, 0)),
                  pl.BlockSpec((1, D_MODEL), lambda i: (0, 0))],
        out_specs=pl.BlockSpec((bm, D_MODEL), lambda i: (i, 0)),
        out_shape=jax.ShapeDtypeStruct((m, D_MODEL), BF16),
        compiler_params=_params(("parallel",), 10 * bm * D_MODEL * 4),
        name="pre_norm",
    )(x, g)


_NT = (((1,), (1,)), ((), ()))


def _in_proj_kernel(h_ref, hm_ref, wt_ref, o_ref, om_ref, w_scr):
    @pl.when(pl.program_id(1) == 0)
    def _():
        w_scr[...] = wt_ref[0].astype(BF16)
        om_ref[...] = lax.dot_general(hm_ref[...], w_scr[...], _NT, preferred_element_type=F32).astype(BF16)

    o_ref[...] = lax.dot_general(h_ref[...], w_scr[...], _NT, preferred_element_type=F32).astype(BF16)


def _in_proj(h, hm, w_in_t, layer, *, bm, bn):
    m, k = h.shape

    def w_rows(j, i):
        skip = jnp.where(j >= COL_CB // bn, N_LOGITS // SUBLANES, 0)
        return layer, (j * (bn // SUBLANES) + skip) * SUBLANES, 0

    est = 2 * (bm * k * 2 + k * bn * 4 + bm * bn * 2) + k * bn * 2 + 2 * bm * bn * 4 + (4 << 20)
    return pl.pallas_call(
        _in_proj_kernel,
        grid=(BIG_COLS // bn, m // bm),
        in_specs=[pl.BlockSpec((bm, k), lambda j, i: (i, 0)),
                  pl.BlockSpec((META_ROWS, k), lambda j, i: (0, 0)),
                  pl.BlockSpec((pl.Element(1), pl.Element(bn), pl.Element(k)), w_rows)],
        out_specs=[pl.BlockSpec((bm, bn), lambda j, i: (i, j)),
                   pl.BlockSpec((META_ROWS, bn), lambda j, i: (0, j))],
        out_shape=[jax.ShapeDtypeStruct((m, BIG_COLS), BF16),
                   jax.ShapeDtypeStruct((META_ROWS, BIG_COLS), BF16)],
        scratch_shapes=[pltpu.VMEM((bn, k), BF16)],
        compiler_params=_params(("arbitrary", "arbitrary"), est),
        name="in_proj",
    )(h, hm, w_in_t)


def _gate_logits_kernel(h_ref, hm_ref, wt_ref, o_ref, om_ref, w_scr):
    @pl.when(pl.program_id(0) == 0)
    def _():
        w_scr[...] = wt_ref[...].astype(BF16)
        om_ref[...] = lax.dot_general(hm_ref[...], w_scr[...], _NT, preferred_element_type=F32)

    o_ref[...] = lax.dot_general(h_ref[...], w_scr[...], _NT, preferred_element_type=F32)


def _gate_logits(h, hm, w_in_t, layer, *, bm):
    m, k = h.shape
    return pl.pallas_call(
        _gate_logits_kernel,
        grid=(m // bm,),
        in_specs=[pl.BlockSpec((bm, k), lambda i: (i, 0)),
                  pl.BlockSpec((META_ROWS, k), lambda i: (0, 0)),
                  pl.BlockSpec((None, LANES, k), lambda i: (layer, COL_CB // LANES, 0))],
        out_specs=[pl.BlockSpec((bm, LANES), lambda i: (i, 0)),
                   pl.BlockSpec((META_ROWS, LANES), lambda i: (0, 0))],
        out_shape=[jax.ShapeDtypeStruct((m, LANES), F32), jax.ShapeDtypeStruct((META_ROWS, LANES), F32)],
        scratch_shapes=[pltpu.VMEM((LANES, k), BF16)],
        compiler_params=_params(("arbitrary",), 4 * bm * k * 2 + (8 << 20)),
        name="gate_logits",
    )(h, hm, w_in_t)


def _shift_rows(x, d):
    return pltpu.roll(x, d, axis=0)


def _cumsum_rows(x):
    rows = x.shape[0]
    row = lax.broadcasted_iota(jnp.int32, x.shape, 0)
    d = 1
    while d < rows:
        x = x + jnp.where(row >= d, _shift_rows(x, d), 0.0)
        d *= 2
    return x


def _cummax_rows(x):
    rows = x.shape[0]
    row = lax.broadcasted_iota(jnp.int32, x.shape, 0)
    d = 1
    while d < rows:
        x = jnp.maximum(x, jnp.where(row >= d, _shift_rows(x, d), -jnp.inf))
        d *= 2
    return x


def _log_sigmoid(x):
    return jnp.minimum(x, 0.0) - jnp.log1p(jnp.exp(-jnp.abs(x)))


def _mlstm_chunk(q_ref, k_ref, v_ref, o_ref, gl_ref, bias_ref, gain_ref, y_ref,
                 c_scr, m_scr, s_scr, r_scr, *, chunk, n_valid):
    scale = HEAD_DIM ** -0.5
    last = n_valid - 1
    gl = gl_ref[...] + bias_ref[...]
    logf = pltpu.roll(_log_sigmoid(gl), LANES - HEADS, axis=1)
    b = _cumsum_rows(logf)
    a = gl - b
    m_prev = m_scr[...]
    g = jnp.maximum(_cummax_rows(a), m_prev)
    inter_w = jnp.exp(m_prev - g) * scale
    exp_neg_m = jnp.exp(-(b + g))
    g_last = g[last:last + 1, :]
    decay = jnp.exp(m_prev - g_last)
    row1 = lax.broadcasted_iota(jnp.int32, a.shape, 0)
    ws = jnp.where(row1 <= last, jnp.exp(a - g_last), 0.0)
    m_scr[...] = b[last:last + 1, :] + g_last
    a_t = (a + math.log(scale)).T

    rows = lax.broadcasted_iota(jnp.int32, (chunk, chunk), 0)
    cols = lax.broadcasted_iota(jnp.int32, (chunk, chunk), 1)
    causal = rows >= cols
    ones = jnp.ones((chunk, HEAD_DIM), BF16)
    head = [slice(h * HEAD_DIM, (h + 1) * HEAD_DIM) for h in range(HEADS)]
    wide = [slice(h * 2 * HEAD_DIM, (h + 1) * 2 * HEAD_DIM) for h in range(HEADS)]

    for h in range(HEADS):
        w = jnp.where(causal, jnp.exp(a_t[h:h + 1, :] - g[:, h:h + 1]), 0.0)
        s = lax.dot_general(q_ref[:, head[h]], k_ref[:, head[h]], (((1,), (1,)), ((), ())),
                            preferred_element_type=F32)
        s_scr[h, 0:chunk, 0:chunk] = (s * w).astype(BF16)
    for h in range(HEADS):
        v_aug = jnp.concatenate([v_ref[:, head[h]], ones], axis=1)
        r_scr[0:chunk, wide[h]] = (
            jnp.dot(q_ref[:, head[h]], c_scr[h].astype(BF16), preferred_element_type=F32)
            * inter_w[:, h:h + 1]
            + jnp.dot(s_scr[h, 0:chunk, 0:chunk], v_aug, preferred_element_type=F32))
    for h in range(HEADS):
        v_aug = jnp.concatenate([v_ref[:, head[h]], ones], axis=1)
        kw = (k_ref[:, head[h]].astype(F32) * ws[:, h:h + 1]).astype(BF16)
        upd = lax.dot_general(kw, v_aug, (((0,), (0,)), ((), ())), preferred_element_type=F32)
        c_scr[h] = decay[:, h:h + 1] * c_scr[h] + upd
    for h in range(HEADS):
        r = r_scr[0:chunk, wide[h]]
        num, den = r[:, :HEAD_DIM], r[:, HEAD_DIM:]
        hh = num / jnp.maximum(jnp.abs(den), exp_neg_m[:, h:h + 1])
        hn = hh * lax.rsqrt(jnp.mean(hh * hh, axis=-1, keepdims=True) + RMS_EPS)
        y_ref[:, head[h]] = (hn * gain_ref[:, head[h]]
                             * jax.nn.sigmoid(o_ref[:, head[h]].astype(F32))).astype(BF16)


def _mlstm_kernel(q_ref, k_ref, v_ref, o_ref, gl_ref, qm_ref, km_ref, vm_ref, om_ref, glm_ref,
                  bias_ref, gain_ref, y_ref, ym_ref, c_scr, m_scr, c0_scr, m0_scr, s_scr, r_scr, *, chunk):
    first_chunk = pl.program_id(1) == 0

    @pl.when(jnp.logical_and(pl.program_id(0) == 0, first_chunk))
    def _():
        c_scr[...] = jnp.zeros_like(c_scr)
        m_scr[...] = jnp.zeros_like(m_scr)
        _mlstm_chunk(qm_ref, km_ref, vm_ref, om_ref, glm_ref, bias_ref, gain_ref, ym_ref,
                     c_scr, m_scr, s_scr, r_scr, chunk=META_ROWS, n_valid=N_META)
        c0_scr[...] = c_scr[...]
        m0_scr[...] = m_scr[...]

    @pl.when(first_chunk)
    def _():
        c_scr[...] = c0_scr[...]
        m_scr[...] = m0_scr[...]

    _mlstm_chunk(q_ref, k_ref, v_ref, o_ref, gl_ref, bias_ref, gain_ref, y_ref,
                 c_scr, m_scr, s_scr, r_scr, chunk=chunk, n_valid=chunk)


def _mlstm(big, big_m, logits, logits_m, bias_row, gain, *, batches, chunk):
    m = big.shape[0]
    n_chunks = m // (batches * chunk)
    cblk = [COL_Q // MIX_WIDTH, COL_K // MIX_WIDTH, COL_V // MIX_WIDTH, COL_O // MIX_WIDTH]
    main = [pl.BlockSpec((chunk, MIX_WIDTH), functools.partial(lambda b, c, cb: (b * n_chunks + c, cb), cb=cb))
            for cb in cblk]
    meta = [pl.BlockSpec((META_ROWS, MIX_WIDTH), functools.partial(lambda b, c, cb: (0, cb), cb=cb))
            for cb in cblk]
    in_specs = (main + [pl.BlockSpec((chunk, LANES), lambda b, c: (b * n_chunks + c, 0))]
                + meta + [pl.BlockSpec((META_ROWS, LANES), lambda b, c: (0, 0)),
                          pl.BlockSpec((1, LANES), lambda b, c: (0, 0)),
                          pl.BlockSpec((1, MIX_WIDTH), lambda b, c: (0, 0))])
    c_shape = (HEADS, HEAD_DIM, 2 * HEAD_DIM)
    est = 14 * chunk * MIX_WIDTH * 2 + 24 * chunk * chunk * 4 + 8 * HEADS * HEAD_DIM * 2 * HEAD_DIM * 4
    return pl.pallas_call(
        functools.partial(_mlstm_kernel, chunk=chunk),
        grid=(batches, n_chunks),
        in_specs=in_specs,
        out_specs=[pl.BlockSpec((chunk, MIX_WIDTH), lambda b, c: (b * n_chunks + c, 0)),
                   pl.BlockSpec((META_ROWS, MIX_WIDTH), lambda b, c: (0, 0))],
        out_shape=[jax.ShapeDtypeStruct((m, MIX_WIDTH), BF16),
                   jax.ShapeDtypeStruct((META_ROWS, MIX_WIDTH), BF16)],
        scratch_shapes=[pltpu.VMEM(c_shape, F32), pltpu.VMEM((1, LANES), F32),
                        pltpu.VMEM(c_shape, F32), pltpu.VMEM((1, LANES), F32),
                        pltpu.VMEM((HEADS, chunk, chunk), BF16),
                        pltpu.VMEM((chunk, HEADS * 2 * HEAD_DIM), F32)],
        compiler_params=_params(("arbitrary", "arbitrary"), est + (16 << 20)),
        name="mlstm",
    )(big, big, big, big, logits, big_m, big_m, big_m, big_m, logits_m, bias_row, gain)


def _conv_pool_tile(cb_ref, cc_ref, cx_ref, pu_ref, halo, cw_ref, pw_ref, ps_ref, yb_ref, yc_ref,
                    p_scr, u_scr, *, rows, from_start):
    if halo is None:
        p_scr[0:HALO, :] = jnp.zeros((HALO, MIX_WIDTH), F32)
        u_scr[0:HALO, :] = jnp.zeros((HALO, MIX_WIDTH), F32)
    else:
        p_scr[0:HALO, :] = halo[0]
        u_scr[0:HALO, :] = halo[1]
    p_scr[HALO:HALO + rows, :] = cc_ref[...].astype(F32) * cx_ref[...].astype(F32)
    u_scr[HALO:HALO + rows, :] = pu_ref[...].astype(F32)

    conv = cw_ref[2:3, :] * p_scr[HALO:HALO + rows, :]
    conv = conv + cw_ref[1:2, :] * p_scr[HALO - 1:HALO - 1 + rows, :]
    conv = conv + cw_ref[0:1, :] * p_scr[HALO - 2:HALO - 2 + rows, :]
    yb_ref[...] = (cb_ref[...].astype(F32) * conv).astype(BF16)

    for grp, win in enumerate(POOL_WINDOWS):
        sl = slice(grp * POOL_GROUP, (grp + 1) * POOL_GROUP)
        cur = u_scr[HALO:HALO + rows, sl]
        tot = cur
        for lag in range(1, win):
            tot = tot + u_scr[HALO - lag:HALO - lag + rows, sl]
        if from_start:
            t = lax.broadcasted_iota(jnp.int32, (rows, 1), 0)
            pooled = tot / jnp.minimum(t + 1, win).astype(F32) - cur
        else:
            pooled = tot * (1.0 / win) - cur
        y = jnp.dot(pooled.astype(BF16), pw_ref[grp], preferred_element_type=F32)
        yc_ref[:, sl] = (y * ps_ref[:, sl]).astype(BF16)


def _conv_pool_kernel(cb_ref, cc_ref, cx_ref, pu_ref, ccp_ref, cxp_ref, pup_ref,
                      cbm_ref, ccm_ref, cxm_ref, pum_ref, cw_ref, pw_ref, ps_ref,
                      yb_ref, yc_ref, ybm_ref, ycm_ref, pw_scr, p_scr, u_scr, *, bm, tiles_per_batch):
    @pl.when(pl.program_id(0) == 0)
    def _():
        pw_scr[...] = pw_ref[...].astype(BF16)
        _conv_pool_tile(cbm_ref, ccm_ref, cxm_ref, pum_ref, None, cw_ref, pw_scr, ps_ref,
                        ybm_ref, ycm_ref, p_scr, u_scr, rows=META_ROWS, from_start=True)

    first = pl.program_id(0) % tiles_per_batch == 0
    cch = jnp.where(first, ccm_ref[0:HALO, :], ccp_ref[...]).astype(F32)
    cxh = jnp.where(first, cxm_ref[0:HALO, :], cxp_ref[...]).astype(F32)
    puh = jnp.where(first, pum_ref[0:HALO, :], pup_ref[...]).astype(F32)
    _conv_pool_tile(cb_ref, cc_ref, cx_ref, pu_ref, (cch * cxh, puh), cw_ref, pw_scr, ps_ref,
                    yb_ref, yc_ref, p_scr, u_scr, rows=bm, from_start=False)


def _conv_pool(big, big_m, conv_w, pool_w, pool_scale, layer, *, bm, tiles_per_batch):
    m = big.shape[0]
    cblk = [COL_CB // MIX_WIDTH, COL_CC // MIX_WIDTH, COL_CX // MIX_WIDTH, COL_PU // MIX_WIDTH]
    per = bm // HALO
    in_specs = [pl.BlockSpec((bm, MIX_WIDTH), functools.partial(lambda i, c: (i, c), c=c)) for c in cblk]
    in_specs += [pl.BlockSpec((HALO, MIX_WIDTH),
                              functools.partial(lambda i, c: (jnp.maximum(i * per - 1, 0), c), c=c))
                 for c in cblk[1:]]
    in_specs += [pl.BlockSpec((META_ROWS, MIX_WIDTH), functools.partial(lambda i, c: (0, c), c=c))
                 for c in cblk]
    in_specs += [pl.BlockSpec((None, CONV_WIDTH, MIX_WIDTH), lambda i: (layer, 0, 0)),
                 pl.BlockSpec((None, len(POOL_WINDOWS), POOL_GROUP, POOL_GROUP), lambda i: (layer, 0, 0, 0)),
                 pl.BlockSpec((1, MIX_WIDTH), lambda i: (0, 0))]
    out_spec = pl.BlockSpec((bm, MIX_WIDTH), lambda i: (i, 0))
    meta_spec = pl.BlockSpec((META_ROWS, MIX_WIDTH), lambda i: (0, 0))
    return pl.pallas_call(
        functools.partial(_conv_pool_kernel, bm=bm, tiles_per_batch=tiles_per_batch),
        grid=(m // bm,),
        in_specs=in_specs,
        out_specs=[out_spec, out_spec, meta_spec, meta_spec],
        out_shape=[jax.ShapeDtypeStruct((m, MIX_WIDTH), BF16)] * 2
        + [jax.ShapeDtypeStruct((META_ROWS, MIX_WIDTH), BF16)] * 2,
        scratch_shapes=[pltpu.VMEM((len(POOL_WINDOWS), POOL_GROUP, POOL_GROUP), BF16),
                        pltpu.VMEM((HALO + bm, MIX_WIDTH), F32), pltpu.VMEM((HALO + bm, MIX_WIDTH), F32)],
        compiler_params=_params(("arbitrary",), 16 * bm * MIX_WIDTH * 4 + (10 << 20)),
        name="conv_pool",
    )(big, big, big, big, big, big, big, big_m, big_m, big_m, big_m, conv_w, pool_w, pool_scale)


def _branch_tile(ya_ref, yb_ref, yc_ref, w_scr, g0_ref, g1_ref, g2_ref, o_ref):
    acc = None
    for y_ref, g_ref, n in ((ya_ref, g0_ref, 0), (yb_ref, g1_ref, 1), (yc_ref, g2_ref, 2)):
        br = jnp.dot(y_ref[...], w_scr[n], preferred_element_type=F32)
        term = jax.nn.sigmoid(g_ref[...].astype(F32)) * br
        acc = term if acc is None else acc + term
    o_ref[...] = acc.astype(BF16)


def _branch_kernel(ya_ref, yb_ref, yc_ref, g0_ref, g1_ref, g2_ref,
                   yam_ref, ybm_ref, ycm_ref, g0m_ref, g1m_ref, g2m_ref, w_ref, o_ref, om_ref, w_scr):
    @pl.when(pl.program_id(1) == 0)
    def _():
        w_scr[...] = w_ref[...].astype(BF16)
        _branch_tile(yam_ref, ybm_ref, ycm_ref, w_scr, g0m_ref, g1m_ref, g2m_ref, om_ref)

    _branch_tile(ya_ref, yb_ref, yc_ref, w_scr, g0_ref, g1_ref, g2_ref, o_ref)


def _branch(ys, ys_m, w_branch, big, big_m, layer, *, bm, bn):
    m = big.shape[0]
    offs = [(COL_GATES + n * D_MODEL) // bn for n in range(N_BRANCH)]
    y_spec = pl.BlockSpec((bm, MIX_WIDTH), lambda j, i: (i, 0))
    ym_spec = pl.BlockSpec((META_ROWS, MIX_WIDTH), lambda j, i: (0, 0))
    g_specs = [pl.BlockSpec((bm, bn), functools.partial(lambda j, i, off: (i, off + j), off=off)) for off in offs]
    gm_specs = [pl.BlockSpec((META_ROWS, bn), functools.partial(lambda j, i, off: (0, off + j), off=off))
                for off in offs]
    est = (2 * (3 * bm * MIX_WIDTH * 2 + 3 * MIX_WIDTH * bn * 4 + 4 * bm * bn * 2) + 3 * MIX_WIDTH * bn * 2
           + 6 * bm * bn * 4 + (4 << 20))
    return pl.pallas_call(
        _branch_kernel,
        grid=(D_MODEL // bn, m // bm),
        in_specs=[y_spec] * 3 + g_specs + [ym_spec] * 3 + gm_specs
        + [pl.BlockSpec((None, N_BRANCH, MIX_WIDTH, bn), lambda j, i: (layer, 0, 0, j))],
        out_specs=[pl.BlockSpec((bm, bn), lambda j, i: (i, j)),
                   pl.BlockSpec((META_ROWS, bn), lambda j, i: (0, j))],
        out_shape=[jax.ShapeDtypeStruct((m, D_MODEL), BF16), jax.ShapeDtypeStruct((META_ROWS, D_MODEL), BF16)],
        scratch_shapes=[pltpu.VMEM((N_BRANCH, MIX_WIDTH, bn), BF16)],
        compiler_params=_params(("arbitrary", "arbitrary"), est),
        name="branch_merge",
    )(*ys, big, big, big, *ys_m, big_m, big_m, big_m, w_branch)


def _finish_rows(y_scr, x_ref, gpost_ref, gnext_ref, xo_ref, ho_ref, *, sub):
    n_n, rows, _ = y_scr.shape

    def rows_step(r, carry):
        rs = pl.ds(pl.multiple_of(r * sub, sub), sub)
        y = jnp.concatenate([y_scr[c, rs, :] for c in range(n_n)], axis=1)
        xn = x_ref[rs, :] + _rms(y, gpost_ref[...])
        xo_ref[rs, :] = xn
        if ho_ref is not None:
            ho_ref[rs, :] = _rms(xn, gnext_ref[...]).astype(BF16)
        return carry
    lax.fori_loop(0, rows // sub, rows_step, 0, unroll=min(4, rows // sub))


def _proj_residual_kernel(*refs, n_n, emit_h, with_meta):
    a_ref, w_ref, x_ref, gpost_ref = refs[:4]
    refs = refs[4:]
    gnext_ref = None
    if emit_h:
        gnext_ref = refs[0]
        refs = refs[1:]
    if with_meta:
        am_ref, xm_ref = refs[:2]
        refs = refs[2:]
    xo_ref = refs[0]
    refs = refs[1:]
    ho_ref = None
    if emit_h:
        ho_ref = refs[0]
        refs = refs[1:]
    if with_meta:
        xmo_ref = refs[0]
        refs = refs[1:]
        hmo_ref = None
        if emit_h:
            hmo_ref = refs[0]
            refs = refs[1:]
        y_scr, ym_scr = refs
    else:
        y_scr, = refs
    n = pl.program_id(1)
    sub = 64

    if with_meta:
        @pl.when(pl.program_id(0) == 0)
        def _():
            ym_scr[n] = jnp.dot(am_ref[...], w_ref[...], preferred_element_type=F32)

            @pl.when(n == n_n - 1)
            def _():
                _finish_rows(ym_scr, xm_ref, gpost_ref, gnext_ref, xmo_ref, hmo_ref, sub=sub)

    y_scr[n] = jnp.dot(a_ref[...], w_ref[...], preferred_element_type=F32)

    @pl.when(n == n_n - 1)
    def _():
        _finish_rows(y_scr, x_ref, gpost_ref, gnext_ref, xo_ref, ho_ref, sub=sub)


def _proj_residual(a, a_m, w, layer, x, x_m, g_post, g_next, *, bm, bn, name):
    m, kdim = a.shape
    n_n = D_MODEL // bn
    emit_h = g_next is not None
    with_meta = a_m is not None
    row = lambda i, n: (i, 0)
    const = lambda i, n: (0, 0)
    vec = pl.BlockSpec((1, D_MODEL), const)
    w_mode = dict(pipeline_mode=pl.Buffered(1)) if n_n == 1 else {}
    in_specs = [pl.BlockSpec((bm, kdim), row),
                pl.BlockSpec((None, kdim, bn), lambda i, n: (layer, 0, n), **w_mode),
                pl.BlockSpec((bm, D_MODEL), row), vec]
    args = [a, w, x, g_post]
    if emit_h:
        in_specs.append(vec)
        args.append(g_next)
    if with_meta:
        in_specs += [pl.BlockSpec((META_ROWS, kdim), const), pl.BlockSpec((META_ROWS, D_MODEL), const)]
        args += [a_m, x_m]
    out_specs = [pl.BlockSpec((bm, D_MODEL), row)]
    out_shape = [jax.ShapeDtypeStruct((m, D_MODEL), F32)]
    if emit_h:
        out_specs.append(pl.BlockSpec((bm, D_MODEL), row))
        out_shape.append(jax.ShapeDtypeStruct((m, D_MODEL), BF16))
    scratch = [pltpu.VMEM((n_n, bm, bn), F32)]
    if with_meta:
        out_specs.append(pl.BlockSpec((META_ROWS, D_MODEL), const))
        out_shape.append(jax.ShapeDtypeStruct((META_ROWS, D_MODEL), F32))
        if emit_h:
            out_specs.append(pl.BlockSpec((META_ROWS, D_MODEL), const))
            out_shape.append(jax.ShapeDtypeStruct((META_ROWS, D_MODEL), BF16))
        scratch.append(pltpu.VMEM((n_n, META_ROWS, bn), F32))
    w_bufs = 1 if n_n == 1 else 2
    est = (2 * (bm * kdim * 2 + bm * D_MODEL * (4 + 4 + 2)) + w_bufs * kdim * bn * 2 + bm * D_MODEL * 4
           + 2 * bm * bn * 4 + 2 * META_ROWS * (kdim * 2 + 3 * D_MODEL * 4) + (2 << 20))
    outs = pl.pallas_call(
        functools.partial(_proj_residual_kernel, n_n=n_n, emit_h=emit_h, with_meta=with_meta),
        grid=(m // bm, n_n),
        in_specs=in_specs,
        out_specs=out_specs,
        out_shape=out_shape,
        scratch_shapes=scratch,
        compiler_params=_params(("arbitrary", "arbitrary"), est),
        name=name,
    )(*args)
    outs = list(outs)
    x_new = outs.pop(0)
    h_new = outs.pop(0) if emit_h else None
    xm_new = outs.pop(0) if with_meta else None
    hm_new = outs.pop(0) if (with_meta and emit_h) else None
    return x_new, h_new, xm_new, hm_new


def _gelu_tanh(x):
    return 0.5 * x * (1.0 + jnp.tanh(0.7978845608028654 * (x + 0.044715 * (x * x * x))))


def _ffn_in_tile(h_ref, wa_scr, wu_scr, cw_ref, halo, o_ref, a_scr, *, rows):
    h = h_ref[...]
    a = jnp.dot(h, wa_scr[...], preferred_element_type=F32)
    u = jnp.dot(h, wu_scr[...], preferred_element_type=F32)
    a_scr[0:SUBLANES, :] = jnp.zeros((SUBLANES, a.shape[1]), F32) if halo is None else halo
    a_scr[SUBLANES:SUBLANES + rows, :] = a
    conv = cw_ref[2:3, :] * a
    conv = conv + cw_ref[1:2, :] * a_scr[SUBLANES - 1:SUBLANES - 1 + rows, :]
    conv = conv + cw_ref[0:1, :] * a_scr[SUBLANES - 2:SUBLANES - 2 + rows, :]
    o_ref[...] = (_gelu_tanh(conv) * u).astype(BF16)
    return a


def _ffn_in_kernel(h_ref, hm_ref, wa_ref, wu_ref, cw_ref, o_ref, om_ref,
                   wa_scr, wu_scr, a_scr, tail_scr, mtail_scr, *, bm, tiles_per_batch):
    @pl.when(pl.program_id(1) == 0)
    def _():
        wa_scr[...] = wa_ref[...].astype(BF16)
        wu_scr[...] = wu_ref[...].astype(BF16)
        a_m = _ffn_in_tile(hm_ref, wa_scr, wu_scr, cw_ref, None, om_ref, a_scr, rows=META_ROWS)
        mtail_scr[...] = a_m[N_META - SUBLANES:N_META, :]

    first = pl.program_id(1) % tiles_per_batch == 0
    halo = jnp.where(first, mtail_scr[...], tail_scr[...])
    a = _ffn_in_tile(h_ref, wa_scr, wu_scr, cw_ref, halo, o_ref, a_scr, rows=bm)
    tail_scr[...] = a[bm - SUBLANES:bm, :]


def _ffn_in(h, hm, w_ffn_in, conv_w, layer, *, bm, bn, tiles_per_batch):
    m = h.shape[0]
    n_j = D_FF // bn
    est = (2 * (bm * D_MODEL * 2 + 2 * D_MODEL * bn * 4 + bm * bn * 2) + 2 * D_MODEL * bn * 2
           + 9 * bm * bn * 4 + (4 << 20))
    return pl.pallas_call(
        functools.partial(_ffn_in_kernel, bm=bm, tiles_per_batch=tiles_per_batch),
        grid=(n_j, m // bm),
        in_specs=[pl.BlockSpec((bm, D_MODEL), lambda j, i: (i, 0)),
                  pl.BlockSpec((META_ROWS, D_MODEL), lambda j, i: (0, 0)),
                  pl.BlockSpec((None, D_MODEL, bn), lambda j, i: (layer, 0, j)),
                  pl.BlockSpec((None, D_MODEL, bn), lambda j, i: (layer, 0, j + n_j)),
                  pl.BlockSpec((None, CONV_WIDTH, bn), lambda j, i: (layer, 0, j))],
        out_specs=[pl.BlockSpec((bm, bn), lambda j, i: (i, j)),
                   pl.BlockSpec((META_ROWS, bn), lambda j, i: (0, j))],
        out_shape=[jax.ShapeDtypeStruct((m, D_FF), BF16), jax.ShapeDtypeStruct((META_ROWS, D_FF), BF16)],
        scratch_shapes=[pltpu.VMEM((D_MODEL, bn), BF16), pltpu.VMEM((D_MODEL, bn), BF16),
                        pltpu.VMEM((SUBLANES + bm, bn), F32),
                        pltpu.VMEM((SUBLANES, bn), F32), pltpu.VMEM((SUBLANES, bn), F32)],
        compiler_params=_params(("arbitrary", "arbitrary"), est),
        name="ffn_in",
    )(h, hm, w_ffn_in, w_ffn_in, conv_w)


def kernel(x, meta_tokens, norm_pre_mix, norm_post_mix, norm_pre_ffn, norm_post_ffn, w_in, b_if,
           mlstm_head_gain, conv_mix_w, pool_w, pool_scale, w_branch, w_out, w_ffn_in, ffn_conv_w,
           w_ffn_out):
    batch, seq, _ = x.shape
    depth = w_in.shape[0]
    m = batch * seq
    xr = x.reshape(m, D_MODEL)
    xm = jnp.pad(meta_tokens.astype(F32), ((0, META_ROWS - N_META), (0, 0)))

    wo = w_out.astype(BF16)
    wfo = w_ffn_out.astype(BF16)
    w_branch, w_ffn_in, pool_w = (t.astype(F32) for t in (w_branch, w_ffn_in, pool_w))
    w_in_t = jnp.swapaxes(w_in.astype(F32), 1, 2)
    cw = conv_mix_w.astype(F32)
    fcw = ffn_conv_w.astype(F32)

    def vec(v):
        return v.reshape(1, -1).astype(F32)

    hr = _norm(xr, vec(norm_pre_mix[0]), 512)
    hm = _norm(xm, vec(norm_pre_mix[0]), META_ROWS)
    for l in range(depth):
        bias_row = jnp.pad(b_if[l].reshape(1, N_LOGITS).astype(F32), ((0, 0), (0, LANES - N_LOGITS)))
        gain = vec(mlstm_head_gain[l])
        ps = vec(pool_scale[l])
        g_post_mix, g_pre_ffn, g_post_ffn = vec(norm_post_mix[l]), vec(norm_pre_ffn[l]), vec(norm_post_ffn[l])
        last_layer = l + 1 == depth
        g_next = None if last_layer else vec(norm_pre_mix[l + 1])

        big, big_m = _in_proj(hr, hm, w_in_t, l, bm=1024, bn=1024)
        logits, logits_m = _gate_logits(hr, hm, w_in_t, l, bm=1024)
        ya, ya_m = _mlstm(big, big_m, logits, logits_m, bias_row, gain, batches=batch, chunk=MLSTM_CHUNK)
        yb, yc, yb_m, yc_m = _conv_pool(big, big_m, cw, pool_w, ps, l, bm=512, tiles_per_batch=seq // 512)
        mg, mg_m = _branch((ya, yb, yc), (ya_m, yb_m, yc_m), w_branch, big, big_m, l, bm=1024, bn=512)
        xr, hf, xm, hf_m = _proj_residual(mg, mg_m, wo, l, xr, xm, g_post_mix, g_pre_ffn,
                                          bm=512, bn=D_MODEL, name="out_proj")
        act, act_m = _ffn_in(hf, hf_m, w_ffn_in, fcw, l, bm=1024, bn=512, tiles_per_batch=seq // 1024)
        xr, hr, xm, hm = _proj_residual(act, None if last_layer else act_m, wfo, l, xr, xm, g_post_ffn, g_next,
                                        bm=512, bn=512, name="ffn_out")
    return xr.reshape(batch, seq, D_MODEL)
```

```python
import functools
import math

import jax
import jax.numpy as jnp
import numpy as np
from jax import lax
from jax.experimental import pallas as pl
from jax.experimental.pallas import tpu as pltpu

D_MODEL = 2048
N_META = 16
MIX_WIDTH = 1024
N_BRANCH = 3
HEADS = 8
HEAD_DIM = MIX_WIDTH // HEADS
POOL_WINDOWS = (2, 4, 8, 16)
POOL_GROUP = MIX_WIDTH // len(POOL_WINDOWS)
D_FF = 5632
RMS_EPS = 1e-6
CONV_WIDTH = 3

COL_Q, COL_K, COL_V, COL_O, COL_CB, COL_CC, COL_CX, COL_PU = (n * MIX_WIDTH for n in range(8))
COL_GATES = 8 * MIX_WIDTH
BIG_COLS = COL_GATES + N_BRANCH * D_MODEL
N_LOGITS = 2 * HEADS
LANES = 128
HALO = 16
SUBLANES = 8
META_ROWS = 128
MLSTM_CHUNK = 256
VMEM_LIMIT_CAP = 56 * 1024 * 1024

F32 = jnp.float32
BF16 = jnp.bfloat16


def _params(semantics, vmem_bytes):
    return pltpu.CompilerParams(dimension_semantics=semantics,
                                vmem_limit_bytes=min(int(vmem_bytes), VMEM_LIMIT_CAP))


def _rms(x, g):
    return x * lax.rsqrt(jnp.mean(x * x, axis=-1, keepdims=True) + RMS_EPS) * g


def _norm_kernel(x_ref, g_ref, h_ref):
    h_ref[...] = _rms(x_ref[...], g_ref[...]).astype(BF16)


def _norm(x, g, bm):
    m = x.shape[0]
    return pl.pallas_call(
        _norm_kernel,
        grid=(m // bm,),
        in_specs=[pl.BlockSpec((bm, D_MODEL), lambda i: (i, 0)),
                  pl.BlockSpec((1, D_MODEL), lambda i: (0, 0))],
        out_specs=pl.BlockSpec((bm, D_MODEL), lambda i: (i, 0)),
        out_shape=jax.ShapeDtypeStruct((m, D_MODEL), BF16),
        compiler_params=_params(("parallel",), 10 * bm * D_MODEL * 4),
        name="pre_norm",
    )(x, g)


_NT = (((1,), (1,)), ((), ()))


def _in_proj_kernel(h_ref, hm_ref, wt_ref, wl_ref, o_ref, om_ref, lg_ref, lgm_ref, w_scr, wl_scr):
    first_tile = pl.program_id(1) == 0

    @pl.when(first_tile)
    def _():
        w_scr[...] = wt_ref[0].astype(BF16)
        om_ref[...] = lax.dot_general(hm_ref[...], w_scr[...], _NT, preferred_element_type=F32).astype(BF16)

    o_ref[...] = lax.dot_general(h_ref[...], w_scr[...], _NT, preferred_element_type=F32).astype(BF16)

    @pl.when(pl.program_id(0) == 0)
    def _():
        @pl.when(first_tile)
        def _():
            wl_scr[...] = wl_ref[0].astype(BF16)
            lgm_ref[...] = lax.dot_general(hm_ref[...], wl_scr[...], _NT, preferred_element_type=F32)

        lg_ref[...] = lax.dot_general(h_ref[...], wl_scr[...], _NT, preferred_element_type=F32)


def _in_proj(h, hm, w_in_t, layer, *, bm, bn):
    m, k = h.shape
    n_i = m // bm

    def w_rows(j, i):
        skip = jnp.where(j >= COL_CB // bn, N_LOGITS // SUBLANES, 0)
        return layer, (j * (bn // SUBLANES) + skip) * SUBLANES, 0

    est = (2 * (bm * k * 2 + k * bn * 4 + k * LANES * 4 + bm * bn * 2 + bm * LANES * 4) + k * bn * 2
           + 2 * bm * bn * 4 + (4 << 20))
    return pl.pallas_call(
        _in_proj_kernel,
        grid=(BIG_COLS // bn, n_i),
        in_specs=[pl.BlockSpec((bm, k), lambda j, i: (i, 0)),
                  pl.BlockSpec((META_ROWS, k), lambda j, i: (0, 0)),
                  pl.BlockSpec((pl.Element(1), pl.Element(bn), pl.Element(k)), w_rows),
                  pl.BlockSpec((pl.Element(1), pl.Element(LANES), pl.Element(k)), lambda j, i: (layer, COL_CB, 0))],
        out_specs=[pl.BlockSpec((bm, bn), lambda j, i: (i, j)),
                   pl.BlockSpec((META_ROWS, bn), lambda j, i: (0, j)),
                   pl.BlockSpec((bm, LANES), lambda j, i: (jnp.where(j == 0, i, n_i - 1), 0)),
                   pl.BlockSpec((META_ROWS, LANES), lambda j, i: (0, 0))],
        out_shape=[jax.ShapeDtypeStruct((m, BIG_COLS), BF16),
                   jax.ShapeDtypeStruct((META_ROWS, BIG_COLS), BF16),
                   jax.ShapeDtypeStruct((m, LANES), F32),
                   jax.ShapeDtypeStruct((META_ROWS, LANES), F32)],
        scratch_shapes=[pltpu.VMEM((bn, k), BF16), pltpu.VMEM((LANES, k), BF16)],
        compiler_params=_params(("arbitrary", "arbitrary"), est),
        name="in_proj",
    )(h, hm, w_in_t, w_in_t)


def _shift_rows(x, d):
    return pltpu.roll(x, d, axis=0)


def _cumsum_rows(x):
    rows = x.shape[0]
    row = lax.broadcasted_iota(jnp.int32, x.shape, 0)
    d = 1
    while d < rows:
        x = x + jnp.where(row >= d, _shift_rows(x, d), 0.0)
        d *= 2
    return x


def _cummax_rows(x):
    rows = x.shape[0]
    row = lax.broadcasted_iota(jnp.int32, x.shape, 0)
    d = 1
    while d < rows:
        x = jnp.maximum(x, jnp.where(row >= d, _shift_rows(x, d), -jnp.inf))
        d *= 2
    return x


def _log_sigmoid(x):
    return jnp.minimum(x, 0.0) - jnp.log1p(jnp.exp(-jnp.abs(x)))


def _mlstm_chunk(q_ref, k_ref, v_ref, o_ref, gl_ref, bias_ref, gain_ref, y_ref,
                 c_scr, m_scr, s_scr, r_scr, *, chunk, n_valid):
    scale = HEAD_DIM ** -0.5
    last = n_valid - 1
    gl = gl_ref[...] + bias_ref[...]
    logf = pltpu.roll(_log_sigmoid(gl), LANES - HEADS, axis=1)
    b = _cumsum_rows(logf)
    a = gl - b
    m_prev = m_scr[...]
    g = jnp.maximum(_cummax_rows(a), m_prev)
    inter_w = jnp.exp(m_prev - g) * scale
    exp_neg_m = jnp.exp(-(b + g))
    g_last = g[last:last + 1, :]
    decay = jnp.exp(m_prev - g_last)
    row1 = lax.broadcasted_iota(jnp.int32, a.shape, 0)
    ws = jnp.where(row1 <= last, jnp.exp(a - g_last), 0.0)
    m_scr[...] = b[last:last + 1, :] + g_last
    a_t = (a + math.log(scale)).T

    rows = lax.broadcasted_iota(jnp.int32, (chunk, chunk), 0)
    cols = lax.broadcasted_iota(jnp.int32, (chunk, chunk), 1)
    causal = rows >= cols
    ones = jnp.ones((chunk, HEAD_DIM), BF16)
    head = [slice(h * HEAD_DIM, (h + 1) * HEAD_DIM) for h in range(HEADS)]
    wide = [slice(h * 2 * HEAD_DIM, (h + 1) * 2 * HEAD_DIM) for h in range(HEADS)]

    for h in range(HEADS):
        w = jnp.where(causal, jnp.exp(a_t[h:h + 1, :] - g[:, h:h + 1]), 0.0)
        s = lax.dot_general(q_ref[:, head[h]], k_ref[:, head[h]], (((1,), (1,)), ((), ())),
                            preferred_element_type=F32)
        s_scr[h, 0:chunk, 0:chunk] = (s * w).astype(BF16)
    for h in range(HEADS):
        v_aug = jnp.concatenate([v_ref[:, head[h]], ones], axis=1)
        r_scr[0:chunk, wide[h]] = (
            jnp.dot(q_ref[:, head[h]], c_scr[h].astype(BF16), preferred_element_type=F32)
            * inter_w[:, h:h + 1]
            + jnp.dot(s_scr[h, 0:chunk, 0:chunk], v_aug, preferred_element_type=F32))
    for h in range(HEADS):
        v_aug = jnp.concatenate([v_ref[:, head[h]], ones], axis=1)
        kw = (k_ref[:, head[h]].astype(F32) * ws[:, h:h + 1]).astype(BF16)
        upd = lax.dot_general(kw, v_aug, (((0,), (0,)), ((), ())), preferred_element_type=F32)
        c_scr[h] = decay[:, h:h + 1] * c_scr[h] + upd
    for h in range(HEADS):
        r = r_scr[0:chunk, wide[h]]
        num, den = r[:, :HEAD_DIM], r[:, HEAD_DIM:]
        hh = num / jnp.maximum(jnp.abs(den), exp_neg_m[:, h:h + 1])
        hn = hh * lax.rsqrt(jnp.mean(hh * hh, axis=-1, keepdims=True) + RMS_EPS)
        y_ref[:, head[h]] = (hn * gain_ref[:, head[h]]
                             * jax.nn.sigmoid(o_ref[:, head[h]].astype(F32))).astype(BF16)


def _mlstm_kernel(q_ref, k_ref, v_ref, o_ref, gl_ref, qm_ref, km_ref, vm_ref, om_ref, glm_ref,
                  bias_ref, gain_ref, y_ref, ym_ref, c_scr, m_scr, c0_scr, m0_scr, s_scr, r_scr, *, chunk):
    first_chunk = pl.program_id(1) == 0

    @pl.when(jnp.logical_and(pl.program_id(0) == 0, first_chunk))
    def _():
        c_scr[...] = jnp.zeros_like(c_scr)
        m_scr[...] = jnp.zeros_like(m_scr)
        _mlstm_chunk(qm_ref, km_ref, vm_ref, om_ref, glm_ref, bias_ref, gain_ref, ym_ref,
                     c_scr, m_scr, s_scr, r_scr, chunk=META_ROWS, n_valid=N_META)
        c0_scr[...] = c_scr[...]
        m0_scr[...] = m_scr[...]

    @pl.when(first_chunk)
    def _():
        c_scr[...] = c0_scr[...]
        m_scr[...] = m0_scr[...]

    _mlstm_chunk(q_ref, k_ref, v_ref, o_ref, gl_ref, bias_ref, gain_ref, y_ref,
                 c_scr, m_scr, s_scr, r_scr, chunk=chunk, n_valid=chunk)


def _mlstm(big, big_m, logits, logits_m, bias_row, gain, *, batches, chunk):
    m = big.shape[0]
    n_chunks = m // (batches * chunk)
    cblk = [COL_Q // MIX_WIDTH, COL_K // MIX_WIDTH, COL_V // MIX_WIDTH, COL_O // MIX_WIDTH]
    main = [pl.BlockSpec((chunk, MIX_WIDTH), functools.partial(lambda b, c, cb: (b * n_chunks + c, cb), cb=cb))
            for cb in cblk]
    meta = [pl.BlockSpec((META_ROWS, MIX_WIDTH), functools.partial(lambda b, c, cb: (0, cb), cb=cb))
            for cb in cblk]
    in_specs = (main + [pl.BlockSpec((chunk, LANES), lambda b, c: (b * n_chunks + c, 0))]
                + meta + [pl.BlockSpec((META_ROWS, LANES), lambda b, c: (0, 0)),
                          pl.BlockSpec((1, LANES), lambda b, c: (0, 0)),
                          pl.BlockSpec((1, MIX_WIDTH), lambda b, c: (0, 0))])
    c_shape = (HEADS, HEAD_DIM, 2 * HEAD_DIM)
    est = 14 * chunk * MIX_WIDTH * 2 + 24 * chunk * chunk * 4 + 8 * HEADS * HEAD_DIM * 2 * HEAD_DIM * 4
    return pl.pallas_call(
        functools.partial(_mlstm_kernel, chunk=chunk),
        grid=(batches, n_chunks),
        in_specs=in_specs,
        out_specs=[pl.BlockSpec((chunk, MIX_WIDTH), lambda b, c: (b * n_chunks + c, 0)),
                   pl.BlockSpec((META_ROWS, MIX_WIDTH), lambda b, c: (0, 0))],
        out_shape=[jax.ShapeDtypeStruct((m, MIX_WIDTH), BF16),
                   jax.ShapeDtypeStruct((META_ROWS, MIX_WIDTH), BF16)],
        scratch_shapes=[pltpu.VMEM(c_shape, F32), pltpu.VMEM((1, LANES), F32),
                        pltpu.VMEM(c_shape, F32), pltpu.VMEM((1, LANES), F32),
                        pltpu.VMEM((HEADS, chunk, chunk), BF16),
                        pltpu.VMEM((chunk, HEADS * 2 * HEAD_DIM), F32)],
        compiler_params=_params(("arbitrary", "arbitrary"), est + (16 << 20)),
        name="mlstm",
    )(big, big, big, big, logits, big_m, big_m, big_m, big_m, logits_m, bias_row, gain)


POOL_BLOCK = 128
CONV_ROWS, CONV_COLS = 128, 256


def _pool_bands():
    bands = np.zeros((len(POOL_WINDOWS), POOL_BLOCK, 2 * POOL_BLOCK), np.float32)
    for grp, win in enumerate(POOL_WINDOWS):
        for r in range(POOL_BLOCK):
            for lag in range(win):
                bands[grp, r, POOL_BLOCK + r - lag] += 1.0 / win
            bands[grp, r, POOL_BLOCK + r] -= 1.0
    return jnp.asarray(bands, BF16)


def _conv3(cb_ref, cc_ref, cx_ref, halo, cw_ref, yb_ref, p_scr, *, rows):
    p_scr[0:HALO, :] = halo
    pieces = [(r, c) for r in range(0, rows, CONV_ROWS) for c in range(0, MIX_WIDTH, CONV_COLS)]
    for r, c in pieces:
        rs, cs = slice(r, r + CONV_ROWS), slice(c, c + CONV_COLS)
        p_scr[HALO + r:HALO + r + CONV_ROWS, cs] = cc_ref[rs, cs].astype(F32) * cx_ref[rs, cs].astype(F32)
    for r, c in pieces:
        rs, cs = slice(r, r + CONV_ROWS), slice(c, c + CONV_COLS)
        conv = cw_ref[2:3, cs] * p_scr[HALO + r:HALO + r + CONV_ROWS, cs]
        conv = conv + cw_ref[1:2, cs] * p_scr[HALO + r - 1:HALO + r - 1 + CONV_ROWS, cs]
        conv = conv + cw_ref[0:1, cs] * p_scr[HALO + r - 2:HALO + r - 2 + CONV_ROWS, cs]
        yb_ref[rs, cs] = (cb_ref[rs, cs].astype(F32) * conv).astype(BF16)


def _pool_from_start(pu_ref, pw_scr, ps_ref, yc_ref, u_scr, *, rows):
    u_scr[0:HALO, :] = jnp.zeros((HALO, MIX_WIDTH), F32)
    u_scr[HALO:HALO + rows, :] = pu_ref[...].astype(F32)
    t = lax.broadcasted_iota(jnp.int32, (rows, 1), 0)
    for grp, win in enumerate(POOL_WINDOWS):
        sl = slice(grp * POOL_GROUP, (grp + 1) * POOL_GROUP)
        cur = u_scr[HALO:HALO + rows, sl]
        tot = cur
        for lag in range(1, win):
            tot = tot + u_scr[HALO - lag:HALO - lag + rows, sl]
        pooled = tot / jnp.minimum(t + 1, win).astype(F32) - cur
        y = jnp.dot(pooled.astype(BF16), pw_scr[grp], preferred_element_type=F32)
        yc_ref[:, sl] = (y * ps_ref[:, sl]).astype(BF16)


def _pool_banded(band_ref, pw_scr, ps_ref, yc_ref, ub_scr, *, rows):
    for grp in range(len(POOL_WINDOWS)):
        sl = slice(grp * POOL_GROUP, (grp + 1) * POOL_GROUP)
        pooled = jnp.concatenate(
            [jnp.dot(band_ref[grp], ub_scr[r:r + 2 * POOL_BLOCK, sl], preferred_element_type=F32)
             for r in range(0, rows, POOL_BLOCK)], axis=0)
        y = jnp.dot(pooled.astype(BF16), pw_scr[grp], preferred_element_type=F32)
        yc_ref[:, sl] = (y * ps_ref[:, sl]).astype(BF16)


def _conv_pool_kernel(cb_ref, cc_ref, cx_ref, pu_ref, ccp_ref, cxp_ref, pup_ref,
                      cbm_ref, ccm_ref, cxm_ref, pum_ref, cw_ref, pw_ref, ps_ref, band_ref,
                      yb_ref, yc_ref, ybm_ref, ycm_ref, pw_scr, p_scr, u_scr, ub_scr, *, bm, tiles_per_batch):
    @pl.when(pl.program_id(0) == 0)
    def _():
        pw_scr[...] = pw_ref[...].astype(BF16)
        _conv3(cbm_ref, ccm_ref, cxm_ref, jnp.zeros((HALO, MIX_WIDTH), F32), cw_ref, ybm_ref, p_scr,
               rows=META_ROWS)
        _pool_from_start(pum_ref, pw_scr, ps_ref, ycm_ref, u_scr, rows=META_ROWS)

    first = pl.program_id(0) % tiles_per_batch == 0
    cch = jnp.where(first, ccm_ref[0:HALO, :], ccp_ref[...]).astype(F32)
    cxh = jnp.where(first, cxm_ref[0:HALO, :], cxp_ref[...]).astype(F32)
    _conv3(cb_ref, cc_ref, cx_ref, cch * cxh, cw_ref, yb_ref, p_scr, rows=bm)
    meta_tail = jnp.concatenate([jnp.zeros((POOL_BLOCK - N_META, MIX_WIDTH), BF16), pum_ref[0:N_META, :]], axis=0)
    ub_scr[0:POOL_BLOCK, :] = jnp.where(first, meta_tail, pup_ref[...])
    ub_scr[POOL_BLOCK:POOL_BLOCK + bm, :] = pu_ref[...]
    _pool_banded(band_ref, pw_scr, ps_ref, yc_ref, ub_scr, rows=bm)


def _conv_pool(big, big_m, conv_w, pool_w, pool_scale, layer, *, bm, tiles_per_batch):
    m = big.shape[0]
    cblk = [COL_CB // MIX_WIDTH, COL_CC // MIX_WIDTH, COL_CX // MIX_WIDTH, COL_PU // MIX_WIDTH]

    def prev_rows(rows, c):
        per = bm // rows
        return pl.BlockSpec((rows, MIX_WIDTH), lambda i: (jnp.maximum(i * per - 1, 0), c))

    in_specs = [pl.BlockSpec((bm, MIX_WIDTH), functools.partial(lambda i, c: (i, c), c=c)) for c in cblk]
    in_specs += [prev_rows(HALO, cblk[1]), prev_rows(HALO, cblk[2]), prev_rows(POOL_BLOCK, cblk[3])]
    in_specs += [pl.BlockSpec((META_ROWS, MIX_WIDTH), functools.partial(lambda i, c: (0, c), c=c))
                 for c in cblk]
    in_specs += [pl.BlockSpec((None, CONV_WIDTH, MIX_WIDTH), lambda i: (layer, 0, 0)),
                 pl.BlockSpec((None, len(POOL_WINDOWS), POOL_GROUP, POOL_GROUP), lambda i: (layer, 0, 0, 0)),
                 pl.BlockSpec((1, MIX_WIDTH), lambda i: (0, 0)),
                 pl.BlockSpec((len(POOL_WINDOWS), POOL_BLOCK, 2 * POOL_BLOCK), lambda i: (0, 0, 0))]
    out_spec = pl.BlockSpec((bm, MIX_WIDTH), lambda i: (i, 0))
    meta_spec = pl.BlockSpec((META_ROWS, MIX_WIDTH), lambda i: (0, 0))
    return pl.pallas_call(
        functools.partial(_conv_pool_kernel, bm=bm, tiles_per_batch=tiles_per_batch),
        grid=(m // bm,),
        in_specs=in_specs,
        out_specs=[out_spec, out_spec, meta_spec, meta_spec],
        out_shape=[jax.ShapeDtypeStruct((m, MIX_WIDTH), BF16)] * 2
        + [jax.ShapeDtypeStruct((META_ROWS, MIX_WIDTH), BF16)] * 2,
        scratch_shapes=[pltpu.VMEM((len(POOL_WINDOWS), POOL_GROUP, POOL_GROUP), BF16),
                        pltpu.VMEM((HALO + max(bm, META_ROWS), MIX_WIDTH), F32),
                        pltpu.VMEM((HALO + META_ROWS, MIX_WIDTH), F32),
                        pltpu.VMEM((POOL_BLOCK + bm, MIX_WIDTH), BF16)],
        compiler_params=_params(("arbitrary",), 14 * bm * MIX_WIDTH * 4 + (10 << 20)),
        name="conv_pool",
    )(big, big, big, big, big, big, big, big_m, big_m, big_m, big_m, conv_w, pool_w, pool_scale, _pool_bands())


def _branch_tile(ya_ref, yb_ref, yc_ref, w_scr, g0_ref, g1_ref, g2_ref, o_ref):
    acc = None
    for y_ref, g_ref, n in ((ya_ref, g0_ref, 0), (yb_ref, g1_ref, 1), (yc_ref, g2_ref, 2)):
        br = jnp.dot(y_ref[...], w_scr[n], preferred_element_type=F32)
        term = jax.nn.sigmoid(g_ref[...].astype(F32)) * br
        acc = term if acc is None else acc + term
    o_ref[...] = acc.astype(BF16)


def _branch_kernel(ya_ref, yb_ref, yc_ref, g0_ref, g1_ref, g2_ref,
                   yam_ref, ybm_ref, ycm_ref, g0m_ref, g1m_ref, g2m_ref, w_ref, o_ref, om_ref, w_scr):
    @pl.when(pl.program_id(1) == 0)
    def _():
        w_scr[...] = w_ref[...].astype(BF16)
        _branch_tile(yam_ref, ybm_ref, ycm_ref, w_scr, g0m_ref, g1m_ref, g2m_ref, om_ref)

    _branch_tile(ya_ref, yb_ref, yc_ref, w_scr, g0_ref, g1_ref, g2_ref, o_ref)


def _branch(ys, ys_m, w_branch, big, big_m, layer, *, bm, bn):
    m = big.shape[0]
    offs = [(COL_GATES + n * D_MODEL) // bn for n in range(N_BRANCH)]
    y_spec = pl.BlockSpec((bm, MIX_WIDTH), lambda j, i: (i, 0))
    ym_spec = pl.BlockSpec((META_ROWS, MIX_WIDTH), lambda j, i: (0, 0))
    g_specs = [pl.BlockSpec((bm, bn), functools.partial(lambda j, i, off: (i, off + j), off=off)) for off in offs]
    gm_specs = [pl.BlockSpec((META_ROWS, bn), functools.partial(lambda j, i, off: (0, off + j), off=off))
                for off in offs]
    est = (2 * (3 * bm * MIX_WIDTH * 2 + 3 * MIX_WIDTH * bn * 4 + 4 * bm * bn * 2) + 3 * MIX_WIDTH * bn * 2
           + 6 * bm * bn * 4 + (4 << 20))
    return pl.pallas_call(
        _branch_kernel,
        grid=(D_MODEL // bn, m // bm),
        in_specs=[y_spec] * 3 + g_specs + [ym_spec] * 3 + gm_specs
        + [pl.BlockSpec((None, N_BRANCH, MIX_WIDTH, bn), lambda j, i: (layer, 0, 0, j))],
        out_specs=[pl.BlockSpec((bm, bn), lambda j, i: (i, j)),
                   pl.BlockSpec((META_ROWS, bn), lambda j, i: (0, j))],
        out_shape=[jax.ShapeDtypeStruct((m, D_MODEL), BF16), jax.ShapeDtypeStruct((META_ROWS, D_MODEL), BF16)],
        scratch_shapes=[pltpu.VMEM((N_BRANCH, MIX_WIDTH, bn), BF16)],
        compiler_params=_params(("arbitrary", "arbitrary"), est),
        name="branch_merge",
    )(*ys, big, big, big, *ys_m, big_m, big_m, big_m, w_branch)


def _finish_rows(y_scr, x_ref, gpost_ref, gnext_ref, xo_ref, ho_ref, *, sub):
    n_n, rows, _ = y_scr.shape

    def rows_step(r, carry):
        rs = pl.ds(pl.multiple_of(r * sub, sub), sub)
        y = jnp.concatenate([y_scr[c, rs, :] for c in range(n_n)], axis=1)
        xn = x_ref[rs, :] + _rms(y, gpost_ref[...])
        xo_ref[rs, :] = xn
        if ho_ref is not None:
            ho_ref[rs, :] = _rms(xn, gnext_ref[...]).astype(BF16)
        return carry
    lax.fori_loop(0, rows // sub, rows_step, 0, unroll=min(4, rows // sub))


def _proj_residual_kernel(*refs, n_n, emit_h, with_meta):
    a_ref, w_ref, x_ref, gpost_ref = refs[:4]
    refs = refs[4:]
    gnext_ref = None
    if emit_h:
        gnext_ref = refs[0]
        refs = refs[1:]
    if with_meta:
        am_ref, xm_ref = refs[:2]
        refs = refs[2:]
    xo_ref = refs[0]
    refs = refs[1:]
    ho_ref = None
    if emit_h:
        ho_ref = refs[0]
        refs = refs[1:]
    if with_meta:
        xmo_ref = refs[0]
        refs = refs[1:]
        hmo_ref = None
        if emit_h:
            hmo_ref = refs[0]
            refs = refs[1:]
        y_scr, ym_scr = refs
    else:
        y_scr, = refs
    n = pl.program_id(1)
    sub = 64

    if with_meta:
        @pl.when(pl.program_id(0) == 0)
        def _():
            ym_scr[n] = jnp.dot(am_ref[...], w_ref[...], preferred_element_type=F32)

            @pl.when(n == n_n - 1)
            def _():
                _finish_rows(ym_scr, xm_ref, gpost_ref, gnext_ref, xmo_ref, hmo_ref, sub=sub)

    y_scr[n] = jnp.dot(a_ref[...], w_ref[...], preferred_element_type=F32)

    @pl.when(n == n_n - 1)
    def _():
        _finish_rows(y_scr, x_ref, gpost_ref, gnext_ref, xo_ref, ho_ref, sub=sub)


def _proj_residual(a, a_m, w, layer, x, x_m, g_post, g_next, *, bm, bn, name):
    m, kdim = a.shape
    n_n = D_MODEL // bn
    emit_h = g_next is not None
    with_meta = a_m is not None
    row = lambda i, n: (i, 0)
    const = lambda i, n: (0, 0)
    vec = pl.BlockSpec((1, D_MODEL), const)
    w_mode = dict(pipeline_mode=pl.Buffered(1)) if n_n == 1 else {}
    in_specs = [pl.BlockSpec((bm, kdim), row),
                pl.BlockSpec((None, kdim, bn), lambda i, n: (layer, 0, n), **w_mode),
                pl.BlockSpec((bm, D_MODEL), row), vec]
    args = [a, w, x, g_post]
    if emit_h:
        in_specs.append(vec)
        args.append(g_next)
    if with_meta:
        in_specs += [pl.BlockSpec((META_ROWS, kdim), const), pl.BlockSpec((META_ROWS, D_MODEL), const)]
        args += [a_m, x_m]
    out_specs = [pl.BlockSpec((bm, D_MODEL), row)]
    out_shape = [jax.ShapeDtypeStruct((m, D_MODEL), F32)]
    if emit_h:
        out_specs.append(pl.BlockSpec((bm, D_MODEL), row))
        out_shape.append(jax.ShapeDtypeStruct((m, D_MODEL), BF16))
    scratch = [pltpu.VMEM((n_n, bm, bn), F32)]
    if with_meta:
        out_specs.append(pl.BlockSpec((META_ROWS, D_MODEL), const))
        out_shape.append(jax.ShapeDtypeStruct((META_ROWS, D_MODEL), F32))
        if emit_h:
            out_specs.append(pl.BlockSpec((META_ROWS, D_MODEL), const))
            out_shape.append(jax.ShapeDtypeStruct((META_ROWS, D_MODEL), BF16))
        scratch.append(pltpu.VMEM((n_n, META_ROWS, bn), F32))
    w_bufs = 1 if n_n == 1 else 2
    est = (2 * (bm * kdim * 2 + bm * D_MODEL * (4 + 4 + 2)) + w_bufs * kdim * bn * 2 + bm * D_MODEL * 4
           + 2 * bm * bn * 4 + 2 * META_ROWS * (kdim * 2 + 3 * D_MODEL * 4) + (2 << 20))
    outs = pl.pallas_call(
        functools.partial(_proj_residual_kernel, n_n=n_n, emit_h=emit_h, with_meta=with_meta),
        grid=(m // bm, n_n),
        in_specs=in_specs,
        out_specs=out_specs,
        out_shape=out_shape,
        scratch_shapes=scratch,
        compiler_params=_params(("arbitrary", "arbitrary"), est),
        name=name,
    )(*args)
    outs = list(outs)
    x_new = outs.pop(0)
    h_new = outs.pop(0) if emit_h else None
    xm_new = outs.pop(0) if with_meta else None
    hm_new = outs.pop(0) if (with_meta and emit_h) else None
    return x_new, h_new, xm_new, hm_new


def _gelu_tanh(x):
    return 0.5 * x * (1.0 + jnp.tanh(0.7978845608028654 * (x + 0.044715 * (x * x * x))))


def _ffn_in_tile(h_ref, wa_scr, wu_scr, cw_ref, halo, o_ref, a_scr, *, rows):
    h = h_ref[...]
    a = jnp.dot(h, wa_scr[...], preferred_element_type=F32)
    u = jnp.dot(h, wu_scr[...], preferred_element_type=F32)
    a_scr[0:SUBLANES, :] = jnp.zeros((SUBLANES, a.shape[1]), F32) if halo is None else halo
    a_scr[SUBLANES:SUBLANES + rows, :] = a
    conv = cw_ref[2:3, :] * a
    conv = conv + cw_ref[1:2, :] * a_scr[SUBLANES - 1:SUBLANES - 1 + rows, :]
    conv = conv + cw_ref[0:1, :] * a_scr[SUBLANES - 2:SUBLANES - 2 + rows, :]
    o_ref[...] = (_gelu_tanh(conv) * u).astype(BF16)
    return a


def _ffn_in_kernel(h_ref, hm_ref, wa_ref, wu_ref, cw_ref, o_ref, om_ref,
                   wa_scr, wu_scr, a_scr, tail_scr, mtail_scr, *, bm, tiles_per_batch):
    @pl.when(pl.program_id(1) == 0)
    def _():
        wa_scr[...] = wa_ref[...].astype(BF16)
        wu_scr[...] = wu_ref[...].astype(BF16)
        a_m = _ffn_in_tile(hm_ref, wa_scr, wu_scr, cw_ref, None, om_ref, a_scr, rows=META_ROWS)
        mtail_scr[...] = a_m[N_META - SUBLANES:N_META, :]

    first = pl.program_id(1) % tiles_per_batch == 0
    halo = jnp.where(first, mtail_scr[...], tail_scr[...])
    a = _ffn_in_tile(h_ref, wa_scr, wu_scr, cw_ref, halo, o_ref, a_scr, rows=bm)
    tail_scr[...] = a[bm - SUBLANES:bm, :]


def _ffn_in(h, hm, w_ffn_in, conv_w, layer, *, bm, bn, tiles_per_batch):
    m = h.shape[0]
    n_j = D_FF // bn
    est = (2 * (bm * D_MODEL * 2 + 2 * D_MODEL * bn * 4 + bm * bn * 2) + 2 * D_MODEL * bn * 2
           + 9 * bm * bn * 4 + (4 << 20))
    return pl.pallas_call(
        functools.partial(_ffn_in_kernel, bm=bm, tiles_per_batch=tiles_per_batch),
        grid=(n_j, m // bm),
        in_specs=[pl.BlockSpec((bm, D_MODEL), lambda j, i: (i, 0)),
                  pl.BlockSpec((META_ROWS, D_MODEL), lambda j, i: (0, 0)),
                  pl.BlockSpec((None, D_MODEL, bn), lambda j, i: (layer, 0, j)),
                  pl.BlockSpec((None, D_MODEL, bn), lambda j, i: (layer, 0, j + n_j)),
                  pl.BlockSpec((None, CONV_WIDTH, bn), lambda j, i: (layer, 0, j))],
        out_specs=[pl.BlockSpec((bm, bn), lambda j, i: (i, j)),
                   pl.BlockSpec((META_ROWS, bn), lambda j, i: (0, j))],
        out_shape=[jax.ShapeDtypeStruct((m, D_FF), BF16), jax.ShapeDtypeStruct((META_ROWS, D_FF), BF16)],
        scratch_shapes=[pltpu.VMEM((D_MODEL, bn), BF16), pltpu.VMEM((D_MODEL, bn), BF16),
                        pltpu.VMEM((SUBLANES + bm, bn), F32),
                        pltpu.VMEM((SUBLANES, bn), F32), pltpu.VMEM((SUBLANES, bn), F32)],
        compiler_params=_params(("arbitrary", "arbitrary"), est),
        name="ffn_in",
    )(h, hm, w_ffn_in, w_ffn_in, conv_w)


def kernel(x, meta_tokens, norm_pre_mix, norm_post_mix, norm_pre_ffn, norm_post_ffn, w_in, b_if,
           mlstm_head_gain, conv_mix_w, pool_w, pool_scale, w_branch, w_out, w_ffn_in, ffn_conv_w,
           w_ffn_out):
    batch, seq, _ = x.shape
    depth = w_in.shape[0]
    m = batch * seq
    xr = x.reshape(m, D_MODEL)
    xm = jnp.pad(meta_tokens.astype(F32), ((0, META_ROWS - N_META), (0, 0)))

    wo = w_out.astype(BF16)
    wfo = w_ffn_out.astype(BF16)
    w_branch, w_ffn_in, pool_w = (t.astype(F32) for t in (w_branch, w_ffn_in, pool_w))
    w_in_t = jnp.swapaxes(w_in.astype(F32), 1, 2)
    cw = conv_mix_w.astype(F32)
    fcw = ffn_conv_w.astype(F32)

    def vec(v):
        return v.reshape(1, -1).astype(F32)

    hr = _norm(xr, vec(norm_pre_mix[0]), 512)
    hm = _norm(xm, vec(norm_pre_mix[0]), META_ROWS)
    for l in range(depth):
        bias_row = jnp.pad(b_if[l].reshape(1, N_LOGITS).astype(F32), ((0, 0), (0, LANES - N_LOGITS)))
        gain = vec(mlstm_head_gain[l])
        ps = vec(pool_scale[l])
        g_post_mix, g_pre_ffn, g_post_ffn = vec(norm_post_mix[l]), vec(norm_pre_ffn[l]), vec(norm_post_ffn[l])
        last_layer = l + 1 == depth
        g_next = None if last_layer else vec(norm_pre_mix[l + 1])

        big, big_m, logits, logits_m = _in_proj(hr, hm, w_in_t, l, bm=1024, bn=1024)
        ya, ya_m = _mlstm(big, big_m, logits, logits_m, bias_row, gain, batches=batch, chunk=MLSTM_CHUNK)
        yb, yc, yb_m, yc_m = _conv_pool(big, big_m, cw, pool_w, ps, l, bm=512, tiles_per_batch=seq // 512)
        mg, mg_m = _branch((ya, yb, yc), (ya_m, yb_m, yc_m), w_branch, big, big_m, l, bm=1024, bn=512)
        xr, hf, xm, hf_m = _proj_residual(mg, mg_m, wo, l, xr, xm, g_post_mix, g_pre_ffn,
                                          bm=512, bn=D_MODEL, name="out_proj")
        act, act_m = _ffn_in(hf, hf_m, w_ffn_in, fcw, l, bm=1024, bn=512, tiles_per_batch=seq // 1024)
        xr, hr, xm, hm = _proj_residual(act, None if last_layer else act_m, wfo, l, xr, xm, g_post_ffn, g_next,
                                        bm=512, bn=512, name="ffn_out")
    return xr.reshape(batch, seq, D_MODEL)
```

```python
import functools
import math

import jax
import jax.numpy as jnp
import numpy as np
from jax import lax
from jax.experimental import pallas as pl
from jax.experimental.pallas import tpu as pltpu

D_MODEL = 2048
N_META = 16
MIX_WIDTH = 1024
N_BRANCH = 3
HEADS = 8
HEAD_DIM = MIX_WIDTH // HEADS
POOL_WINDOWS = (2, 4, 8, 16)
POOL_GROUP = MIX_WIDTH // len(POOL_WINDOWS)
D_FF = 5632
RMS_EPS = 1e-6
CONV_WIDTH = 3

COL_Q, COL_K, COL_V, COL_O, COL_CB, COL_CC, COL_CX, COL_PU = (n * MIX_WIDTH for n in range(8))
COL_GATES = 8 * MIX_WIDTH
BIG_COLS = COL_GATES + N_BRANCH * D_MODEL
N_LOGITS = 2 * HEADS
LANES = 128
HALO = 16
SUBLANES = 8
META_ROWS = 128
MLSTM_CHUNK = 512
MLSTM_GROUP = 1
VMEM_LIMIT_CAP = 56 * 1024 * 1024

F32 = jnp.float32
BF16 = jnp.bfloat16


def _params(semantics, vmem_bytes):
    return pltpu.CompilerParams(dimension_semantics=semantics,
                                vmem_limit_bytes=min(int(vmem_bytes), VMEM_LIMIT_CAP))


def _rms(x, g):
    return x * lax.rsqrt(jnp.mean(x * x, axis=-1, keepdims=True) + RMS_EPS) * g


def _norm_kernel(x_ref, g_ref, h_ref):
    h_ref[...] = _rms(x_ref[...], g_ref[...]).astype(BF16)


def _norm(x, g, bm):
    m = x.shape[0]
    return pl.pallas_call(
        _norm_kernel,
        grid=(m // bm,),
        in_specs=[pl.BlockSpec((bm, D_MODEL), lambda i: (i, 0)),
                  pl.BlockSpec((1, D_MODEL), lambda i: (0, 0))],
        out_specs=pl.BlockSpec((bm, D_MODEL), lambda i: (i, 0)),
        out_shape=jax.ShapeDtypeStruct((m, D_MODEL), BF16),
        compiler_params=_params(("parallel",), 10 * bm * D_MODEL * 4),
        name="pre_norm",
    )(x, g)


_NT = (((1,), (1,)), ((), ()))


def _in_proj_kernel(h_ref, hm_ref, wt_ref, wl_ref, o_ref, om_ref, lg_ref, lgm_ref, w_scr, wl_scr):
    first_tile = pl.program_id(1) == 0

    @pl.when(first_tile)
    def _():
        w_scr[...] = wt_ref[0].astype(BF16)
        om_ref[...] = lax.dot_general(hm_ref[...], w_scr[...], _NT, preferred_element_type=F32).astype(BF16)

    o_ref[...] = lax.dot_general(h_ref[...], w_scr[...], _NT, preferred_element_type=F32).astype(BF16)

    @pl.when(pl.program_id(0) == 0)
    def _():
        @pl.when(first_tile)
        def _():
            wl_scr[...] = wl_ref[0].astype(BF16)
            lgm_ref[...] = lax.dot_general(hm_ref[...], wl_scr[...], _NT, preferred_element_type=F32)

        lg_ref[...] = lax.dot_general(h_ref[...], wl_scr[...], _NT, preferred_element_type=F32)


def _in_proj(h, hm, w_in_t, layer, *, bm, bn):
    m, k = h.shape
    n_i = m // bm

    def w_rows(j, i):
        skip = jnp.where(j >= COL_CB // bn, N_LOGITS // SUBLANES, 0)
        return layer, (j * (bn // SUBLANES) + skip) * SUBLANES, 0

    est = (2 * (bm * k * 2 + k * bn * 4 + k * LANES * 4 + bm * bn * 2 + bm * LANES * 4) + k * bn * 2
           + 2 * bm * bn * 4 + (4 << 20))
    return pl.pallas_call(
        _in_proj_kernel,
        grid=(BIG_COLS // bn, n_i),
        in_specs=[pl.BlockSpec((bm, k), lambda j, i: (i, 0)),
                  pl.BlockSpec((META_ROWS, k), lambda j, i: (0, 0)),
                  pl.BlockSpec((pl.Element(1), pl.Element(bn), pl.Element(k)), w_rows),
                  pl.BlockSpec((pl.Element(1), pl.Element(LANES), pl.Element(k)), lambda j, i: (layer, COL_CB, 0))],
        out_specs=[pl.BlockSpec((bm, bn), lambda j, i: (i, j)),
                   pl.BlockSpec((META_ROWS, bn), lambda j, i: (0, j)),
                   pl.BlockSpec((bm, LANES), lambda j, i: (jnp.where(j == 0, i, n_i - 1), 0)),
                   pl.BlockSpec((META_ROWS, LANES), lambda j, i: (0, 0))],
        out_shape=[jax.ShapeDtypeStruct((m, BIG_COLS), BF16),
                   jax.ShapeDtypeStruct((META_ROWS, BIG_COLS), BF16),
                   jax.ShapeDtypeStruct((m, LANES), F32),
                   jax.ShapeDtypeStruct((META_ROWS, LANES), F32)],
        scratch_shapes=[pltpu.VMEM((bn, k), BF16), pltpu.VMEM((LANES, k), BF16)],
        compiler_params=_params(("arbitrary", "arbitrary"), est),
        name="in_proj",
    )(h, hm, w_in_t, w_in_t)


def _shift_rows(x, d):
    return pltpu.roll(x, d, axis=0)


def _cumsum_rows(x):
    rows = x.shape[0]
    row = lax.broadcasted_iota(jnp.int32, x.shape, 0)
    d = 1
    while d < rows:
        x = x + jnp.where(row >= d, _shift_rows(x, d), 0.0)
        d *= 2
    return x


def _cummax_rows(x):
    rows = x.shape[0]
    row = lax.broadcasted_iota(jnp.int32, x.shape, 0)
    d = 1
    while d < rows:
        x = jnp.maximum(x, jnp.where(row >= d, _shift_rows(x, d), -jnp.inf))
        d *= 2
    return x


def _log_sigmoid(x):
    return jnp.minimum(x, 0.0) - jnp.log1p(jnp.exp(-jnp.abs(x)))


def _mlstm_chunk(streams, bias_ref, gain_ref, *, chunk, n_valid):
    scale = HEAD_DIM ** -0.5
    last = n_valid - 1
    row1 = lax.broadcasted_iota(jnp.int32, (chunk, LANES), 0)
    gates = []
    for q_ref, k_ref, v_ref, o_ref, gl_ref, y_ref, c_scr, m_scr, s_scr, r_scr in streams:
        gl = gl_ref[...] + bias_ref[...]
        logf = pltpu.roll(_log_sigmoid(gl), LANES - HEADS, axis=1)
        b = _cumsum_rows(logf)
        a = gl - b
        m_prev = m_scr[...]
        g = jnp.maximum(_cummax_rows(a), m_prev)
        inter_w = jnp.exp(m_prev - g) * scale
        exp_neg_m = jnp.exp(-(b + g))
        g_last = g[last:last + 1, :]
        decay = jnp.exp(m_prev - g_last)
        ws = jnp.where(row1 <= last, jnp.exp(a - g_last), 0.0)
        m_scr[...] = b[last:last + 1, :] + g_last
        a_t = (a + math.log(scale)).T
        gates.append((g, a_t, inter_w, exp_neg_m, ws, decay))

    rows = lax.broadcasted_iota(jnp.int32, (chunk, chunk), 0)
    cols = lax.broadcasted_iota(jnp.int32, (chunk, chunk), 1)
    causal = rows >= cols
    ones = jnp.ones((chunk, HEAD_DIM), BF16)
    head = [slice(h * HEAD_DIM, (h + 1) * HEAD_DIM) for h in range(HEADS)]
    wide = [slice(h * 2 * HEAD_DIM, (h + 1) * 2 * HEAD_DIM) for h in range(HEADS)]
    work = [(h, st, gt) for h in range(HEADS) for st, gt in zip(streams, gates)]

    for h, (q_ref, k_ref, _, _, _, _, _, _, s_scr, _), (g, a_t, _, _, _, _) in work:
        w = jnp.where(causal, jnp.exp(a_t[h:h + 1, :] - g[:, h:h + 1]), 0.0)
        s = lax.dot_general(q_ref[:, head[h]], k_ref[:, head[h]], _NT, preferred_element_type=F32)
        s_scr[h, 0:chunk, 0:chunk] = (s * w).astype(BF16)
    for h, (q_ref, _, v_ref, _, _, _, c_scr, _, s_scr, r_scr), (_, _, inter_w, _, _, _) in work:
        v_aug = jnp.concatenate([v_ref[:, head[h]], ones], axis=1)
        r_scr[0:chunk, wide[h]] = (
            jnp.dot(q_ref[:, head[h]], c_scr[h].astype(BF16), preferred_element_type=F32)
            * inter_w[:, h:h + 1]
            + jnp.dot(s_scr[h, 0:chunk, 0:chunk], v_aug, preferred_element_type=F32))
    for h, (_, k_ref, v_ref, _, _, _, c_scr, _, _, _), (_, _, _, _, ws, decay) in work:
        v_aug = jnp.concatenate([v_ref[:, head[h]], ones], axis=1)
        kw = (k_ref[:, head[h]].astype(F32) * ws[:, h:h + 1]).astype(BF16)
        upd = lax.dot_general(kw, v_aug, (((0,), (0,)), ((), ())), preferred_element_type=F32)
        c_scr[h] = decay[:, h:h + 1] * c_scr[h] + upd
    for h, (_, _, _, o_ref, _, y_ref, _, _, _, r_scr), (_, _, _, exp_neg_m, _, _) in work:
        r = r_scr[0:chunk, wide[h]]
        num, den = r[:, :HEAD_DIM], r[:, HEAD_DIM:]
        hh = num / jnp.maximum(jnp.abs(den), exp_neg_m[:, h:h + 1])
        hn = hh * lax.rsqrt(jnp.mean(hh * hh, axis=-1, keepdims=True) + RMS_EPS)
        y_ref[:, head[h]] = (hn * gain_ref[:, head[h]]
                             * jax.nn.sigmoid(o_ref[:, head[h]].astype(F32))).astype(BF16)


def _mlstm_kernel(q_ref, k_ref, v_ref, o_ref, gl_ref, qm_ref, km_ref, vm_ref, om_ref, glm_ref,
                  bias_ref, gain_ref, y_ref, ym_ref, c_scr, m_scr, c0_scr, m0_scr, s_scr, r_scr, *, group, chunk):
    first_chunk = pl.program_id(1) == 0

    @pl.when(jnp.logical_and(pl.program_id(0) == 0, first_chunk))
    def _():
        c0_scr[...] = jnp.zeros_like(c0_scr)
        m0_scr[...] = jnp.zeros_like(m0_scr)
        meta = (qm_ref, km_ref, vm_ref, om_ref, glm_ref, ym_ref, c0_scr, m0_scr, s_scr.at[0], r_scr.at[0])
        _mlstm_chunk([meta], bias_ref, gain_ref, chunk=META_ROWS, n_valid=N_META)

    @pl.when(first_chunk)
    def _():
        for bb in range(group):
            c_scr[bb] = c0_scr[...]
            m_scr[bb] = m0_scr[...]

    streams = [(q_ref.at[bb], k_ref.at[bb], v_ref.at[bb], o_ref.at[bb], gl_ref.at[bb], y_ref.at[bb],
                c_scr.at[bb], m_scr.at[bb], s_scr.at[bb], r_scr.at[bb]) for bb in range(group)]
    _mlstm_chunk(streams, bias_ref, gain_ref, chunk=chunk, n_valid=chunk)


def _mlstm(big, big_m, logits, logits_m, bias_row, gain, *, batches, group, chunk):
    m = big.shape[0]
    seq = m // batches
    big3 = big.reshape(batches, seq, BIG_COLS)
    logits3 = logits.reshape(batches, seq, LANES)
    cblk = [COL_Q // MIX_WIDTH, COL_K // MIX_WIDTH, COL_V // MIX_WIDTH, COL_O // MIX_WIDTH]
    main = [pl.BlockSpec((group, chunk, MIX_WIDTH), functools.partial(lambda b, c, cb: (b, c, cb), cb=cb))
            for cb in cblk]
    meta = [pl.BlockSpec((META_ROWS, MIX_WIDTH), functools.partial(lambda b, c, cb: (0, cb), cb=cb))
            for cb in cblk]
    in_specs = (main + [pl.BlockSpec((group, chunk, LANES), lambda b, c: (b, c, 0))]
                + meta + [pl.BlockSpec((META_ROWS, LANES), lambda b, c: (0, 0)),
                          pl.BlockSpec((1, LANES), lambda b, c: (0, 0)),
                          pl.BlockSpec((1, MIX_WIDTH), lambda b, c: (0, 0))])
    c_shape = (HEADS, HEAD_DIM, 2 * HEAD_DIM)
    est = (group * (12 * chunk * MIX_WIDTH * 2 + HEADS * chunk * chunk * 2 + chunk * HEADS * 2 * HEAD_DIM * 4
                    + HEADS * HEAD_DIM * 2 * HEAD_DIM * 4 + 24 * chunk * LANES * 4) + (16 << 20))
    y, y_m = pl.pallas_call(
        functools.partial(_mlstm_kernel, group=group, chunk=chunk),
        grid=(batches // group, seq // chunk),
        in_specs=in_specs,
        out_specs=[pl.BlockSpec((group, chunk, MIX_WIDTH), lambda b, c: (b, c, 0)),
                   pl.BlockSpec((META_ROWS, MIX_WIDTH), lambda b, c: (0, 0))],
        out_shape=[jax.ShapeDtypeStruct((batches, seq, MIX_WIDTH), BF16),
                   jax.ShapeDtypeStruct((META_ROWS, MIX_WIDTH), BF16)],
        scratch_shapes=[pltpu.VMEM((group,) + c_shape, F32), pltpu.VMEM((group, 1, LANES), F32),
                        pltpu.VMEM(c_shape, F32), pltpu.VMEM((1, LANES), F32),
                        pltpu.VMEM((group, HEADS, chunk, chunk), BF16),
                        pltpu.VMEM((group, chunk, HEADS * 2 * HEAD_DIM), F32)],
        compiler_params=_params(("arbitrary", "arbitrary"), est),
        name="mlstm",
    )(big3, big3, big3, big3, logits3, big_m, big_m, big_m, big_m, logits_m, bias_row, gain)
    return y.reshape(m, MIX_WIDTH), y_m


POOL_BLOCK = 128
CONV_ROWS, CONV_COLS = 128, 256


def _pool_bands():
    bands = np.zeros((len(POOL_WINDOWS), POOL_BLOCK, 2 * POOL_BLOCK), np.float32)
    for grp, win in enumerate(POOL_WINDOWS):
        for r in range(POOL_BLOCK):
            for lag in range(win):
                bands[grp, r, POOL_BLOCK + r - lag] += 1.0 / win
            bands[grp, r, POOL_BLOCK + r] -= 1.0
    return jnp.asarray(bands, BF16)


def _conv3(cb_ref, cc_ref, cx_ref, halo, cw_ref, yb_ref, p_scr, *, rows):
    p_scr[0:HALO, :] = halo
    pieces = [(r, c) for r in range(0, rows, CONV_ROWS) for c in range(0, MIX_WIDTH, CONV_COLS)]
    for r, c in pieces:
        rs, cs = slice(r, r + CONV_ROWS), slice(c, c + CONV_COLS)
        p_scr[HALO + r:HALO + r + CONV_ROWS, cs] = cc_ref[rs, cs].astype(F32) * cx_ref[rs, cs].astype(F32)
    for r, c in pieces:
        rs, cs = slice(r, r + CONV_ROWS), slice(c, c + CONV_COLS)
        conv = cw_ref[2:3, cs] * p_scr[HALO + r:HALO + r + CONV_ROWS, cs]
        conv = conv + cw_ref[1:2, cs] * p_scr[HALO + r - 1:HALO + r - 1 + CONV_ROWS, cs]
        conv = conv + cw_ref[0:1, cs] * p_scr[HALO + r - 2:HALO + r - 2 + CONV_ROWS, cs]
        yb_ref[rs, cs] = (cb_ref[rs, cs].astype(F32) * conv).astype(BF16)


def _pool_from_start(pu_ref, pw_scr, ps_ref, yc_ref, u_scr, *, rows):
    u_scr[0:HALO, :] = jnp.zeros((HALO, MIX_WIDTH), F32)
    u_scr[HALO:HALO + rows, :] = pu_ref[...].astype(F32)
    t = lax.broadcasted_iota(jnp.int32, (rows, 1), 0)
    for grp, win in enumerate(POOL_WINDOWS):
        sl = slice(grp * POOL_GROUP, (grp + 1) * POOL_GROUP)
        cur = u_scr[HALO:HALO + rows, sl]
        tot = cur
        for lag in range(1, win):
            tot = tot + u_scr[HALO - lag:HALO - lag + rows, sl]
        pooled = tot / jnp.minimum(t + 1, win).astype(F32) - cur
        y = jnp.dot(pooled.astype(BF16), pw_scr[grp], preferred_element_type=F32)
        yc_ref[:, sl] = (y * ps_ref[:, sl]).astype(BF16)


def _pool_banded(band_ref, pw_scr, ps_ref, yc_ref, ub_scr, *, rows):
    for grp in range(len(POOL_WINDOWS)):
        sl = slice(grp * POOL_GROUP, (grp + 1) * POOL_GROUP)
        pooled = jnp.concatenate(
            [jnp.dot(band_ref[grp], ub_scr[r:r + 2 * POOL_BLOCK, sl], preferred_element_type=F32)
             for r in range(0, rows, POOL_BLOCK)], axis=0)
        y = jnp.dot(pooled.astype(BF16), pw_scr[grp], preferred_element_type=F32)
        yc_ref[:, sl] = (y * ps_ref[:, sl]).astype(BF16)


def _conv_pool_kernel(cb_ref, cc_ref, cx_ref, pu_ref, ccp_ref, cxp_ref, pup_ref,
                      cbm_ref, ccm_ref, cxm_ref, pum_ref, cw_ref, pw_ref, ps_ref, band_ref,
                      yb_ref, yc_ref, ybm_ref, ycm_ref, pw_scr, p_scr, u_scr, ub_scr, *, bm, tiles_per_batch):
    @pl.when(pl.program_id(0) == 0)
    def _():
        pw_scr[...] = pw_ref[...].astype(BF16)
        _conv3(cbm_ref, ccm_ref, cxm_ref, jnp.zeros((HALO, MIX_WIDTH), F32), cw_ref, ybm_ref, p_scr,
               rows=META_ROWS)
        _pool_from_start(pum_ref, pw_scr, ps_ref, ycm_ref, u_scr, rows=META_ROWS)

    first = pl.program_id(0) % tiles_per_batch == 0
    cch = jnp.where(first, ccm_ref[0:HALO, :], ccp_ref[...]).astype(F32)
    cxh = jnp.where(first, cxm_ref[0:HALO, :], cxp_ref[...]).astype(F32)
    _conv3(cb_ref, cc_ref, cx_ref, cch * cxh, cw_ref, yb_ref, p_scr, rows=bm)
    meta_tail = jnp.concatenate([jnp.zeros((POOL_BLOCK - N_META, MIX_WIDTH), BF16), pum_ref[0:N_META, :]], axis=0)
    ub_scr[0:POOL_BLOCK, :] = jnp.where(first, meta_tail, pup_ref[...])
    ub_scr[POOL_BLOCK:POOL_BLOCK + bm, :] = pu_ref[...]
    _pool_banded(band_ref, pw_scr, ps_ref, yc_ref, ub_scr, rows=bm)


def _conv_pool(big, big_m, conv_w, pool_w, pool_scale, layer, *, bm, tiles_per_batch):
    m = big.shape[0]
    cblk = [COL_CB // MIX_WIDTH, COL_CC // MIX_WIDTH, COL_CX // MIX_WIDTH, COL_PU // MIX_WIDTH]

    def prev_rows(rows, c):
        per = bm // rows
        return pl.BlockSpec((rows, MIX_WIDTH), lambda i: (jnp.maximum(i * per - 1, 0), c))

    in_specs = [pl.BlockSpec((bm, MIX_WIDTH), functools.partial(lambda i, c: (i, c), c=c)) for c in cblk]
    in_specs += [prev_rows(HALO, cblk[1]), prev_rows(HALO, cblk[2]), prev_rows(POOL_BLOCK, cblk[3])]
    in_specs += [pl.BlockSpec((META_ROWS, MIX_WIDTH), functools.partial(lambda i, c: (0, c), c=c))
                 for c in cblk]
    in_specs += [pl.BlockSpec((None, CONV_WIDTH, MIX_WIDTH), lambda i: (layer, 0, 0)),
                 pl.BlockSpec((None, len(POOL_WINDOWS), POOL_GROUP, POOL_GROUP), lambda i: (layer, 0, 0, 0)),
                 pl.BlockSpec((1, MIX_WIDTH), lambda i: (0, 0)),
                 pl.BlockSpec((len(POOL_WINDOWS), POOL_BLOCK, 2 * POOL_BLOCK), lambda i: (0, 0, 0))]
    out_spec = pl.BlockSpec((bm, MIX_WIDTH), lambda i: (i, 0))
    meta_spec = pl.BlockSpec((META_ROWS, MIX_WIDTH), lambda i: (0, 0))
    return pl.pallas_call(
        functools.partial(_conv_pool_kernel, bm=bm, tiles_per_batch=tiles_per_batch),
        grid=(m // bm,),
        in_specs=in_specs,
        out_specs=[out_spec, out_spec, meta_spec, meta_spec],
        out_shape=[jax.ShapeDtypeStruct((m, MIX_WIDTH), BF16)] * 2
        + [jax.ShapeDtypeStruct((META_ROWS, MIX_WIDTH), BF16)] * 2,
        scratch_shapes=[pltpu.VMEM((len(POOL_WINDOWS), POOL_GROUP, POOL_GROUP), BF16),
                        pltpu.VMEM((HALO + max(bm, META_ROWS), MIX_WIDTH), F32),
                        pltpu.VMEM((HALO + META_ROWS, MIX_WIDTH), F32),
                        pltpu.VMEM((POOL_BLOCK + bm, MIX_WIDTH), BF16)],
        compiler_params=_params(("arbitrary",), 14 * bm * MIX_WIDTH * 4 + (10 << 20)),
        name="conv_pool",
    )(big, big, big, big, big, big, big, big_m, big_m, big_m, big_m, conv_w, pool_w, pool_scale, _pool_bands())


def _branch_tile(ya_ref, yb_ref, yc_ref, w_scr, g0_ref, g1_ref, g2_ref, o_ref):
    acc = None
    for y_ref, g_ref, n in ((ya_ref, g0_ref, 0), (yb_ref, g1_ref, 1), (yc_ref, g2_ref, 2)):
        br = jnp.dot(y_ref[...], w_scr[n], preferred_element_type=F32)
        term = jax.nn.sigmoid(g_ref[...].astype(F32)) * br
        acc = term if acc is None else acc + term
    o_ref[...] = acc.astype(BF16)


def _branch_kernel(ya_ref, yb_ref, yc_ref, g0_ref, g1_ref, g2_ref,
                   yam_ref, ybm_ref, ycm_ref, g0m_ref, g1m_ref, g2m_ref, w_ref, o_ref, om_ref, w_scr):
    @pl.when(pl.program_id(1) == 0)
    def _():
        w_scr[...] = w_ref[...].astype(BF16)
        _branch_tile(yam_ref, ybm_ref, ycm_ref, w_scr, g0m_ref, g1m_ref, g2m_ref, om_ref)

    _branch_tile(ya_ref, yb_ref, yc_ref, w_scr, g0_ref, g1_ref, g2_ref, o_ref)


def _branch(ys, ys_m, w_branch, big, big_m, layer, *, bm, bn):
    m = big.shape[0]
    offs = [(COL_GATES + n * D_MODEL) // bn for n in range(N_BRANCH)]
    y_spec = pl.BlockSpec((bm, MIX_WIDTH), lambda j, i: (i, 0))
    ym_spec = pl.BlockSpec((META_ROWS, MIX_WIDTH), lambda j, i: (0, 0))
    g_specs = [pl.BlockSpec((bm, bn), functools.partial(lambda j, i, off: (i, off + j), off=off)) for off in offs]
    gm_specs = [pl.BlockSpec((META_ROWS, bn), functools.partial(lambda j, i, off: (0, off + j), off=off))
                for off in offs]
    est = (2 * (3 * bm * MIX_WIDTH * 2 + 3 * MIX_WIDTH * bn * 4 + 4 * bm * bn * 2) + 3 * MIX_WIDTH * bn * 2
           + 6 * bm * bn * 4 + (4 << 20))
    return pl.pallas_call(
        _branch_kernel,
        grid=(D_MODEL // bn, m // bm),
        in_specs=[y_spec] * 3 + g_specs + [ym_spec] * 3 + gm_specs
        + [pl.BlockSpec((None, N_BRANCH, MIX_WIDTH, bn), lambda j, i: (layer, 0, 0, j))],
        out_specs=[pl.BlockSpec((bm, bn), lambda j, i: (i, j)),
                   pl.BlockSpec((META_ROWS, bn), lambda j, i: (0, j))],
        out_shape=[jax.ShapeDtypeStruct((m, D_MODEL), BF16), jax.ShapeDtypeStruct((META_ROWS, D_MODEL), BF16)],
        scratch_shapes=[pltpu.VMEM((N_BRANCH, MIX_WIDTH, bn), BF16)],
        compiler_params=_params(("arbitrary", "arbitrary"), est),
        name="branch_merge",
    )(*ys, big, big, big, *ys_m, big_m, big_m, big_m, w_branch)


def _finish_rows(y_scr, x_ref, gpost_ref, gnext_ref, xo_ref, ho_ref, *, sub):
    n_n, rows, _ = y_scr.shape

    def rows_step(r, carry):
        rs = pl.ds(pl.multiple_of(r * sub, sub), sub)
        y = jnp.concatenate([y_scr[c, rs, :] for c in range(n_n)], axis=1)
        xn = x_ref[rs, :] + _rms(y, gpost_ref[...])
        xo_ref[rs, :] = xn
        if ho_ref is not None:
            ho_ref[rs, :] = _rms(xn, gnext_ref[...]).astype(BF16)
        return carry
    lax.fori_loop(0, rows // sub, rows_step, 0, unroll=min(4, rows // sub))


def _proj_residual_kernel(*refs, n_n, emit_h, with_meta):
    a_ref, w_ref, x_ref, gpost_ref = refs[:4]
    refs = refs[4:]
    gnext_ref = None
    if emit_h:
        gnext_ref = refs[0]
        refs = refs[1:]
    if with_meta:
        am_ref, xm_ref = refs[:2]
        refs = refs[2:]
    xo_ref = refs[0]
    refs = refs[1:]
    ho_ref = None
    if emit_h:
        ho_ref = refs[0]
        refs = refs[1:]
    if with_meta:
        xmo_ref = refs[0]
        refs = refs[1:]
        hmo_ref = None
        if emit_h:
            hmo_ref = refs[0]
            refs = refs[1:]
        y_scr, ym_scr = refs
    else:
        y_scr, = refs
    n = pl.program_id(1)
    sub = 64

    if with_meta:
        @pl.when(pl.program_id(0) == 0)
        def _():
            ym_scr[n] = jnp.dot(am_ref[...], w_ref[...], preferred_element_type=F32)

            @pl.when(n == n_n - 1)
            def _():
                _finish_rows(ym_scr, xm_ref, gpost_ref, gnext_ref, xmo_ref, hmo_ref, sub=sub)

    y_scr[n] = jnp.dot(a_ref[...], w_ref[...], preferred_element_type=F32)

    @pl.when(n == n_n - 1)
    def _():
        _finish_rows(y_scr, x_ref, gpost_ref, gnext_ref, xo_ref, ho_ref, sub=sub)


def _proj_residual(a, a_m, w, layer, x, x_m, g_post, g_next, *, bm, bn, name):
    m, kdim = a.shape
    n_n = D_MODEL // bn
    emit_h = g_next is not None
    with_meta = a_m is not None
    row = lambda i, n: (i, 0)
    const = lambda i, n: (0, 0)
    vec = pl.BlockSpec((1, D_MODEL), const)
    w_mode = dict(pipeline_mode=pl.Buffered(1)) if n_n == 1 else {}
    in_specs = [pl.BlockSpec((bm, kdim), row),
                pl.BlockSpec((None, kdim, bn), lambda i, n: (layer, 0, n), **w_mode),
                pl.BlockSpec((bm, D_MODEL), row), vec]
    args = [a, w, x, g_post]
    if emit_h:
        in_specs.append(vec)
        args.append(g_next)
    if with_meta:
        in_specs += [pl.BlockSpec((META_ROWS, kdim), const), pl.BlockSpec((META_ROWS, D_MODEL), const)]
        args += [a_m, x_m]
    out_specs = [pl.BlockSpec((bm, D_MODEL), row)]
    out_shape = [jax.ShapeDtypeStruct((m, D_MODEL), F32)]
    if emit_h:
        out_specs.append(pl.BlockSpec((bm, D_MODEL), row))
        out_shape.append(jax.ShapeDtypeStruct((m, D_MODEL), BF16))
    scratch = [pltpu.VMEM((n_n, bm, bn), F32)]
    if with_meta:
        out_specs.append(pl.BlockSpec((META_ROWS, D_MODEL), const))
        out_shape.append(jax.ShapeDtypeStruct((META_ROWS, D_MODEL), F32))
        if emit_h:
            out_specs.append(pl.BlockSpec((META_ROWS, D_MODEL), const))
            out_shape.append(jax.ShapeDtypeStruct((META_ROWS, D_MODEL), BF16))
        scratch.append(pltpu.VMEM((n_n, META_ROWS, bn), F32))
    w_bufs = 1 if n_n == 1 else 2
    est = (2 * (bm * kdim * 2 + bm * D_MODEL * (4 + 4 + 2)) + w_bufs * kdim * bn * 2 + bm * D_MODEL * 4
           + 2 * bm * bn * 4 + 2 * META_ROWS * (kdim * 2 + 3 * D_MODEL * 4) + (2 << 20))
    outs = pl.pallas_call(
        functools.partial(_proj_residual_kernel, n_n=n_n, emit_h=emit_h, with_meta=with_meta),
        grid=(m // bm, n_n),
        in_specs=in_specs,
        out_specs=out_specs,
        out_shape=out_shape,
        scratch_shapes=scratch,
        compiler_params=_params(("arbitrary", "arbitrary"), est),
        name=name,
    )(*args)
    outs = list(outs)
    x_new = outs.pop(0)
    h_new = outs.pop(0) if emit_h else None
    xm_new = outs.pop(0) if with_meta else None
    hm_new = outs.pop(0) if (with_meta and emit_h) else None
    return x_new, h_new, xm_new, hm_new


def _gelu_tanh(x):
    return 0.5 * x * (1.0 + jnp.tanh(0.7978845608028654 * (x + 0.044715 * (x * x * x))))


def _ffn_in_tile(h_ref, wa_scr, wu_scr, cw_ref, halo, o_ref, a_scr, *, rows):
    h = h_ref[...]
    a = jnp.dot(h, wa_scr[...], preferred_element_type=F32)
    u = jnp.dot(h, wu_scr[...], preferred_element_type=F32)
    a_scr[0:SUBLANES, :] = jnp.zeros((SUBLANES, a.shape[1]), F32) if halo is None else halo
    a_scr[SUBLANES:SUBLANES + rows, :] = a
    conv = cw_ref[2:3, :] * a
    conv = conv + cw_ref[1:2, :] * a_scr[SUBLANES - 1:SUBLANES - 1 + rows, :]
    conv = conv + cw_ref[0:1, :] * a_scr[SUBLANES - 2:SUBLANES - 2 + rows, :]
    o_ref[...] = (_gelu_tanh(conv) * u).astype(BF16)
    return a


def _ffn_in_kernel(h_ref, hm_ref, wa_ref, wu_ref, cw_ref, o_ref, om_ref,
                   wa_scr, wu_scr, a_scr, tail_scr, mtail_scr, *, bm, tiles_per_batch):
    @pl.when(pl.program_id(1) == 0)
    def _():
        wa_scr[...] = wa_ref[...].astype(BF16)
        wu_scr[...] = wu_ref[...].astype(BF16)
        a_m = _ffn_in_tile(hm_ref, wa_scr, wu_scr, cw_ref, None, om_ref, a_scr, rows=META_ROWS)
        mtail_scr[...] = a_m[N_META - SUBLANES:N_META, :]

    first = pl.program_id(1) % tiles_per_batch == 0
    halo = jnp.where(first, mtail_scr[...], tail_scr[...])
    a = _ffn_in_tile(h_ref, wa_scr, wu_scr, cw_ref, halo, o_ref, a_scr, rows=bm)
    tail_scr[...] = a[bm - SUBLANES:bm, :]


def _ffn_in(h, hm, w_ffn_in, conv_w, layer, *, bm, bn, tiles_per_batch):
    m = h.shape[0]
    n_j = D_FF // bn
    est = (2 * (bm * D_MODEL * 2 + 2 * D_MODEL * bn * 4 + bm * bn * 2) + 2 * D_MODEL * bn * 2
           + 9 * bm * bn * 4 + (4 << 20))
    return pl.pallas_call(
        functools.partial(_ffn_in_kernel, bm=bm, tiles_per_batch=tiles_per_batch),
        grid=(n_j, m // bm),
        in_specs=[pl.BlockSpec((bm, D_MODEL), lambda j, i: (i, 0)),
                  pl.BlockSpec((META_ROWS, D_MODEL), lambda j, i: (0, 0)),
                  pl.BlockSpec((None, D_MODEL, bn), lambda j, i: (layer, 0, j)),
                  pl.BlockSpec((None, D_MODEL, bn), lambda j, i: (layer, 0, j + n_j)),
                  pl.BlockSpec((None, CONV_WIDTH, bn), lambda j, i: (layer, 0, j))],
        out_specs=[pl.BlockSpec((bm, bn), lambda j, i: (i, j)),
                   pl.BlockSpec((META_ROWS, bn), lambda j, i: (0, j))],
        out_shape=[jax.ShapeDtypeStruct((m, D_FF), BF16), jax.ShapeDtypeStruct((META_ROWS, D_FF), BF16)],
        scratch_shapes=[pltpu.VMEM((D_MODEL, bn), BF16), pltpu.VMEM((D_MODEL, bn), BF16),
                        pltpu.VMEM((SUBLANES + bm, bn), F32),
                        pltpu.VMEM((SUBLANES, bn), F32), pltpu.VMEM((SUBLANES, bn), F32)],
        compiler_params=_params(("arbitrary", "arbitrary"), est),
        name="ffn_in",
    )(h, hm, w_ffn_in, w_ffn_in, conv_w)


def kernel(x, meta_tokens, norm_pre_mix, norm_post_mix, norm_pre_ffn, norm_post_ffn, w_in, b_if,
           mlstm_head_gain, conv_mix_w, pool_w, pool_scale, w_branch, w_out, w_ffn_in, ffn_conv_w,
           w_ffn_out):
    batch, seq, _ = x.shape
    depth = w_in.shape[0]
    m = batch * seq
    xr = x.reshape(m, D_MODEL)
    xm = jnp.pad(meta_tokens.astype(F32), ((0, META_ROWS - N_META), (0, 0)))

    wo = w_out.astype(BF16)
    wfo = w_ffn_out.astype(BF16)
    w_branch, w_ffn_in, pool_w = (t.astype(F32) for t in (w_branch, w_ffn_in, pool_w))
    w_in_t = jnp.swapaxes(w_in.astype(F32), 1, 2)
    cw = conv_mix_w.astype(F32)
    fcw = ffn_conv_w.astype(F32)

    def vec(v):
        return v.reshape(1, -1).astype(F32)

    hr = _norm(xr, vec(norm_pre_mix[0]), 512)
    hm = _norm(xm, vec(norm_pre_mix[0]), META_ROWS)
    for l in range(depth):
        bias_row = jnp.pad(b_if[l].reshape(1, N_LOGITS).astype(F32), ((0, 0), (0, LANES - N_LOGITS)))
        gain = vec(mlstm_head_gain[l])
        ps = vec(pool_scale[l])
        g_post_mix, g_pre_ffn, g_post_ffn = vec(norm_post_mix[l]), vec(norm_pre_ffn[l]), vec(norm_post_ffn[l])
        last_layer = l + 1 == depth
        g_next = None if last_layer else vec(norm_pre_mix[l + 1])

        big, big_m, logits, logits_m = _in_proj(hr, hm, w_in_t, l, bm=1024, bn=1024)
        ya, ya_m = _mlstm(big, big_m, logits, logits_m, bias_row, gain, batches=batch, group=MLSTM_GROUP,
                          chunk=MLSTM_CHUNK)
        yb, yc, yb_m, yc_m = _conv_pool(big, big_m, cw, pool_w, ps, l, bm=512, tiles_per_batch=seq // 512)
        mg, mg_m = _branch((ya, yb, yc), (ya_m, yb_m, yc_m), w_branch, big, big_m, l, bm=1024, bn=512)
        xr, hf, xm, hf_m = _proj_residual(mg, mg_m, wo, l, xr, xm, g_post_mix, g_pre_ffn,
                                          bm=512, bn=D_MODEL, name="out_proj")
        act, act_m = _ffn_in(hf, hf_m, w_ffn_in, fcw, l, bm=1024, bn=512, tiles_per_batch=seq // 1024)
        xr, hr, xm, hm = _proj_residual(act, None if last_layer else act_m, wfo, l, xr, xm, g_post_ffn, g_next,
                                        bm=512, bn=512, name="ffn_out")
    return xr.reshape(batch, seq, D_MODEL)
```

```python
import functools
import math

import jax
import jax.numpy as jnp
import numpy as np
from jax import lax
from jax.experimental import pallas as pl
from jax.experimental.pallas import tpu as pltpu

D_MODEL = 2048
N_META = 16
MIX_WIDTH = 1024
N_BRANCH = 3
HEADS = 8
HEAD_DIM = MIX_WIDTH // HEADS
POOL_WINDOWS = (2, 4, 8, 16)
POOL_GROUP = MIX_WIDTH // len(POOL_WINDOWS)
D_FF = 5632
RMS_EPS = 1e-6
CONV_WIDTH = 3

COL_Q, COL_K, COL_V, COL_O, COL_CB, COL_CC, COL_CX, COL_PU = (n * MIX_WIDTH for n in range(8))
COL_GATES = 8 * MIX_WIDTH
BIG_COLS = COL_GATES + N_BRANCH * D_MODEL
N_LOGITS = 2 * HEADS
LANES = 128
HALO = 16
SUBLANES = 8
META_ROWS = 128
MLSTM_CHUNK = 512
MLSTM_GROUP = 1
VMEM_LIMIT_CAP = 56 * 1024 * 1024

F32 = jnp.float32
BF16 = jnp.bfloat16


def _params(semantics, vmem_bytes):
    return pltpu.CompilerParams(dimension_semantics=semantics,
                                vmem_limit_bytes=min(int(vmem_bytes), VMEM_LIMIT_CAP))


def _rms(x, g):
    return x * lax.rsqrt(jnp.mean(x * x, axis=-1, keepdims=True) + RMS_EPS) * g


def _norm_kernel(x_ref, g_ref, h_ref):
    h_ref[...] = _rms(x_ref[...], g_ref[...]).astype(BF16)


def _norm(x, g, bm):
    m = x.shape[0]
    return pl.pallas_call(
        _norm_kernel,
        grid=(m // bm,),
        in_specs=[pl.BlockSpec((bm, D_MODEL), lambda i: (i, 0)),
                  pl.BlockSpec((1, D_MODEL), lambda i: (0, 0))],
        out_specs=pl.BlockSpec((bm, D_MODEL), lambda i: (i, 0)),
        out_shape=jax.ShapeDtypeStruct((m, D_MODEL), BF16),
        compiler_params=_params(("parallel",), 10 * bm * D_MODEL * 4),
        name="pre_norm",
    )(x, g)


_NT = (((1,), (1,)), ((), ()))


def _in_proj_kernel(h_ref, hm_ref, wt_ref, wl_ref, o_ref, om_ref, lg_ref, lgm_ref, w_scr, wl_scr):
    first_tile = pl.program_id(1) == 0

    @pl.when(first_tile)
    def _():
        w_scr[...] = wt_ref[0].astype(BF16)
        om_ref[...] = lax.dot_general(hm_ref[...], w_scr[...], _NT, preferred_element_type=F32).astype(BF16)

    o_ref[...] = lax.dot_general(h_ref[...], w_scr[...], _NT, preferred_element_type=F32).astype(BF16)

    @pl.when(pl.program_id(0) == 0)
    def _():
        @pl.when(first_tile)
        def _():
            wl_scr[...] = wl_ref[0].astype(BF16)
            lgm_ref[...] = lax.dot_general(hm_ref[...], wl_scr[...], _NT, preferred_element_type=F32)

        lg_ref[...] = lax.dot_general(h_ref[...], wl_scr[...], _NT, preferred_element_type=F32)


def _in_proj(h, hm, w_in_t, layer, *, bm, bn):
    m, k = h.shape
    n_i = m // bm

    def w_rows(j, i):
        skip = jnp.where(j >= COL_CB // bn, N_LOGITS // SUBLANES, 0)
        return layer, (j * (bn // SUBLANES) + skip) * SUBLANES, 0

    est = (2 * (bm * k * 2 + k * bn * 4 + k * LANES * 4 + bm * bn * 2 + bm * LANES * 4) + k * bn * 2
           + 2 * bm * bn * 4 + (4 << 20))
    return pl.pallas_call(
        _in_proj_kernel,
        grid=(BIG_COLS // bn, n_i),
        in_specs=[pl.BlockSpec((bm, k), lambda j, i: (i, 0)),
                  pl.BlockSpec((META_ROWS, k), lambda j, i: (0, 0)),
                  pl.BlockSpec((pl.Element(1), pl.Element(bn), pl.Element(k)), w_rows),
                  pl.BlockSpec((pl.Element(1), pl.Element(LANES), pl.Element(k)), lambda j, i: (layer, COL_CB, 0))],
        out_specs=[pl.BlockSpec((bm, bn), lambda j, i: (i, j)),
                   pl.BlockSpec((META_ROWS, bn), lambda j, i: (0, j)),
                   pl.BlockSpec((bm, LANES), lambda j, i: (jnp.where(j == 0, i, n_i - 1), 0)),
                   pl.BlockSpec((META_ROWS, LANES), lambda j, i: (0, 0))],
        out_shape=[jax.ShapeDtypeStruct((m, BIG_COLS), BF16),
                   jax.ShapeDtypeStruct((META_ROWS, BIG_COLS), BF16),
                   jax.ShapeDtypeStruct((m, LANES), F32),
                   jax.ShapeDtypeStruct((META_ROWS, LANES), F32)],
        scratch_shapes=[pltpu.VMEM((bn, k), BF16), pltpu.VMEM((LANES, k), BF16)],
        compiler_params=_params(("arbitrary", "arbitrary"), est),
        name="in_proj",
    )(h, hm, w_in_t, w_in_t)


def _shift_rows(x, d):
    return pltpu.roll(x, d, axis=0)


def _cumsum_rows(x):
    rows = x.shape[0]
    row = lax.broadcasted_iota(jnp.int32, x.shape, 0)
    d = 1
    while d < rows:
        x = x + jnp.where(row >= d, _shift_rows(x, d), 0.0)
        d *= 2
    return x


def _cummax_rows(x):
    rows = x.shape[0]
    row = lax.broadcasted_iota(jnp.int32, x.shape, 0)
    d = 1
    while d < rows:
        x = jnp.maximum(x, jnp.where(row >= d, _shift_rows(x, d), -jnp.inf))
        d *= 2
    return x


def _log_sigmoid(x):
    return jnp.minimum(x, 0.0) - jnp.log1p(jnp.exp(-jnp.abs(x)))


def _mlstm_chunk(streams, bias_ref, gain_ref, *, chunk, n_valid):
    scale = HEAD_DIM ** -0.5
    last = n_valid - 1
    row1 = lax.broadcasted_iota(jnp.int32, (chunk, LANES), 0)
    gates = []
    for q_ref, k_ref, v_ref, o_ref, gl_ref, y_ref, c_scr, m_scr, s_scr, r_scr in streams:
        gl = gl_ref[...] + bias_ref[...]
        logf = pltpu.roll(_log_sigmoid(gl), LANES - HEADS, axis=1)
        b = _cumsum_rows(logf)
        a = gl - b
        m_prev = m_scr[...]
        g = jnp.maximum(_cummax_rows(a), m_prev)
        inter_w = jnp.exp(m_prev - g) * scale
        exp_neg_m = jnp.exp(-(b + g))
        g_last = g[last:last + 1, :]
        decay = jnp.exp(m_prev - g_last)
        ws = jnp.where(row1 <= last, jnp.exp(a - g_last), 0.0)
        m_scr[...] = b[last:last + 1, :] + g_last
        a_t = (a + math.log(scale)).T
        gates.append((g, a_t, inter_w, exp_neg_m, ws, decay))

    rows = lax.broadcasted_iota(jnp.int32, (chunk, chunk), 0)
    cols = lax.broadcasted_iota(jnp.int32, (chunk, chunk), 1)
    causal = rows >= cols
    ones = jnp.ones((chunk, HEAD_DIM), BF16)
    head = [slice(h * HEAD_DIM, (h + 1) * HEAD_DIM) for h in range(HEADS)]
    wide = [slice(h * 2 * HEAD_DIM, (h + 1) * 2 * HEAD_DIM) for h in range(HEADS)]
    work = [(h, st, gt) for h in range(HEADS) for st, gt in zip(streams, gates)]

    for h, (q_ref, k_ref, _, _, _, _, _, _, s_scr, _), (g, a_t, _, _, _, _) in work:
        w = jnp.where(causal, jnp.exp(a_t[h:h + 1, :] - g[:, h:h + 1]), 0.0)
        s = lax.dot_general(q_ref[:, head[h]], k_ref[:, head[h]], _NT, preferred_element_type=F32)
        s_scr[h, 0:chunk, 0:chunk] = (s * w).astype(BF16)
    for h, (q_ref, _, v_ref, _, _, _, c_scr, _, s_scr, r_scr), (_, _, inter_w, _, _, _) in work:
        v_aug = jnp.concatenate([v_ref[:, head[h]], ones], axis=1)
        r_scr[0:chunk, wide[h]] = (
            jnp.dot(q_ref[:, head[h]], c_scr[h].astype(BF16), preferred_element_type=F32)
            * inter_w[:, h:h + 1]
            + jnp.dot(s_scr[h, 0:chunk, 0:chunk], v_aug, preferred_element_type=F32))
    for h, (_, k_ref, v_ref, _, _, _, c_scr, _, _, _), (_, _, _, _, ws, decay) in work:
        v_aug = jnp.concatenate([v_ref[:, head[h]], ones], axis=1)
        kw = (k_ref[:, head[h]].astype(F32) * ws[:, h:h + 1]).astype(BF16)
        upd = lax.dot_general(kw, v_aug, (((0,), (0,)), ((), ())), preferred_element_type=F32)
        c_scr[h] = decay[:, h:h + 1] * c_scr[h] + upd
    for h, (_, _, _, o_ref, _, y_ref, _, _, _, r_scr), (_, _, _, exp_neg_m, _, _) in work:
        r = r_scr[0:chunk, wide[h]]
        num, den = r[:, :HEAD_DIM], r[:, HEAD_DIM:]
        hh = num / jnp.maximum(jnp.abs(den), exp_neg_m[:, h:h + 1])
        hn = hh * lax.rsqrt(jnp.mean(hh * hh, axis=-1, keepdims=True) + RMS_EPS)
        y_ref[:, head[h]] = (hn * gain_ref[:, head[h]]
                             * jax.nn.sigmoid(o_ref[:, head[h]].astype(F32))).astype(BF16)


def _mlstm_kernel(q_ref, k_ref, v_ref, o_ref, gl_ref, qm_ref, km_ref, vm_ref, om_ref, glm_ref,
                  bias_ref, gain_ref, y_ref, ym_ref, c_scr, m_scr, c0_scr, m0_scr, s_scr, r_scr, *, group, chunk):
    first_chunk = pl.program_id(1) == 0

    @pl.when(jnp.logical_and(pl.program_id(0) == 0, first_chunk))
    def _():
        c0_scr[...] = jnp.zeros_like(c0_scr)
        m0_scr[...] = jnp.zeros_like(m0_scr)
        meta = (qm_ref, km_ref, vm_ref, om_ref, glm_ref, ym_ref, c0_scr, m0_scr, s_scr.at[0], r_scr.at[0])
        _mlstm_chunk([meta], bias_ref, gain_ref, chunk=META_ROWS, n_valid=N_META)

    @pl.when(first_chunk)
    def _():
        for bb in range(group):
            c_scr[bb] = c0_scr[...]
            m_scr[bb] = m0_scr[...]

    streams = [(q_ref.at[bb], k_ref.at[bb], v_ref.at[bb], o_ref.at[bb], gl_ref.at[bb], y_ref.at[bb],
                c_scr.at[bb], m_scr.at[bb], s_scr.at[bb], r_scr.at[bb]) for bb in range(group)]
    _mlstm_chunk(streams, bias_ref, gain_ref, chunk=chunk, n_valid=chunk)


def _mlstm(big, big_m, logits, logits_m, bias_row, gain, *, batches, group, chunk):
    m = big.shape[0]
    seq = m // batches
    big3 = big.reshape(batches, seq, BIG_COLS)
    logits3 = logits.reshape(batches, seq, LANES)
    cblk = [COL_Q // MIX_WIDTH, COL_K // MIX_WIDTH, COL_V // MIX_WIDTH, COL_O // MIX_WIDTH]
    main = [pl.BlockSpec((group, chunk, MIX_WIDTH), functools.partial(lambda b, c, cb: (b, c, cb), cb=cb))
            for cb in cblk]
    meta = [pl.BlockSpec((META_ROWS, MIX_WIDTH), functools.partial(lambda b, c, cb: (0, cb), cb=cb))
            for cb in cblk]
    in_specs = (main + [pl.BlockSpec((group, chunk, LANES), lambda b, c: (b, c, 0))]
                + meta + [pl.BlockSpec((META_ROWS, LANES), lambda b, c: (0, 0)),
                          pl.BlockSpec((1, LANES), lambda b, c: (0, 0)),
                          pl.BlockSpec((1, MIX_WIDTH), lambda b, c: (0, 0))])
    c_shape = (HEADS, HEAD_DIM, 2 * HEAD_DIM)
    est = (group * (12 * chunk * MIX_WIDTH * 2 + HEADS * chunk * chunk * 2 + chunk * HEADS * 2 * HEAD_DIM * 4
                    + HEADS * HEAD_DIM * 2 * HEAD_DIM * 4 + 24 * chunk * LANES * 4) + (16 << 20))
    y, y_m = pl.pallas_call(
        functools.partial(_mlstm_kernel, group=group, chunk=chunk),
        grid=(batches // group, seq // chunk),
        in_specs=in_specs,
        out_specs=[pl.BlockSpec((group, chunk, MIX_WIDTH), lambda b, c: (b, c, 0)),
                   pl.BlockSpec((META_ROWS, MIX_WIDTH), lambda b, c: (0, 0))],
        out_shape=[jax.ShapeDtypeStruct((batches, seq, MIX_WIDTH), BF16),
                   jax.ShapeDtypeStruct((META_ROWS, MIX_WIDTH), BF16)],
        scratch_shapes=[pltpu.VMEM((group,) + c_shape, F32), pltpu.VMEM((group, 1, LANES), F32),
                        pltpu.VMEM(c_shape, F32), pltpu.VMEM((1, LANES), F32),
                        pltpu.VMEM((group, HEADS, chunk, chunk), BF16),
                        pltpu.VMEM((group, chunk, HEADS * 2 * HEAD_DIM), F32)],
        compiler_params=_params(("arbitrary", "arbitrary"), est),
        name="mlstm",
    )(big3, big3, big3, big3, logits3, big_m, big_m, big_m, big_m, logits_m, bias_row, gain)
    return y.reshape(m, MIX_WIDTH), y_m


POOL_BLOCK = 128
CONV_ROWS, CONV_COLS = 128, 256


def _pool_bands():
    bands = np.zeros((len(POOL_WINDOWS), POOL_BLOCK, 2 * POOL_BLOCK), np.float32)
    for grp, win in enumerate(POOL_WINDOWS):
        for r in range(POOL_BLOCK):
            for lag in range(win):
                bands[grp, r, POOL_BLOCK + r - lag] += 1.0 / win
            bands[grp, r, POOL_BLOCK + r] -= 1.0
    return jnp.asarray(bands, BF16)


def _conv3(cb_ref, cc_ref, cx_ref, halo, cw_ref, yb_ref, p_scr, *, rows):
    p_scr[0:HALO, :] = halo
    pieces = [(r, c) for r in range(0, rows, CONV_ROWS) for c in range(0, MIX_WIDTH, CONV_COLS)]
    for r, c in pieces:
        rs, cs = slice(r, r + CONV_ROWS), slice(c, c + CONV_COLS)
        p_scr[HALO + r:HALO + r + CONV_ROWS, cs] = cc_ref[rs, cs].astype(F32) * cx_ref[rs, cs].astype(F32)
    for r, c in pieces:
        rs, cs = slice(r, r + CONV_ROWS), slice(c, c + CONV_COLS)
        conv = cw_ref[2:3, cs] * p_scr[HALO + r:HALO + r + CONV_ROWS, cs]
        conv = conv + cw_ref[1:2, cs] * p_scr[HALO + r - 1:HALO + r - 1 + CONV_ROWS, cs]
        conv = conv + cw_ref[0:1, cs] * p_scr[HALO + r - 2:HALO + r - 2 + CONV_ROWS, cs]
        yb_ref[rs, cs] = (cb_ref[rs, cs].astype(F32) * conv).astype(BF16)


def _pool_from_start(pu_ref, pw_scr, ps_ref, yc_ref, u_scr, *, rows):
    u_scr[0:HALO, :] = jnp.zeros((HALO, MIX_WIDTH), F32)
    u_scr[HALO:HALO + rows, :] = pu_ref[...].astype(F32)
    t = lax.broadcasted_iota(jnp.int32, (rows, 1), 0)
    for grp, win in enumerate(POOL_WINDOWS):
        sl = slice(grp * POOL_GROUP, (grp + 1) * POOL_GROUP)
        cur = u_scr[HALO:HALO + rows, sl]
        tot = cur
        for lag in range(1, win):
            tot = tot + u_scr[HALO - lag:HALO - lag + rows, sl]
        pooled = tot / jnp.minimum(t + 1, win).astype(F32) - cur
        y = jnp.dot(pooled.astype(BF16), pw_scr[grp], preferred_element_type=F32)
        yc_ref[:, sl] = (y * ps_ref[:, sl]).astype(BF16)


def _pool_banded(band_ref, pw_scr, ps_ref, yc_ref, ub_scr, *, rows):
    for grp in range(len(POOL_WINDOWS)):
        sl = slice(grp * POOL_GROUP, (grp + 1) * POOL_GROUP)
        pooled = jnp.concatenate(
            [jnp.dot(band_ref[grp], ub_scr[r:r + 2 * POOL_BLOCK, sl], preferred_element_type=F32)
             for r in range(0, rows, POOL_BLOCK)], axis=0)
        y = jnp.dot(pooled.astype(BF16), pw_scr[grp], preferred_element_type=F32)
        yc_ref[:, sl] = (y * ps_ref[:, sl]).astype(BF16)


def _conv_pool_kernel(cb_ref, cc_ref, cx_ref, pu_ref, ccp_ref, cxp_ref, pup_ref,
                      cbm_ref, ccm_ref, cxm_ref, pum_ref, cw_ref, pw_ref, ps_ref, band_ref,
                      yb_ref, yc_ref, ybm_ref, ycm_ref, pw_scr, p_scr, u_scr, ub_scr, *, bm, tiles_per_batch):
    @pl.when(pl.program_id(0) == 0)
    def _():
        pw_scr[...] = pw_ref[...].astype(BF16)
        _conv3(cbm_ref, ccm_ref, cxm_ref, jnp.zeros((HALO, MIX_WIDTH), F32), cw_ref, ybm_ref, p_scr,
               rows=META_ROWS)
        _pool_from_start(pum_ref, pw_scr, ps_ref, ycm_ref, u_scr, rows=META_ROWS)

    first = pl.program_id(0) % tiles_per_batch == 0
    cch = jnp.where(first, ccm_ref[0:HALO, :], ccp_ref[...]).astype(F32)
    cxh = jnp.where(first, cxm_ref[0:HALO, :], cxp_ref[...]).astype(F32)
    _conv3(cb_ref, cc_ref, cx_ref, cch * cxh, cw_ref, yb_ref, p_scr, rows=bm)
    meta_tail = jnp.concatenate([jnp.zeros((POOL_BLOCK - N_META, MIX_WIDTH), BF16), pum_ref[0:N_META, :]], axis=0)
    ub_scr[0:POOL_BLOCK, :] = jnp.where(first, meta_tail, pup_ref[...])
    ub_scr[POOL_BLOCK:POOL_BLOCK + bm, :] = pu_ref[...]
    _pool_banded(band_ref, pw_scr, ps_ref, yc_ref, ub_scr, rows=bm)


def _conv_pool(big, big_m, conv_w, pool_w, pool_scale, layer, *, bm, tiles_per_batch):
    m = big.shape[0]
    cblk = [COL_CB // MIX_WIDTH, COL_CC // MIX_WIDTH, COL_CX // MIX_WIDTH, COL_PU // MIX_WIDTH]

    def prev_rows(rows, c):
        per = bm // rows
        return pl.BlockSpec((rows, MIX_WIDTH), lambda i: (jnp.maximum(i * per - 1, 0), c))

    in_specs = [pl.BlockSpec((bm, MIX_WIDTH), functools.partial(lambda i, c: (i, c), c=c)) for c in cblk]
    in_specs += [prev_rows(HALO, cblk[1]), prev_rows(HALO, cblk[2]), prev_rows(POOL_BLOCK, cblk[3])]
    in_specs += [pl.BlockSpec((META_ROWS, MIX_WIDTH), functools.partial(lambda i, c: (0, c), c=c))
                 for c in cblk]
    in_specs += [pl.BlockSpec((None, CONV_WIDTH, MIX_WIDTH), lambda i: (layer, 0, 0)),
                 pl.BlockSpec((None, len(POOL_WINDOWS), POOL_GROUP, POOL_GROUP), lambda i: (layer, 0, 0, 0)),
                 pl.BlockSpec((1, MIX_WIDTH), lambda i: (0, 0)),
                 pl.BlockSpec((len(POOL_WINDOWS), POOL_BLOCK, 2 * POOL_BLOCK), lambda i: (0, 0, 0))]
    out_spec = pl.BlockSpec((bm, MIX_WIDTH), lambda i: (i, 0))
    meta_spec = pl.BlockSpec((META_ROWS, MIX_WIDTH), lambda i: (0, 0))
    return pl.pallas_call(
        functools.partial(_conv_pool_kernel, bm=bm, tiles_per_batch=tiles_per_batch),
        grid=(m // bm,),
        in_specs=in_specs,
        out_specs=[out_spec, out_spec, meta_spec, meta_spec],
        out_shape=[jax.ShapeDtypeStruct((m, MIX_WIDTH), BF16)] * 2
        + [jax.ShapeDtypeStruct((META_ROWS, MIX_WIDTH), BF16)] * 2,
        scratch_shapes=[pltpu.VMEM((len(POOL_WINDOWS), POOL_GROUP, POOL_GROUP), BF16),
                        pltpu.VMEM((HALO + max(bm, META_ROWS), MIX_WIDTH), F32),
                        pltpu.VMEM((HALO + META_ROWS, MIX_WIDTH), F32),
                        pltpu.VMEM((POOL_BLOCK + bm, MIX_WIDTH), BF16)],
        compiler_params=_params(("arbitrary",), 14 * bm * MIX_WIDTH * 4 + (10 << 20)),
        name="conv_pool",
    )(big, big, big, big, big, big, big, big_m, big_m, big_m, big_m, conv_w, pool_w, pool_scale, _pool_bands())


def _branch_tile(ya_ref, yb_ref, yc_ref, w_scr, g0_ref, g1_ref, g2_ref, o_ref):
    acc = None
    for y_ref, g_ref, n in ((ya_ref, g0_ref, 0), (yb_ref, g1_ref, 1), (yc_ref, g2_ref, 2)):
        br = jnp.dot(y_ref[...], w_scr[n], preferred_element_type=F32)
        term = jax.nn.sigmoid(g_ref[...].astype(F32)) * br
        acc = term if acc is None else acc + term
    o_ref[...] = acc.astype(BF16)


def _branch_kernel(ya_ref, yb_ref, yc_ref, g0_ref, g1_ref, g2_ref,
                   yam_ref, ybm_ref, ycm_ref, g0m_ref, g1m_ref, g2m_ref, w_ref, o_ref, om_ref, w_scr):
    @pl.when(pl.program_id(1) == 0)
    def _():
        w_scr[...] = w_ref[...].astype(BF16)
        _branch_tile(yam_ref, ybm_ref, ycm_ref, w_scr, g0m_ref, g1m_ref, g2m_ref, om_ref)

    _branch_tile(ya_ref, yb_ref, yc_ref, w_scr, g0_ref, g1_ref, g2_ref, o_ref)


def _branch(ys, ys_m, w_branch, big, big_m, layer, *, bm, bn):
    m = big.shape[0]
    offs = [(COL_GATES + n * D_MODEL) // bn for n in range(N_BRANCH)]
    y_spec = pl.BlockSpec((bm, MIX_WIDTH), lambda j, i: (i, 0))
    ym_spec = pl.BlockSpec((META_ROWS, MIX_WIDTH), lambda j, i: (0, 0))
    g_specs = [pl.BlockSpec((bm, bn), functools.partial(lambda j, i, off: (i, off + j), off=off)) for off in offs]
    gm_specs = [pl.BlockSpec((META_ROWS, bn), functools.partial(lambda j, i, off: (0, off + j), off=off))
                for off in offs]
    est = (2 * (3 * bm * MIX_WIDTH * 2 + 3 * MIX_WIDTH * bn * 4 + 4 * bm * bn * 2) + 3 * MIX_WIDTH * bn * 2
           + 6 * bm * bn * 4 + (4 << 20))
    return pl.pallas_call(
        _branch_kernel,
        grid=(D_MODEL // bn, m // bm),
        in_specs=[y_spec] * 3 + g_specs + [ym_spec] * 3 + gm_specs
        + [pl.BlockSpec((None, N_BRANCH, MIX_WIDTH, bn), lambda j, i: (layer, 0, 0, j))],
        out_specs=[pl.BlockSpec((bm, bn), lambda j, i: (i, j)),
                   pl.BlockSpec((META_ROWS, bn), lambda j, i: (0, j))],
        out_shape=[jax.ShapeDtypeStruct((m, D_MODEL), BF16), jax.ShapeDtypeStruct((META_ROWS, D_MODEL), BF16)],
        scratch_shapes=[pltpu.VMEM((N_BRANCH, MIX_WIDTH, bn), BF16)],
        compiler_params=_params(("arbitrary", "arbitrary"), est),
        name="branch_merge",
    )(*ys, big, big, big, *ys_m, big_m, big_m, big_m, w_branch)


FINISH_ROWS = 64


def _finish_rows(y_scr, slot, x_ref, gpost_ref, gnext_ref, xo_ref, ho_ref, *, row0, rows):
    n_n = y_scr.shape[1]
    for s in range(0, rows, FINISH_ROWS):
        if isinstance(row0, int):
            ys = slice(row0 + s, row0 + s + FINISH_ROWS)
        else:
            ys = pl.ds(pl.multiple_of(row0 + s, FINISH_ROWS), FINISH_ROWS)
        y = jnp.concatenate([y_scr[slot, c, ys, :] for c in range(n_n)], axis=1)
        xn = x_ref[s:s + FINISH_ROWS, :] + _rms(y, gpost_ref[...])
        xo_ref[s:s + FINISH_ROWS, :] = xn
        if ho_ref is not None:
            ho_ref[s:s + FINISH_ROWS, :] = _rms(xn, gnext_ref[...]).astype(BF16)


def _proj_residual_kernel(*refs, n_i, n_n, part, emit_h, with_meta):
    a_ref, w_ref, x_ref, gpost_ref = refs[:4]
    refs = refs[4:]
    gnext_ref = None
    if emit_h:
        gnext_ref = refs[0]
        refs = refs[1:]
    if with_meta:
        am_ref, xm_ref = refs[:2]
        refs = refs[2:]
    xo_ref = refs[0]
    refs = refs[1:]
    ho_ref = None
    if emit_h:
        ho_ref = refs[0]
        refs = refs[1:]
    if with_meta:
        xmo_ref = refs[0]
        refs = refs[1:]
        hmo_ref = None
        if emit_h:
            hmo_ref = refs[0]
            refs = refs[1:]
        y_scr, ym_scr = refs
    else:
        y_scr, = refs
    i = pl.program_id(0)
    n = pl.program_id(1) if n_n > 1 else 0

    def step(slot, do_matmul, do_finish):
        if do_finish:
            _finish_rows(y_scr, 1 - slot, x_ref, gpost_ref, gnext_ref, xo_ref, ho_ref, row0=n * part, rows=part)
        if do_matmul:
            y_scr[slot, n] = jnp.dot(a_ref[...], w_ref[...], preferred_element_type=F32)

    @pl.when(i == 0)
    def _():
        if with_meta:
            ym_scr[0, n] = jnp.dot(am_ref[...], w_ref[...], preferred_element_type=F32)

            def finish_meta():
                _finish_rows(ym_scr, 0, xm_ref, gpost_ref, gnext_ref, xmo_ref, hmo_ref, row0=0, rows=META_ROWS)
            if n_n > 1:
                pl.when(n == n_n - 1)(finish_meta)
            else:
                finish_meta()
        step(0, True, False)

    for parity in (0, 1):
        @pl.when(jnp.logical_and(jnp.logical_and(i > 0, i < n_i), lax.rem(i, 2) == parity))
        def _():
            step(parity, True, True)

    @pl.when(i == n_i)
    def _():
        step(n_i % 2, False, True)


def _proj_residual(a, a_m, w, layer, x, x_m, g_post, g_next, *, bm, bn, name):
    m, kdim = a.shape
    n_i = m // bm
    n_n = D_MODEL // bn
    part = bm // n_n
    emit_h = g_next is not None
    with_meta = a_m is not None
    const = lambda i, n: (0, 0)
    prev_part = lambda i, n: (jnp.where(i == 0, 0, (i - 1) * n_n + n), 0)
    vec = pl.BlockSpec((1, D_MODEL), const)
    w_mode = dict(pipeline_mode=pl.Buffered(1)) if n_n == 1 else {}
    in_specs = [pl.BlockSpec((bm, kdim), lambda i, n: (jnp.minimum(i, n_i - 1), 0)),
                pl.BlockSpec((None, kdim, bn), lambda i, n: (layer, 0, n), **w_mode),
                pl.BlockSpec((part, D_MODEL), prev_part), vec]
    args = [a, w, x, g_post]
    if emit_h:
        in_specs.append(vec)
        args.append(g_next)
    if with_meta:
        in_specs += [pl.BlockSpec((META_ROWS, kdim), const), pl.BlockSpec((META_ROWS, D_MODEL), const)]
        args += [a_m, x_m]
    out_specs = [pl.BlockSpec((part, D_MODEL), prev_part)]
    out_shape = [jax.ShapeDtypeStruct((m, D_MODEL), F32)]
    if emit_h:
        out_specs.append(pl.BlockSpec((part, D_MODEL), prev_part))
        out_shape.append(jax.ShapeDtypeStruct((m, D_MODEL), BF16))
    scratch = [pltpu.VMEM((2, n_n, bm, bn), F32)]
    if with_meta:
        out_specs.append(pl.BlockSpec((META_ROWS, D_MODEL), const))
        out_shape.append(jax.ShapeDtypeStruct((META_ROWS, D_MODEL), F32))
        if emit_h:
            out_specs.append(pl.BlockSpec((META_ROWS, D_MODEL), const))
            out_shape.append(jax.ShapeDtypeStruct((META_ROWS, D_MODEL), BF16))
        scratch.append(pltpu.VMEM((1, n_n, META_ROWS, bn), F32))
    w_bufs = 1 if n_n == 1 else 2
    est = (2 * (bm * kdim * 2 + part * D_MODEL * (4 + 4 + 2)) + w_bufs * kdim * bn * 2 + 2 * bm * D_MODEL * 4
           + 2 * bm * bn * 4 + 2 * META_ROWS * (kdim * 2 + 3 * D_MODEL * 4) + (4 << 20))
    outs = pl.pallas_call(
        functools.partial(_proj_residual_kernel, n_i=n_i, n_n=n_n, part=part, emit_h=emit_h, with_meta=with_meta),
        grid=(n_i + 1, n_n),
        in_specs=in_specs,
        out_specs=out_specs,
        out_shape=out_shape,
        scratch_shapes=scratch,
        compiler_params=_params(("arbitrary", "arbitrary"), est),
        name=name,
    )(*args)
    outs = list(outs)
    x_new = outs.pop(0)
    h_new = outs.pop(0) if emit_h else None
    xm_new = outs.pop(0) if with_meta else None
    hm_new = outs.pop(0) if (with_meta and emit_h) else None
    return x_new, h_new, xm_new, hm_new


def _gelu_tanh(x):
    return 0.5 * x * (1.0 + jnp.tanh(0.7978845608028654 * (x + 0.044715 * (x * x * x))))


def _ffn_in_tile(h_ref, wa_scr, wu_scr, cw_ref, halo, o_ref, a_scr, *, rows):
    h = h_ref[...]
    a = jnp.dot(h, wa_scr[...], preferred_element_type=F32)
    u = jnp.dot(h, wu_scr[...], preferred_element_type=F32)
    a_scr[0:SUBLANES, :] = jnp.zeros((SUBLANES, a.shape[1]), F32) if halo is None else halo
    a_scr[SUBLANES:SUBLANES + rows, :] = a
    conv = cw_ref[2:3, :] * a
    conv = conv + cw_ref[1:2, :] * a_scr[SUBLANES - 1:SUBLANES - 1 + rows, :]
    conv = conv + cw_ref[0:1, :] * a_scr[SUBLANES - 2:SUBLANES - 2 + rows, :]
    o_ref[...] = (_gelu_tanh(conv) * u).astype(BF16)
    return a


def _ffn_in_kernel(h_ref, hm_ref, wa_ref, wu_ref, cw_ref, o_ref, om_ref,
                   wa_scr, wu_scr, a_scr, tail_scr, mtail_scr, *, bm, tiles_per_batch):
    @pl.when(pl.program_id(1) == 0)
    def _():
        wa_scr[...] = wa_ref[...].astype(BF16)
        wu_scr[...] = wu_ref[...].astype(BF16)
        a_m = _ffn_in_tile(hm_ref, wa_scr, wu_scr, cw_ref, None, om_ref, a_scr, rows=META_ROWS)
        mtail_scr[...] = a_m[N_META - SUBLANES:N_META, :]

    first = pl.program_id(1) % tiles_per_batch == 0
    halo = jnp.where(first, mtail_scr[...], tail_scr[...])
    a = _ffn_in_tile(h_ref, wa_scr, wu_scr, cw_ref, halo, o_ref, a_scr, rows=bm)
    tail_scr[...] = a[bm - SUBLANES:bm, :]


def _ffn_in(h, hm, w_ffn_in, conv_w, layer, *, bm, bn, tiles_per_batch):
    m = h.shape[0]
    n_j = D_FF // bn
    est = (2 * (bm * D_MODEL * 2 + 2 * D_MODEL * bn * 4 + bm * bn * 2) + 2 * D_MODEL * bn * 2
           + 9 * bm * bn * 4 + (4 << 20))
    return pl.pallas_call(
        functools.partial(_ffn_in_kernel, bm=bm, tiles_per_batch=tiles_per_batch),
        grid=(n_j, m // bm),
        in_specs=[pl.BlockSpec((bm, D_MODEL), lambda j, i: (i, 0)),
                  pl.BlockSpec((META_ROWS, D_MODEL), lambda j, i: (0, 0)),
                  pl.BlockSpec((None, D_MODEL, bn), lambda j, i: (layer, 0, j)),
                  pl.BlockSpec((None, D_MODEL, bn), lambda j, i: (layer, 0, j + n_j)),
                  pl.BlockSpec((None, CONV_WIDTH, bn), lambda j, i: (layer, 0, j))],
        out_specs=[pl.BlockSpec((bm, bn), lambda j, i: (i, j)),
                   pl.BlockSpec((META_ROWS, bn), lambda j, i: (0, j))],
        out_shape=[jax.ShapeDtypeStruct((m, D_FF), BF16), jax.ShapeDtypeStruct((META_ROWS, D_FF), BF16)],
        scratch_shapes=[pltpu.VMEM((D_MODEL, bn), BF16), pltpu.VMEM((D_MODEL, bn), BF16),
                        pltpu.VMEM((SUBLANES + bm, bn), F32),
                        pltpu.VMEM((SUBLANES, bn), F32), pltpu.VMEM((SUBLANES, bn), F32)],
        compiler_params=_params(("arbitrary", "arbitrary"), est),
        name="ffn_in",
    )(h, hm, w_ffn_in, w_ffn_in, conv_w)


def kernel(x, meta_tokens, norm_pre_mix, norm_post_mix, norm_pre_ffn, norm_post_ffn, w_in, b_if,
           mlstm_head_gain, conv_mix_w, pool_w, pool_scale, w_branch, w_out, w_ffn_in, ffn_conv_w,
           w_ffn_out):
    batch, seq, _ = x.shape
    depth = w_in.shape[0]
    m = batch * seq
    xr = x.reshape(m, D_MODEL)
    xm = jnp.pad(meta_tokens.astype(F32), ((0, META_ROWS - N_META), (0, 0)))

    wo = w_out.astype(BF16)
    wfo = w_ffn_out.astype(BF16)
    w_branch, w_ffn_in, pool_w = (t.astype(F32) for t in (w_branch, w_ffn_in, pool_w))
    w_in_t = jnp.swapaxes(w_in.astype(F32), 1, 2)
    cw = conv_mix_w.astype(F32)
    fcw = ffn_conv_w.astype(F32)

    def vec(v):
        return v.reshape(1, -1).astype(F32)

    hr = _norm(xr, vec(norm_pre_mix[0]), 512)
    hm = _norm(xm, vec(norm_pre_mix[0]), META_ROWS)
    for l in range(depth):
        bias_row = jnp.pad(b_if[l].reshape(1, N_LOGITS).astype(F32), ((0, 0), (0, LANES - N_LOGITS)))
        gain = vec(mlstm_head_gain[l])
        ps = vec(pool_scale[l])
        g_post_mix, g_pre_ffn, g_post_ffn = vec(norm_post_mix[l]), vec(norm_pre_ffn[l]), vec(norm_post_ffn[l])
        last_layer = l + 1 == depth
        g_next = None if last_layer else vec(norm_pre_mix[l + 1])

        big, big_m, logits, logits_m = _in_proj(hr, hm, w_in_t, l, bm=1024, bn=1024)
        ya, ya_m = _mlstm(big, big_m, logits, logits_m, bias_row, gain, batches=batch, group=MLSTM_GROUP,
                          chunk=MLSTM_CHUNK)
        yb, yc, yb_m, yc_m = _conv_pool(big, big_m, cw, pool_w, ps, l, bm=512, tiles_per_batch=seq // 512)
        mg, mg_m = _branch((ya, yb, yc), (ya_m, yb_m, yc_m), w_branch, big, big_m, l, bm=1024, bn=512)
        xr, hf, xm, hf_m = _proj_residual(mg, mg_m, wo, l, xr, xm, g_post_mix, g_pre_ffn,
                                          bm=512, bn=D_MODEL, name="out_proj")
        act, act_m = _ffn_in(hf, hf_m, w_ffn_in, fcw, l, bm=1024, bn=512, tiles_per_batch=seq // 1024)
        xr, hr, xm, hm = _proj_residual(act, None if last_layer else act_m, wfo, l, xr, xm, g_post_ffn, g_next,
                                        bm=512, bn=512, name="ffn_out")
    return xr.reshape(batch, seq, D_MODEL)
```

```python
import functools
import math

import jax
import jax.numpy as jnp
import numpy as np
from jax import lax
from jax.experimental import pallas as pl
from jax.experimental.pallas import tpu as pltpu

D_MODEL = 2048
N_META = 16
MIX_WIDTH = 1024
N_BRANCH = 3
HEADS = 8
HEAD_DIM = MIX_WIDTH // HEADS
POOL_WINDOWS = (2, 4, 8, 16)
POOL_GROUP = MIX_WIDTH // len(POOL_WINDOWS)
D_FF = 5632
RMS_EPS = 1e-6
CONV_WIDTH = 3

COL_Q, COL_K, COL_V, COL_O, COL_CB, COL_CC, COL_CX, COL_PU = (n * MIX_WIDTH for n in range(8))
COL_GATES = 8 * MIX_WIDTH
BIG_COLS = COL_GATES + N_BRANCH * D_MODEL
N_LOGITS = 2 * HEADS
LANES = 128
HALO = 16
SUBLANES = 8
META_ROWS = 128
MLSTM_CHUNK = 512
MLSTM_GROUP = 1
VMEM_LIMIT_CAP = 56 * 1024 * 1024

F32 = jnp.float32
BF16 = jnp.bfloat16


def _params(semantics, vmem_bytes):
    return pltpu.CompilerParams(dimension_semantics=semantics,
                                vmem_limit_bytes=min(int(vmem_bytes), VMEM_LIMIT_CAP))


def _rms(x, g):
    return x * lax.rsqrt(jnp.mean(x * x, axis=-1, keepdims=True) + RMS_EPS) * g


def _norm_kernel(x_ref, g_ref, h_ref):
    h_ref[...] = _rms(x_ref[...], g_ref[...]).astype(BF16)


def _norm(x, g, bm):
    m = x.shape[0]
    return pl.pallas_call(
        _norm_kernel,
        grid=(m // bm,),
        in_specs=[pl.BlockSpec((bm, D_MODEL), lambda i: (i, 0)),
                  pl.BlockSpec((1, D_MODEL), lambda i: (0, 0))],
        out_specs=pl.BlockSpec((bm, D_MODEL), lambda i: (i, 0)),
        out_shape=jax.ShapeDtypeStruct((m, D_MODEL), BF16),
        compiler_params=_params(("parallel",), 10 * bm * D_MODEL * 4),
        name="pre_norm",
    )(x, g)


_NT = (((1,), (1,)), ((), ()))


def _in_proj_kernel(h_ref, hm_ref, wt_ref, wl_ref, o_ref, om_ref, lg_ref, lgm_ref, w_scr, wl_scr):
    first_tile = pl.program_id(1) == 0

    @pl.when(first_tile)
    def _():
        w_scr[...] = wt_ref[0].astype(BF16)
        om_ref[...] = lax.dot_general(hm_ref[...], w_scr[...], _NT, preferred_element_type=F32).astype(BF16)

    o_ref[...] = lax.dot_general(h_ref[...], w_scr[...], _NT, preferred_element_type=F32).astype(BF16)

    @pl.when(pl.program_id(0) == 0)
    def _():
        @pl.when(first_tile)
        def _():
            wl_scr[...] = wl_ref[0].astype(BF16)
            lgm_ref[...] = lax.dot_general(hm_ref[...], wl_scr[...], _NT, preferred_element_type=F32)

        lg_ref[...] = lax.dot_general(h_ref[...], wl_scr[...], _NT, preferred_element_type=F32)


def _in_proj(h, hm, w_in_t, layer, *, bm, bn):
    m, k = h.shape
    n_i = m // bm

    def w_rows(j, i):
        skip = jnp.where(j >= COL_CB // bn, N_LOGITS // SUBLANES, 0)
        return layer, (j * (bn // SUBLANES) + skip) * SUBLANES, 0

    est = (2 * (bm * k * 2 + k * bn * 4 + k * LANES * 4 + bm * bn * 2 + bm * LANES * 4) + k * bn * 2
           + 2 * bm * bn * 4 + (4 << 20))
    return pl.pallas_call(
        _in_proj_kernel,
        grid=(BIG_COLS // bn, n_i),
        in_specs=[pl.BlockSpec((bm, k), lambda j, i: (i, 0)),
                  pl.BlockSpec((META_ROWS, k), lambda j, i: (0, 0)),
                  pl.BlockSpec((pl.Element(1), pl.Element(bn), pl.Element(k)), w_rows),
                  pl.BlockSpec((pl.Element(1), pl.Element(LANES), pl.Element(k)), lambda j, i: (layer, COL_CB, 0))],
        out_specs=[pl.BlockSpec((bm, bn), lambda j, i: (i, j)),
                   pl.BlockSpec((META_ROWS, bn), lambda j, i: (0, j)),
                   pl.BlockSpec((bm, LANES), lambda j, i: (jnp.where(j == 0, i, n_i - 1), 0)),
                   pl.BlockSpec((META_ROWS, LANES), lambda j, i: (0, 0))],
        out_shape=[jax.ShapeDtypeStruct((m, BIG_COLS), BF16),
                   jax.ShapeDtypeStruct((META_ROWS, BIG_COLS), BF16),
                   jax.ShapeDtypeStruct((m, LANES), F32),
                   jax.ShapeDtypeStruct((META_ROWS, LANES), F32)],
        scratch_shapes=[pltpu.VMEM((bn, k), BF16), pltpu.VMEM((LANES, k), BF16)],
        compiler_params=_params(("arbitrary", "arbitrary"), est),
        name="in_proj",
    )(h, hm, w_in_t, w_in_t)


def _shift_rows(x, d):
    return pltpu.roll(x, d, axis=0)


def _cumsum_rows(x):
    rows = x.shape[0]
    row = lax.broadcasted_iota(jnp.int32, x.shape, 0)
    d = 1
    while d < rows:
        x = x + jnp.where(row >= d, _shift_rows(x, d), 0.0)
        d *= 2
    return x


def _cummax_rows(x):
    rows = x.shape[0]
    row = lax.broadcasted_iota(jnp.int32, x.shape, 0)
    d = 1
    while d < rows:
        x = jnp.maximum(x, jnp.where(row >= d, _shift_rows(x, d), -jnp.inf))
        d *= 2
    return x


def _log_sigmoid(x):
    return jnp.minimum(x, 0.0) - jnp.log1p(jnp.exp(-jnp.abs(x)))


def _mlstm_chunk(streams, bias_ref, gain_ref, *, chunk, n_valid):
    scale = HEAD_DIM ** -0.5
    last = n_valid - 1
    row1 = lax.broadcasted_iota(jnp.int32, (chunk, LANES), 0)
    gates = []
    for q_ref, k_ref, v_ref, o_ref, gl_ref, y_ref, c_scr, m_scr, s_scr, r_scr in streams:
        gl = gl_ref[...] + bias_ref[...]
        logf = pltpu.roll(_log_sigmoid(gl), LANES - HEADS, axis=1)
        b = _cumsum_rows(logf)
        a = gl - b
        m_prev = m_scr[...]
        g = jnp.maximum(_cummax_rows(a), m_prev)
        inter_w = jnp.exp(m_prev - g) * scale
        exp_neg_m = jnp.exp(-(b + g))
        g_last = g[last:last + 1, :]
        decay = jnp.exp(m_prev - g_last)
        ws = jnp.where(row1 <= last, jnp.exp(a - g_last), 0.0)
        m_scr[...] = b[last:last + 1, :] + g_last
        a_t = (a + math.log(scale)).T
        gates.append((g, a_t, inter_w, exp_neg_m, ws, decay))

    rows = lax.broadcasted_iota(jnp.int32, (chunk, chunk), 0)
    cols = lax.broadcasted_iota(jnp.int32, (chunk, chunk), 1)
    causal = rows >= cols
    ones = jnp.ones((chunk, HEAD_DIM), BF16)
    head = [slice(h * HEAD_DIM, (h + 1) * HEAD_DIM) for h in range(HEADS)]
    wide = [slice(h * 2 * HEAD_DIM, (h + 1) * 2 * HEAD_DIM) for h in range(HEADS)]
    work = [(h, st, gt) for h in range(HEADS) for st, gt in zip(streams, gates)]

    for h, (q_ref, k_ref, _, _, _, _, _, _, s_scr, _), (g, a_t, _, _, _, _) in work:
        w = jnp.where(causal, jnp.exp(a_t[h:h + 1, :] - g[:, h:h + 1]), 0.0)
        s = lax.dot_general(q_ref[:, head[h]], k_ref[:, head[h]], _NT, preferred_element_type=F32)
        s_scr[h, 0:chunk, 0:chunk] = (s * w).astype(BF16)
    for h, (q_ref, _, v_ref, _, _, _, c_scr, _, s_scr, r_scr), (_, _, inter_w, _, _, _) in work:
        v_aug = jnp.concatenate([v_ref[:, head[h]], ones], axis=1)
        r_scr[0:chunk, wide[h]] = (
            jnp.dot(q_ref[:, head[h]], c_scr[h].astype(BF16), preferred_element_type=F32)
            * inter_w[:, h:h + 1]
            + jnp.dot(s_scr[h, 0:chunk, 0:chunk], v_aug, preferred_element_type=F32))
    for h, (_, k_ref, v_ref, _, _, _, c_scr, _, _, _), (_, _, _, _, ws, decay) in work:
        v_aug = jnp.concatenate([v_ref[:, head[h]], ones], axis=1)
        kw = (k_ref[:, head[h]].astype(F32) * ws[:, h:h + 1]).astype(BF16)
        upd = lax.dot_general(kw, v_aug, (((0,), (0,)), ((), ())), preferred_element_type=F32)
        c_scr[h] = decay[:, h:h + 1] * c_scr[h] + upd
    for h, (_, _, _, o_ref, _, y_ref, _, _, _, r_scr), (_, _, _, exp_neg_m, _, _) in work:
        r = r_scr[0:chunk, wide[h]]
        num, den = r[:, :HEAD_DIM], r[:, HEAD_DIM:]
        hh = num / jnp.maximum(jnp.abs(den), exp_neg_m[:, h:h + 1])
        hn = hh * lax.rsqrt(jnp.mean(hh * hh, axis=-1, keepdims=True) + RMS_EPS)
        y_ref[:, head[h]] = (hn * gain_ref[:, head[h]]
                             * jax.nn.sigmoid(o_ref[:, head[h]].astype(F32))).astype(BF16)


def _mlstm_kernel(q_ref, k_ref, v_ref, o_ref, gl_ref, qm_ref, km_ref, vm_ref, om_ref, glm_ref,
                  bias_ref, gain_ref, y_ref, ym_ref, c_scr, m_scr, c0_scr, m0_scr, s_scr, r_scr, *, group, chunk):
    first_chunk = pl.program_id(1) == 0

    @pl.when(jnp.logical_and(pl.program_id(0) == 0, first_chunk))
    def _():
        c0_scr[...] = jnp.zeros_like(c0_scr)
        m0_scr[...] = jnp.zeros_like(m0_scr)
        meta = (qm_ref, km_ref, vm_ref, om_ref, glm_ref, ym_ref, c0_scr, m0_scr, s_scr.at[0], r_scr.at[0])
        _mlstm_chunk([meta], bias_ref, gain_ref, chunk=META_ROWS, n_valid=N_META)

    @pl.when(first_chunk)
    def _():
        for bb in range(group):
            c_scr[bb] = c0_scr[...]
            m_scr[bb] = m0_scr[...]

    streams = [(q_ref.at[bb], k_ref.at[bb], v_ref.at[bb], o_ref.at[bb], gl_ref.at[bb], y_ref.at[bb],
                c_scr.at[bb], m_scr.at[bb], s_scr.at[bb], r_scr.at[bb]) for bb in range(group)]
    _mlstm_chunk(streams, bias_ref, gain_ref, chunk=chunk, n_valid=chunk)


def _mlstm(big, big_m, logits, logits_m, bias_row, gain, *, batches, group, chunk):
    m = big.shape[0]
    seq = m // batches
    big3 = big.reshape(batches, seq, BIG_COLS)
    logits3 = logits.reshape(batches, seq, LANES)
    cblk = [COL_Q // MIX_WIDTH, COL_K // MIX_WIDTH, COL_V // MIX_WIDTH, COL_O // MIX_WIDTH]
    main = [pl.BlockSpec((group, chunk, MIX_WIDTH), functools.partial(lambda b, c, cb: (b, c, cb), cb=cb))
            for cb in cblk]
    meta = [pl.BlockSpec((META_ROWS, MIX_WIDTH), functools.partial(lambda b, c, cb: (0, cb), cb=cb))
            for cb in cblk]
    in_specs = (main + [pl.BlockSpec((group, chunk, LANES), lambda b, c: (b, c, 0))]
                + meta + [pl.BlockSpec((META_ROWS, LANES), lambda b, c: (0, 0)),
                          pl.BlockSpec((1, LANES), lambda b, c: (0, 0)),
                          pl.BlockSpec((1, MIX_WIDTH), lambda b, c: (0, 0))])
    c_shape = (HEADS, HEAD_DIM, 2 * HEAD_DIM)
    est = (group * (12 * chunk * MIX_WIDTH * 2 + HEADS * chunk * chunk * 2 + chunk * HEADS * 2 * HEAD_DIM * 4
                    + HEADS * HEAD_DIM * 2 * HEAD_DIM * 4 + 24 * chunk * LANES * 4) + (16 << 20))
    y, y_m = pl.pallas_call(
        functools.partial(_mlstm_kernel, group=group, chunk=chunk),
        grid=(batches // group, seq // chunk),
        in_specs=in_specs,
        out_specs=[pl.BlockSpec((group, chunk, MIX_WIDTH), lambda b, c: (b, c, 0)),
                   pl.BlockSpec((META_ROWS, MIX_WIDTH), lambda b, c: (0, 0))],
        out_shape=[jax.ShapeDtypeStruct((batches, seq, MIX_WIDTH), BF16),
                   jax.ShapeDtypeStruct((META_ROWS, MIX_WIDTH), BF16)],
        scratch_shapes=[pltpu.VMEM((group,) + c_shape, F32), pltpu.VMEM((group, 1, LANES), F32),
                        pltpu.VMEM(c_shape, F32), pltpu.VMEM((1, LANES), F32),
                        pltpu.VMEM((group, HEADS, chunk, chunk), BF16),
                        pltpu.VMEM((group, chunk, HEADS * 2 * HEAD_DIM), F32)],
        compiler_params=_params(("arbitrary", "arbitrary"), est),
        name="mlstm",
    )(big3, big3, big3, big3, logits3, big_m, big_m, big_m, big_m, logits_m, bias_row, gain)
    return y.reshape(m, MIX_WIDTH), y_m


POOL_BLOCK = 128
CONV_ROWS, CONV_COLS = 128, 256


def _pool_bands():
    bands = np.zeros((len(POOL_WINDOWS), POOL_BLOCK, 2 * POOL_BLOCK), np.float32)
    for grp, win in enumerate(POOL_WINDOWS):
        for r in range(POOL_BLOCK):
            for lag in range(win):
                bands[grp, r, POOL_BLOCK + r - lag] += 1.0 / win
            bands[grp, r, POOL_BLOCK + r] -= 1.0
    return jnp.asarray(bands, BF16)


def _conv3(cb_ref, cc_ref, cx_ref, halo, cw_ref, yb_ref, p_scr, *, rows):
    p_scr[0:HALO, :] = halo
    pieces = [(r, c) for r in range(0, rows, CONV_ROWS) for c in range(0, MIX_WIDTH, CONV_COLS)]
    for r, c in pieces:
        rs, cs = slice(r, r + CONV_ROWS), slice(c, c + CONV_COLS)
        p_scr[HALO + r:HALO + r + CONV_ROWS, cs] = cc_ref[rs, cs].astype(F32) * cx_ref[rs, cs].astype(F32)
    for r, c in pieces:
        rs, cs = slice(r, r + CONV_ROWS), slice(c, c + CONV_COLS)
        conv = cw_ref[2:3, cs] * p_scr[HALO + r:HALO + r + CONV_ROWS, cs]
        conv = conv + cw_ref[1:2, cs] * p_scr[HALO + r - 1:HALO + r - 1 + CONV_ROWS, cs]
        conv = conv + cw_ref[0:1, cs] * p_scr[HALO + r - 2:HALO + r - 2 + CONV_ROWS, cs]
        yb_ref[rs, cs] = (cb_ref[rs, cs].astype(F32) * conv).astype(BF16)


def _pool_from_start(pu_ref, pw_scr, ps_ref, yc_ref, u_scr, *, rows):
    u_scr[0:HALO, :] = jnp.zeros((HALO, MIX_WIDTH), F32)
    u_scr[HALO:HALO + rows, :] = pu_ref[...].astype(F32)
    t = lax.broadcasted_iota(jnp.int32, (rows, 1), 0)
    for grp, win in enumerate(POOL_WINDOWS):
        sl = slice(grp * POOL_GROUP, (grp + 1) * POOL_GROUP)
        cur = u_scr[HALO:HALO + rows, sl]
        tot = cur
        for lag in range(1, win):
            tot = tot + u_scr[HALO - lag:HALO - lag + rows, sl]
        pooled = tot / jnp.minimum(t + 1, win).astype(F32) - cur
        y = jnp.dot(pooled.astype(BF16), pw_scr[grp], preferred_element_type=F32)
        yc_ref[:, sl] = (y * ps_ref[:, sl]).astype(BF16)


def _pool_banded(band_ref, pw_scr, ps_ref, yc_ref, ub_scr, *, rows):
    for grp in range(len(POOL_WINDOWS)):
        sl = slice(grp * POOL_GROUP, (grp + 1) * POOL_GROUP)
        pooled = jnp.concatenate(
            [jnp.dot(band_ref[grp], ub_scr[r:r + 2 * POOL_BLOCK, sl], preferred_element_type=F32)
             for r in range(0, rows, POOL_BLOCK)], axis=0)
        y = jnp.dot(pooled.astype(BF16), pw_scr[grp], preferred_element_type=F32)
        yc_ref[:, sl] = (y * ps_ref[:, sl]).astype(BF16)


def _conv_pool_kernel(cb_ref, cc_ref, cx_ref, pu_ref, ccp_ref, cxp_ref, pup_ref,
                      cbm_ref, ccm_ref, cxm_ref, pum_ref, cw_ref, pw_ref, ps_ref, band_ref,
                      yb_ref, yc_ref, ybm_ref, ycm_ref, pw_scr, p_scr, u_scr, ub_scr, *, bm, tiles_per_batch):
    @pl.when(pl.program_id(0) == 0)
    def _():
        pw_scr[...] = pw_ref[...].astype(BF16)
        _conv3(cbm_ref, ccm_ref, cxm_ref, jnp.zeros((HALO, MIX_WIDTH), F32), cw_ref, ybm_ref, p_scr,
               rows=META_ROWS)
        _pool_from_start(pum_ref, pw_scr, ps_ref, ycm_ref, u_scr, rows=META_ROWS)

    first = pl.program_id(0) % tiles_per_batch == 0
    cch = jnp.where(first, ccm_ref[0:HALO, :], ccp_ref[...]).astype(F32)
    cxh = jnp.where(first, cxm_ref[0:HALO, :], cxp_ref[...]).astype(F32)
    _conv3(cb_ref, cc_ref, cx_ref, cch * cxh, cw_ref, yb_ref, p_scr, rows=bm)
    meta_tail = jnp.concatenate([jnp.zeros((POOL_BLOCK - N_META, MIX_WIDTH), BF16), pum_ref[0:N_META, :]], axis=0)
    ub_scr[0:POOL_BLOCK, :] = jnp.where(first, meta_tail, pup_ref[...])
    ub_scr[POOL_BLOCK:POOL_BLOCK + bm, :] = pu_ref[...]
    _pool_banded(band_ref, pw_scr, ps_ref, yc_ref, ub_scr, rows=bm)


def _conv_pool(big, big_m, conv_w, pool_w, pool_scale, layer, *, bm, tiles_per_batch):
    m = big.shape[0]
    cblk = [COL_CB // MIX_WIDTH, COL_CC // MIX_WIDTH, COL_CX // MIX_WIDTH, COL_PU // MIX_WIDTH]

    def prev_rows(rows, c):
        per = bm // rows
        return pl.BlockSpec((rows, MIX_WIDTH), lambda i: (jnp.maximum(i * per - 1, 0), c))

    in_specs = [pl.BlockSpec((bm, MIX_WIDTH), functools.partial(lambda i, c: (i, c), c=c)) for c in cblk]
    in_specs += [prev_rows(HALO, cblk[1]), prev_rows(HALO, cblk[2]), prev_rows(POOL_BLOCK, cblk[3])]
    in_specs += [pl.BlockSpec((META_ROWS, MIX_WIDTH), functools.partial(lambda i, c: (0, c), c=c))
                 for c in cblk]
    in_specs += [pl.BlockSpec((None, CONV_WIDTH, MIX_WIDTH), lambda i: (layer, 0, 0)),
                 pl.BlockSpec((None, len(POOL_WINDOWS), POOL_GROUP, POOL_GROUP), lambda i: (layer, 0, 0, 0)),
                 pl.BlockSpec((1, MIX_WIDTH), lambda i: (0, 0)),
                 pl.BlockSpec((len(POOL_WINDOWS), POOL_BLOCK, 2 * POOL_BLOCK), lambda i: (0, 0, 0))]
    out_spec = pl.BlockSpec((bm, MIX_WIDTH), lambda i: (i, 0))
    meta_spec = pl.BlockSpec((META_ROWS, MIX_WIDTH), lambda i: (0, 0))
    return pl.pallas_call(
        functools.partial(_conv_pool_kernel, bm=bm, tiles_per_batch=tiles_per_batch),
        grid=(m // bm,),
        in_specs=in_specs,
        out_specs=[out_spec, out_spec, meta_spec, meta_spec],
        out_shape=[jax.ShapeDtypeStruct((m, MIX_WIDTH), BF16)] * 2
        + [jax.ShapeDtypeStruct((META_ROWS, MIX_WIDTH), BF16)] * 2,
        scratch_shapes=[pltpu.VMEM((len(POOL_WINDOWS), POOL_GROUP, POOL_GROUP), BF16),
                        pltpu.VMEM((HALO + max(bm, META_ROWS), MIX_WIDTH), F32),
                        pltpu.VMEM((HALO + META_ROWS, MIX_WIDTH), F32),
                        pltpu.VMEM((POOL_BLOCK + bm, MIX_WIDTH), BF16)],
        compiler_params=_params(("arbitrary",), 14 * bm * MIX_WIDTH * 4 + (10 << 20)),
        name="conv_pool",
    )(big, big, big, big, big, big, big, big_m, big_m, big_m, big_m, conv_w, pool_w, pool_scale, _pool_bands())


def _branch_tile(ya_ref, yb_ref, yc_ref, w_scr, g0_ref, g1_ref, g2_ref, o_ref):
    acc = None
    for y_ref, g_ref, n in ((ya_ref, g0_ref, 0), (yb_ref, g1_ref, 1), (yc_ref, g2_ref, 2)):
        br = jnp.dot(y_ref[...], w_scr[n], preferred_element_type=F32)
        term = jax.nn.sigmoid(g_ref[...].astype(F32)) * br
        acc = term if acc is None else acc + term
    o_ref[...] = acc.astype(BF16)


def _branch_kernel(ya_ref, yb_ref, yc_ref, g0_ref, g1_ref, g2_ref,
                   yam_ref, ybm_ref, ycm_ref, g0m_ref, g1m_ref, g2m_ref, w_ref, o_ref, om_ref, w_scr):
    @pl.when(pl.program_id(1) == 0)
    def _():
        w_scr[...] = w_ref[...].astype(BF16)
        _branch_tile(yam_ref, ybm_ref, ycm_ref, w_scr, g0m_ref, g1m_ref, g2m_ref, om_ref)

    _branch_tile(ya_ref, yb_ref, yc_ref, w_scr, g0_ref, g1_ref, g2_ref, o_ref)


def _branch(ys, ys_m, w_branch, big, big_m, layer, *, bm, bn):
    m = big.shape[0]
    offs = [(COL_GATES + n * D_MODEL) // bn for n in range(N_BRANCH)]
    y_spec = pl.BlockSpec((bm, MIX_WIDTH), lambda j, i: (i, 0))
    ym_spec = pl.BlockSpec((META_ROWS, MIX_WIDTH), lambda j, i: (0, 0))
    g_specs = [pl.BlockSpec((bm, bn), functools.partial(lambda j, i, off: (i, off + j), off=off)) for off in offs]
    gm_specs = [pl.BlockSpec((META_ROWS, bn), functools.partial(lambda j, i, off: (0, off + j), off=off))
                for off in offs]
    est = (2 * (3 * bm * MIX_WIDTH * 2 + 3 * MIX_WIDTH * bn * 4 + 4 * bm * bn * 2) + 3 * MIX_WIDTH * bn * 2
           + 6 * bm * bn * 4 + (4 << 20))
    return pl.pallas_call(
        _branch_kernel,
        grid=(D_MODEL // bn, m // bm),
        in_specs=[y_spec] * 3 + g_specs + [ym_spec] * 3 + gm_specs
        + [pl.BlockSpec((None, N_BRANCH, MIX_WIDTH, bn), lambda j, i: (layer, 0, 0, j))],
        out_specs=[pl.BlockSpec((bm, bn), lambda j, i: (i, j)),
                   pl.BlockSpec((META_ROWS, bn), lambda j, i: (0, j))],
        out_shape=[jax.ShapeDtypeStruct((m, D_MODEL), BF16), jax.ShapeDtypeStruct((META_ROWS, D_MODEL), BF16)],
        scratch_shapes=[pltpu.VMEM((N_BRANCH, MIX_WIDTH, bn), BF16)],
        compiler_params=_params(("arbitrary", "arbitrary"), est),
        name="branch_merge",
    )(*ys, big, big, big, *ys_m, big_m, big_m, big_m, w_branch)


FINISH_ROWS = 64


def _finish_rows(y_scr, slot, x_ref, gpost_ref, gnext_ref, xo_ref, ho_ref, *, row0, rows):
    n_n = y_scr.shape[1]
    for s in range(0, rows, FINISH_ROWS):
        if isinstance(row0, int):
            ys = slice(row0 + s, row0 + s + FINISH_ROWS)
        else:
            ys = pl.ds(pl.multiple_of(row0 + s, FINISH_ROWS), FINISH_ROWS)
        y = jnp.concatenate([y_scr[slot, c, ys, :] for c in range(n_n)], axis=1)
        xn = x_ref[s:s + FINISH_ROWS, :] + _rms(y, gpost_ref[...])
        xo_ref[s:s + FINISH_ROWS, :] = xn
        if ho_ref is not None:
            ho_ref[s:s + FINISH_ROWS, :] = _rms(xn, gnext_ref[...]).astype(BF16)


def _proj_residual_kernel(*refs, n_i, n_n, part, emit_h, with_meta):
    a_ref, w_ref, x_ref, gpost_ref = refs[:4]
    refs = refs[4:]
    gnext_ref = None
    if emit_h:
        gnext_ref = refs[0]
        refs = refs[1:]
    if with_meta:
        am_ref, xm_ref = refs[:2]
        refs = refs[2:]
    xo_ref = refs[0]
    refs = refs[1:]
    ho_ref = None
    if emit_h:
        ho_ref = refs[0]
        refs = refs[1:]
    if with_meta:
        xmo_ref = refs[0]
        refs = refs[1:]
        hmo_ref = None
        if emit_h:
            hmo_ref = refs[0]
            refs = refs[1:]
        y_scr, ym_scr = refs
    else:
        y_scr, = refs
    i = pl.program_id(0)
    n = pl.program_id(1) if n_n > 1 else 0

    def step(slot, do_matmul, do_finish):
        if do_finish:
            _finish_rows(y_scr, 1 - slot, x_ref, gpost_ref, gnext_ref, xo_ref, ho_ref, row0=n * part, rows=part)
        if do_matmul:
            y_scr[slot, n] = jnp.dot(a_ref[...], w_ref[...], preferred_element_type=F32)

    @pl.when(i == 0)
    def _():
        if with_meta:
            ym_scr[0, n] = jnp.dot(am_ref[...], w_ref[...], preferred_element_type=F32)

            def finish_meta():
                _finish_rows(ym_scr, 0, xm_ref, gpost_ref, gnext_ref, xmo_ref, hmo_ref, row0=0, rows=META_ROWS)
            if n_n > 1:
                pl.when(n == n_n - 1)(finish_meta)
            else:
                finish_meta()
        step(0, True, False)

    for parity in (0, 1):
        @pl.when(jnp.logical_and(jnp.logical_and(i > 0, i < n_i), lax.rem(i, 2) == parity))
        def _():
            step(parity, True, True)

    @pl.when(i == n_i)
    def _():
        step(n_i % 2, False, True)


def _proj_residual(a, a_m, w, layer, x, x_m, g_post, g_next, *, bm, bn, name):
    m, kdim = a.shape
    n_i = m // bm
    n_n = D_MODEL // bn
    part = bm // n_n
    emit_h = g_next is not None
    with_meta = a_m is not None
    const = lambda i, n: (0, 0)
    prev_part = lambda i, n: (jnp.where(i == 0, 0, (i - 1) * n_n + n), 0)
    vec = pl.BlockSpec((1, D_MODEL), const)
    w_mode = dict(pipeline_mode=pl.Buffered(1)) if n_n == 1 else {}
    in_specs = [pl.BlockSpec((bm, kdim), lambda i, n: (jnp.minimum(i, n_i - 1), 0)),
                pl.BlockSpec((None, kdim, bn), lambda i, n: (layer, 0, n), **w_mode),
                pl.BlockSpec((part, D_MODEL), prev_part), vec]
    args = [a, w, x, g_post]
    if emit_h:
        in_specs.append(vec)
        args.append(g_next)
    if with_meta:
        in_specs += [pl.BlockSpec((META_ROWS, kdim), const), pl.BlockSpec((META_ROWS, D_MODEL), const)]
        args += [a_m, x_m]
    out_specs = [pl.BlockSpec((part, D_MODEL), prev_part)]
    out_shape = [jax.ShapeDtypeStruct((m, D_MODEL), F32)]
    if emit_h:
        out_specs.append(pl.BlockSpec((part, D_MODEL), prev_part))
        out_shape.append(jax.ShapeDtypeStruct((m, D_MODEL), BF16))
    scratch = [pltpu.VMEM((2, n_n, bm, bn), F32)]
    if with_meta:
        out_specs.append(pl.BlockSpec((META_ROWS, D_MODEL), const))
        out_shape.append(jax.ShapeDtypeStruct((META_ROWS, D_MODEL), F32))
        if emit_h:
            out_specs.append(pl.BlockSpec((META_ROWS, D_MODEL), const))
            out_shape.append(jax.ShapeDtypeStruct((META_ROWS, D_MODEL), BF16))
        scratch.append(pltpu.VMEM((1, n_n, META_ROWS, bn), F32))
    w_bufs = 1 if n_n == 1 else 2
    est = (2 * (bm * kdim * 2 + part * D_MODEL * (4 + 4 + 2)) + w_bufs * kdim * bn * 2 + 2 * bm * D_MODEL * 4
           + 2 * bm * bn * 4 + 2 * META_ROWS * (kdim * 2 + 3 * D_MODEL * 4) + (4 << 20))
    outs = pl.pallas_call(
        functools.partial(_proj_residual_kernel, n_i=n_i, n_n=n_n, part=part, emit_h=emit_h, with_meta=with_meta),
        grid=(n_i + 1, n_n),
        in_specs=in_specs,
        out_specs=out_specs,
        out_shape=out_shape,
        scratch_shapes=scratch,
        compiler_params=_params(("arbitrary", "arbitrary"), est),
        name=name,
    )(*args)
    outs = list(outs)
    x_new = outs.pop(0)
    h_new = outs.pop(0) if emit_h else None
    xm_new = outs.pop(0) if with_meta else None
    hm_new = outs.pop(0) if (with_meta and emit_h) else None
    return x_new, h_new, xm_new, hm_new


def _gelu_tanh(x):
    return 0.5 * x * (1.0 + jnp.tanh(0.7978845608028654 * (x + 0.044715 * (x * x * x))))


def _ffn_in_tile(h_ref, wa_scr, wu_scr, cw_ref, halo, o_ref, a_scr, *, rows):
    h = h_ref[...]
    a = jnp.dot(h, wa_scr[...], preferred_element_type=F32)
    u = jnp.dot(h, wu_scr[...], preferred_element_type=F32)
    a_scr[0:SUBLANES, :] = jnp.zeros((SUBLANES, a.shape[1]), F32) if halo is None else halo
    a_scr[SUBLANES:SUBLANES + rows, :] = a
    conv = cw_ref[2:3, :] * a
    conv = conv + cw_ref[1:2, :] * a_scr[SUBLANES - 1:SUBLANES - 1 + rows, :]
    conv = conv + cw_ref[0:1, :] * a_scr[SUBLANES - 2:SUBLANES - 2 + rows, :]
    o_ref[...] = (_gelu_tanh(conv) * u).astype(BF16)
    return a


def _ffn_in_kernel(h_ref, hm_ref, wa_ref, wu_ref, cw_ref, o_ref, om_ref,
                   wa_scr, wu_scr, a_scr, tail_scr, mtail_scr, *, bm, tiles_per_batch):
    @pl.when(pl.program_id(1) == 0)
    def _():
        wa_scr[...] = wa_ref[...].astype(BF16)
        wu_scr[...] = wu_ref[...].astype(BF16)
        a_m = _ffn_in_tile(hm_ref, wa_scr, wu_scr, cw_ref, None, om_ref, a_scr, rows=META_ROWS)
        mtail_scr[...] = a_m[N_META - SUBLANES:N_META, :]

    first = pl.program_id(1) % tiles_per_batch == 0
    halo = jnp.where(first, mtail_scr[...], tail_scr[...])
    a = _ffn_in_tile(h_ref, wa_scr, wu_scr, cw_ref, halo, o_ref, a_scr, rows=bm)
    tail_scr[...] = a[bm - SUBLANES:bm, :]


def _ffn_in(h, hm, w_ffn_in, conv_w, layer, *, bm, bn, tiles_per_batch):
    m = h.shape[0]
    n_j = D_FF // bn
    est = (2 * (bm * D_MODEL * 2 + 2 * D_MODEL * bn * 4 + bm * bn * 2) + 2 * D_MODEL * bn * 2
           + 9 * bm * bn * 4 + (4 << 20))
    return pl.pallas_call(
        functools.partial(_ffn_in_kernel, bm=bm, tiles_per_batch=tiles_per_batch),
        grid=(n_j, m // bm),
        in_specs=[pl.BlockSpec((bm, D_MODEL), lambda j, i: (i, 0)),
                  pl.BlockSpec((META_ROWS, D_MODEL), lambda j, i: (0, 0)),
                  pl.BlockSpec((None, D_MODEL, bn), lambda j, i: (layer, 0, j)),
                  pl.BlockSpec((None, D_MODEL, bn), lambda j, i: (layer, 0, j + n_j)),
                  pl.BlockSpec((None, CONV_WIDTH, bn), lambda j, i: (layer, 0, j))],
        out_specs=[pl.BlockSpec((bm, bn), lambda j, i: (i, j)),
                   pl.BlockSpec((META_ROWS, bn), lambda j, i: (0, j))],
        out_shape=[jax.ShapeDtypeStruct((m, D_FF), BF16), jax.ShapeDtypeStruct((META_ROWS, D_FF), BF16)],
        scratch_shapes=[pltpu.VMEM((D_MODEL, bn), BF16), pltpu.VMEM((D_MODEL, bn), BF16),
                        pltpu.VMEM((SUBLANES + bm, bn), F32),
                        pltpu.VMEM((SUBLANES, bn), F32), pltpu.VMEM((SUBLANES, bn), F32)],
        compiler_params=_params(("arbitrary", "arbitrary"), est),
        name="ffn_in",
    )(h, hm, w_ffn_in, w_ffn_in, conv_w)


def kernel(x, meta_tokens, norm_pre_mix, norm_post_mix, norm_pre_ffn, norm_post_ffn, w_in, b_if,
           mlstm_head_gain, conv_mix_w, pool_w, pool_scale, w_branch, w_out, w_ffn_in, ffn_conv_w,
           w_ffn_out):
    batch, seq, _ = x.shape
    depth = w_in.shape[0]
    m = batch * seq
    xr = x.reshape(m, D_MODEL)
    xm = jnp.pad(meta_tokens.astype(F32), ((0, META_ROWS - N_META), (0, 0)))

    wo = w_out.astype(BF16)
    wfo = w_ffn_out.astype(BF16)
    w_branch, w_ffn_in, pool_w = (t.astype(F32) for t in (w_branch, w_ffn_in, pool_w))
    w_in_t = jnp.swapaxes(w_in.astype(F32), 1, 2)
    cw = conv_mix_w.astype(F32)
    fcw = ffn_conv_w.astype(F32)

    def vec(v):
        return v.reshape(1, -1).astype(F32)

    hr = _norm(xr, vec(norm_pre_mix[0]), 512)
    hm = _norm(xm, vec(norm_pre_mix[0]), META_ROWS)
    for l in range(depth):
        bias_row = jnp.pad(b_if[l].reshape(1, N_LOGITS).astype(F32), ((0, 0), (0, LANES - N_LOGITS)))
        gain = vec(mlstm_head_gain[l])
        ps = vec(pool_scale[l])
        g_post_mix, g_pre_ffn, g_post_ffn = vec(norm_post_mix[l]), vec(norm_pre_ffn[l]), vec(norm_post_ffn[l])
        last_layer = l + 1 == depth
        g_next = None if last_layer else vec(norm_pre_mix[l + 1])

        big, big_m, logits, logits_m = _in_proj(hr, hm, w_in_t, l, bm=1024, bn=1024)
        ya, ya_m = _mlstm(big, big_m, logits, logits_m, bias_row, gain, batches=batch, group=MLSTM_GROUP,
                          chunk=MLSTM_CHUNK)
        yb, yc, yb_m, yc_m = _conv_pool(big, big_m, cw, pool_w, ps, l, bm=512, tiles_per_batch=seq // 512)
        mg, mg_m = _branch((ya, yb, yc), (ya_m, yb_m, yc_m), w_branch, big, big_m, l, bm=1024, bn=512)
        xr, hf, xm, hf_m = _proj_residual(mg, mg_m, wo, l, xr, xm, g_post_mix, g_pre_ffn,
                                          bm=512, bn=D_MODEL, name="out_proj")
        act, act_m = _ffn_in(hf, hf_m, w_ffn_in, fcw, l, bm=1024, bn=512, tiles_per_batch=seq // 1024)
        xr, hr, xm, hm = _proj_residual(act, None if last_layer else act_m, wfo, l, xr, xm, g_post_ffn, g_next,
                                        bm=256, bn=D_MODEL, name="ffn_out")
    return xr.reshape(batch, seq, D_MODEL)
```

```python
import functools
import math

import jax
import jax.numpy as jnp
import numpy as np
from jax import lax
from jax.experimental import pallas as pl
from jax.experimental.pallas import tpu as pltpu

D_MODEL = 2048
N_META = 16
MIX_WIDTH = 1024
N_BRANCH = 3
HEADS = 8
HEAD_DIM = MIX_WIDTH // HEADS
POOL_WINDOWS = (2, 4, 8, 16)
POOL_GROUP = MIX_WIDTH // len(POOL_WINDOWS)
D_FF = 5632
RMS_EPS = 1e-6
CONV_WIDTH = 3

COL_Q, COL_K, COL_V, COL_O, COL_CB, COL_CC, COL_CX, COL_PU = (n * MIX_WIDTH for n in range(8))
COL_GATES = 8 * MIX_WIDTH
BIG_COLS = COL_GATES + N_BRANCH * D_MODEL
N_LOGITS = 2 * HEADS
LANES = 128
HALO = 16
SUBLANES = 8
META_ROWS = 128
MLSTM_CHUNK = 512
MLSTM_GROUP = 1
VMEM_LIMIT_CAP = 60 * 1024 * 1024
MATMUL_ROWS = 1024

F32 = jnp.float32
BF16 = jnp.bfloat16


def _params(semantics, vmem_bytes):
    return pltpu.CompilerParams(dimension_semantics=semantics,
                                vmem_limit_bytes=min(int(vmem_bytes), VMEM_LIMIT_CAP))


def _rms(x, g):
    return x * lax.rsqrt(jnp.mean(x * x, axis=-1, keepdims=True) + RMS_EPS) * g


def _norm_kernel(x_ref, g_ref, h_ref):
    h_ref[...] = _rms(x_ref[...], g_ref[...]).astype(BF16)


def _norm(x, g, bm):
    m = x.shape[0]
    return pl.pallas_call(
        _norm_kernel,
        grid=(m // bm,),
        in_specs=[pl.BlockSpec((bm, D_MODEL), lambda i: (i, 0)),
                  pl.BlockSpec((1, D_MODEL), lambda i: (0, 0))],
        out_specs=pl.BlockSpec((bm, D_MODEL), lambda i: (i, 0)),
        out_shape=jax.ShapeDtypeStruct((m, D_MODEL), BF16),
        compiler_params=_params(("parallel",), 10 * bm * D_MODEL * 4),
        name="pre_norm",
    )(x, g)


_NT = (((1,), (1,)), ((), ()))


def _in_proj_kernel(h_ref, hm_ref, wt_ref, wl_ref, o_ref, om_ref, lg_ref, lgm_ref, w_scr, wl_scr):
    first_tile = pl.program_id(1) == 0

    @pl.when(first_tile)
    def _():
        w_scr[...] = wt_ref[0].astype(BF16)
        om_ref[...] = lax.dot_general(hm_ref[...], w_scr[...], _NT, preferred_element_type=F32).astype(BF16)

    for r in range(0, o_ref.shape[0], MATMUL_ROWS):
        rs = slice(r, r + MATMUL_ROWS)
        o_ref[rs, :] = lax.dot_general(h_ref[rs, :], w_scr[...], _NT, preferred_element_type=F32).astype(BF16)

    @pl.when(pl.program_id(0) == 0)
    def _():
        @pl.when(first_tile)
        def _():
            wl_scr[...] = wl_ref[0].astype(BF16)
            lgm_ref[...] = lax.dot_general(hm_ref[...], wl_scr[...], _NT, preferred_element_type=F32)

        lg_ref[...] = lax.dot_general(h_ref[...], wl_scr[...], _NT, preferred_element_type=F32)


def _in_proj(h, hm, w_in_t, layer, *, bm, bn):
    m, k = h.shape
    n_i = m // bm

    def w_rows(j, i):
        skip = jnp.where(j >= COL_CB // bn, N_LOGITS // SUBLANES, 0)
        return layer, (j * (bn // SUBLANES) + skip) * SUBLANES, 0

    est = (2 * (bm * k * 2 + k * bn * 4 + k * LANES * 4 + bm * bn * 2 + bm * LANES * 4) + k * bn * 2
           + 2 * bm * bn * 4 + (4 << 20))
    return pl.pallas_call(
        _in_proj_kernel,
        grid=(BIG_COLS // bn, n_i),
        in_specs=[pl.BlockSpec((bm, k), lambda j, i: (i, 0)),
                  pl.BlockSpec((META_ROWS, k), lambda j, i: (0, 0)),
                  pl.BlockSpec((pl.Element(1), pl.Element(bn), pl.Element(k)), w_rows),
                  pl.BlockSpec((pl.Element(1), pl.Element(LANES), pl.Element(k)), lambda j, i: (layer, COL_CB, 0))],
        out_specs=[pl.BlockSpec((bm, bn), lambda j, i: (i, j)),
                   pl.BlockSpec((META_ROWS, bn), lambda j, i: (0, j)),
                   pl.BlockSpec((bm, LANES), lambda j, i: (jnp.where(j == 0, i, n_i - 1), 0)),
                   pl.BlockSpec((META_ROWS, LANES), lambda j, i: (0, 0))],
        out_shape=[jax.ShapeDtypeStruct((m, BIG_COLS), BF16),
                   jax.ShapeDtypeStruct((META_ROWS, BIG_COLS), BF16),
                   jax.ShapeDtypeStruct((m, LANES), F32),
                   jax.ShapeDtypeStruct((META_ROWS, LANES), F32)],
        scratch_shapes=[pltpu.VMEM((bn, k), BF16), pltpu.VMEM((LANES, k), BF16)],
        compiler_params=_params(("arbitrary", "arbitrary"), est),
        name="in_proj",
    )(h, hm, w_in_t, w_in_t)


def _shift_rows(x, d):
    return pltpu.roll(x, d, axis=0)


def _cumsum_rows(x):
    rows = x.shape[0]
    row = lax.broadcasted_iota(jnp.int32, x.shape, 0)
    d = 1
    while d < rows:
        x = x + jnp.where(row >= d, _shift_rows(x, d), 0.0)
        d *= 2
    return x


def _cummax_rows(x):
    rows = x.shape[0]
    row = lax.broadcasted_iota(jnp.int32, x.shape, 0)
    d = 1
    while d < rows:
        x = jnp.maximum(x, jnp.where(row >= d, _shift_rows(x, d), -jnp.inf))
        d *= 2
    return x


def _log_sigmoid(x):
    return jnp.minimum(x, 0.0) - jnp.log1p(jnp.exp(-jnp.abs(x)))


def _mlstm_chunk(streams, bias_ref, gain_ref, *, chunk, n_valid):
    scale = HEAD_DIM ** -0.5
    last = n_valid - 1
    row1 = lax.broadcasted_iota(jnp.int32, (chunk, LANES), 0)
    gates = []
    for q_ref, k_ref, v_ref, o_ref, gl_ref, y_ref, c_scr, m_scr, s_scr, r_scr in streams:
        gl = gl_ref[...] + bias_ref[...]
        logf = pltpu.roll(_log_sigmoid(gl), LANES - HEADS, axis=1)
        b = _cumsum_rows(logf)
        a = gl - b
        m_prev = m_scr[...]
        g = jnp.maximum(_cummax_rows(a), m_prev)
        inter_w = jnp.exp(m_prev - g) * scale
        exp_neg_m = jnp.exp(-(b + g))
        g_last = g[last:last + 1, :]
        decay = jnp.exp(m_prev - g_last)
        ws = jnp.where(row1 <= last, jnp.exp(a - g_last), 0.0)
        m_scr[...] = b[last:last + 1, :] + g_last
        a_t = (a + math.log(scale)).T
        gates.append((g, a_t, inter_w, exp_neg_m, ws, decay))

    rows = lax.broadcasted_iota(jnp.int32, (chunk, chunk), 0)
    cols = lax.broadcasted_iota(jnp.int32, (chunk, chunk), 1)
    causal = rows >= cols
    ones = jnp.ones((chunk, HEAD_DIM), BF16)
    head = [slice(h * HEAD_DIM, (h + 1) * HEAD_DIM) for h in range(HEADS)]
    wide = [slice(h * 2 * HEAD_DIM, (h + 1) * 2 * HEAD_DIM) for h in range(HEADS)]
    work = [(h, st, gt) for h in range(HEADS) for st, gt in zip(streams, gates)]

    for h, (q_ref, k_ref, _, _, _, _, _, _, s_scr, _), (g, a_t, _, _, _, _) in work:
        w = jnp.where(causal, jnp.exp(a_t[h:h + 1, :] - g[:, h:h + 1]), 0.0)
        s = lax.dot_general(q_ref[:, head[h]], k_ref[:, head[h]], _NT, preferred_element_type=F32)
        s_scr[h, 0:chunk, 0:chunk] = (s * w).astype(BF16)
    for h, (q_ref, _, v_ref, _, _, _, c_scr, _, s_scr, r_scr), (_, _, inter_w, _, _, _) in work:
        v_aug = jnp.concatenate([v_ref[:, head[h]], ones], axis=1)
        r_scr[0:chunk, wide[h]] = (
            jnp.dot(q_ref[:, head[h]], c_scr[h].astype(BF16), preferred_element_type=F32)
            * inter_w[:, h:h + 1]
            + jnp.dot(s_scr[h, 0:chunk, 0:chunk], v_aug, preferred_element_type=F32))
    for h, (_, k_ref, v_ref, _, _, _, c_scr, _, _, _), (_, _, _, _, ws, decay) in work:
        v_aug = jnp.concatenate([v_ref[:, head[h]], ones], axis=1)
        kw = (k_ref[:, head[h]].astype(F32) * ws[:, h:h + 1]).astype(BF16)
        upd = lax.dot_general(kw, v_aug, (((0,), (0,)), ((), ())), preferred_element_type=F32)
        c_scr[h] = decay[:, h:h + 1] * c_scr[h] + upd
    for h, (_, _, _, o_ref, _, y_ref, _, _, _, r_scr), (_, _, _, exp_neg_m, _, _) in work:
        r = r_scr[0:chunk, wide[h]]
        num, den = r[:, :HEAD_DIM], r[:, HEAD_DIM:]
        hh = num / jnp.maximum(jnp.abs(den), exp_neg_m[:, h:h + 1])
        hn = hh * lax.rsqrt(jnp.mean(hh * hh, axis=-1, keepdims=True) + RMS_EPS)
        y_ref[:, head[h]] = (hn * gain_ref[:, head[h]]
                             * jax.nn.sigmoid(o_ref[:, head[h]].astype(F32))).astype(BF16)


def _mlstm_kernel(q_ref, k_ref, v_ref, o_ref, gl_ref, qm_ref, km_ref, vm_ref, om_ref, glm_ref,
                  bias_ref, gain_ref, y_ref, ym_ref, c_scr, m_scr, c0_scr, m0_scr, s_scr, r_scr, *, group, chunk):
    first_chunk = pl.program_id(1) == 0

    @pl.when(jnp.logical_and(pl.program_id(0) == 0, first_chunk))
    def _():
        c0_scr[...] = jnp.zeros_like(c0_scr)
        m0_scr[...] = jnp.zeros_like(m0_scr)
        meta = (qm_ref, km_ref, vm_ref, om_ref, glm_ref, ym_ref, c0_scr, m0_scr, s_scr.at[0], r_scr.at[0])
        _mlstm_chunk([meta], bias_ref, gain_ref, chunk=META_ROWS, n_valid=N_META)

    @pl.when(first_chunk)
    def _():
        for bb in range(group):
            c_scr[bb] = c0_scr[...]
            m_scr[bb] = m0_scr[...]

    streams = [(q_ref.at[bb], k_ref.at[bb], v_ref.at[bb], o_ref.at[bb], gl_ref.at[bb], y_ref.at[bb],
                c_scr.at[bb], m_scr.at[bb], s_scr.at[bb], r_scr.at[bb]) for bb in range(group)]
    _mlstm_chunk(streams, bias_ref, gain_ref, chunk=chunk, n_valid=chunk)


def _mlstm(big, big_m, logits, logits_m, bias_row, gain, *, batches, group, chunk):
    m = big.shape[0]
    seq = m // batches
    big3 = big.reshape(batches, seq, BIG_COLS)
    logits3 = logits.reshape(batches, seq, LANES)
    cblk = [COL_Q // MIX_WIDTH, COL_K // MIX_WIDTH, COL_V // MIX_WIDTH, COL_O // MIX_WIDTH]
    main = [pl.BlockSpec((group, chunk, MIX_WIDTH), functools.partial(lambda b, c, cb: (b, c, cb), cb=cb))
            for cb in cblk]
    meta = [pl.BlockSpec((META_ROWS, MIX_WIDTH), functools.partial(lambda b, c, cb: (0, cb), cb=cb))
            for cb in cblk]
    in_specs = (main + [pl.BlockSpec((group, chunk, LANES), lambda b, c: (b, c, 0))]
                + meta + [pl.BlockSpec((META_ROWS, LANES), lambda b, c: (0, 0)),
                          pl.BlockSpec((1, LANES), lambda b, c: (0, 0)),
                          pl.BlockSpec((1, MIX_WIDTH), lambda b, c: (0, 0))])
    c_shape = (HEADS, HEAD_DIM, 2 * HEAD_DIM)
    est = (group * (12 * chunk * MIX_WIDTH * 2 + HEADS * chunk * chunk * 2 + chunk * HEADS * 2 * HEAD_DIM * 4
                    + HEADS * HEAD_DIM * 2 * HEAD_DIM * 4 + 24 * chunk * LANES * 4) + (16 << 20))
    y, y_m = pl.pallas_call(
        functools.partial(_mlstm_kernel, group=group, chunk=chunk),
        grid=(batches // group, seq // chunk),
        in_specs=in_specs,
        out_specs=[pl.BlockSpec((group, chunk, MIX_WIDTH), lambda b, c: (b, c, 0)),
                   pl.BlockSpec((META_ROWS, MIX_WIDTH), lambda b, c: (0, 0))],
        out_shape=[jax.ShapeDtypeStruct((batches, seq, MIX_WIDTH), BF16),
                   jax.ShapeDtypeStruct((META_ROWS, MIX_WIDTH), BF16)],
        scratch_shapes=[pltpu.VMEM((group,) + c_shape, F32), pltpu.VMEM((group, 1, LANES), F32),
                        pltpu.VMEM(c_shape, F32), pltpu.VMEM((1, LANES), F32),
                        pltpu.VMEM((group, HEADS, chunk, chunk), BF16),
                        pltpu.VMEM((group, chunk, HEADS * 2 * HEAD_DIM), F32)],
        compiler_params=_params(("arbitrary", "arbitrary"), est),
        name="mlstm",
    )(big3, big3, big3, big3, logits3, big_m, big_m, big_m, big_m, logits_m, bias_row, gain)
    return y.reshape(m, MIX_WIDTH), y_m


POOL_BLOCK = 128
CONV_ROWS, CONV_COLS = 128, 256


def _pool_bands():
    bands = np.zeros((len(POOL_WINDOWS), POOL_BLOCK, 2 * POOL_BLOCK), np.float32)
    for grp, win in enumerate(POOL_WINDOWS):
        for r in range(POOL_BLOCK):
            for lag in range(win):
                bands[grp, r, POOL_BLOCK + r - lag] += 1.0 / win
            bands[grp, r, POOL_BLOCK + r] -= 1.0
    return jnp.asarray(bands, BF16)


def _conv3(cb_ref, cc_ref, cx_ref, halo, cw_ref, yb_ref, p_scr, *, rows):
    p_scr[0:HALO, :] = halo
    pieces = [(r, c) for r in range(0, rows, CONV_ROWS) for c in range(0, MIX_WIDTH, CONV_COLS)]
    for r, c in pieces:
        rs, cs = slice(r, r + CONV_ROWS), slice(c, c + CONV_COLS)
        p_scr[HALO + r:HALO + r + CONV_ROWS, cs] = cc_ref[rs, cs].astype(F32) * cx_ref[rs, cs].astype(F32)
    for r, c in pieces:
        rs, cs = slice(r, r + CONV_ROWS), slice(c, c + CONV_COLS)
        conv = cw_ref[2:3, cs] * p_scr[HALO + r:HALO + r + CONV_ROWS, cs]
        conv = conv + cw_ref[1:2, cs] * p_scr[HALO + r - 1:HALO + r - 1 + CONV_ROWS, cs]
        conv = conv + cw_ref[0:1, cs] * p_scr[HALO + r - 2:HALO + r - 2 + CONV_ROWS, cs]
        yb_ref[rs, cs] = (cb_ref[rs, cs].astype(F32) * conv).astype(BF16)


def _pool_from_start(pu_ref, pw_scr, ps_ref, yc_ref, u_scr, *, rows):
    u_scr[0:HALO, :] = jnp.zeros((HALO, MIX_WIDTH), F32)
    u_scr[HALO:HALO + rows, :] = pu_ref[...].astype(F32)
    t = lax.broadcasted_iota(jnp.int32, (rows, 1), 0)
    for grp, win in enumerate(POOL_WINDOWS):
        sl = slice(grp * POOL_GROUP, (grp + 1) * POOL_GROUP)
        cur = u_scr[HALO:HALO + rows, sl]
        tot = cur
        for lag in range(1, win):
            tot = tot + u_scr[HALO - lag:HALO - lag + rows, sl]
        pooled = tot / jnp.minimum(t + 1, win).astype(F32) - cur
        y = jnp.dot(pooled.astype(BF16), pw_scr[grp], preferred_element_type=F32)
        yc_ref[:, sl] = (y * ps_ref[:, sl]).astype(BF16)


def _pool_banded(band_ref, pw_scr, ps_ref, yc_ref, ub_scr, *, rows):
    for grp in range(len(POOL_WINDOWS)):
        sl = slice(grp * POOL_GROUP, (grp + 1) * POOL_GROUP)
        pooled = jnp.concatenate(
            [jnp.dot(band_ref[grp], ub_scr[r:r + 2 * POOL_BLOCK, sl], preferred_element_type=F32)
             for r in range(0, rows, POOL_BLOCK)], axis=0)
        y = jnp.dot(pooled.astype(BF16), pw_scr[grp], preferred_element_type=F32)
        yc_ref[:, sl] = (y * ps_ref[:, sl]).astype(BF16)


def _conv_pool_kernel(cb_ref, cc_ref, cx_ref, pu_ref, ccp_ref, cxp_ref, pup_ref,
                      cbm_ref, ccm_ref, cxm_ref, pum_ref, cw_ref, pw_ref, ps_ref, band_ref,
                      yb_ref, yc_ref, ybm_ref, ycm_ref, pw_scr, p_scr, u_scr, ub_scr, *, bm, tiles_per_batch):
    @pl.when(pl.program_id(0) == 0)
    def _():
        pw_scr[...] = pw_ref[...].astype(BF16)
        _conv3(cbm_ref, ccm_ref, cxm_ref, jnp.zeros((HALO, MIX_WIDTH), F32), cw_ref, ybm_ref, p_scr,
               rows=META_ROWS)
        _pool_from_start(pum_ref, pw_scr, ps_ref, ycm_ref, u_scr, rows=META_ROWS)

    first = pl.program_id(0) % tiles_per_batch == 0
    cch = jnp.where(first, ccm_ref[0:HALO, :], ccp_ref[...]).astype(F32)
    cxh = jnp.where(first, cxm_ref[0:HALO, :], cxp_ref[...]).astype(F32)
    _conv3(cb_ref, cc_ref, cx_ref, cch * cxh, cw_ref, yb_ref, p_scr, rows=bm)
    meta_tail = jnp.concatenate([jnp.zeros((POOL_BLOCK - N_META, MIX_WIDTH), BF16), pum_ref[0:N_META, :]], axis=0)
    ub_scr[0:POOL_BLOCK, :] = jnp.where(first, meta_tail, pup_ref[...])
    ub_scr[POOL_BLOCK:POOL_BLOCK + bm, :] = pu_ref[...]
    _pool_banded(band_ref, pw_scr, ps_ref, yc_ref, ub_scr, rows=bm)


def _conv_pool(big, big_m, conv_w, pool_w, pool_scale, layer, *, bm, tiles_per_batch):
    m = big.shape[0]
    cblk = [COL_CB // MIX_WIDTH, COL_CC // MIX_WIDTH, COL_CX // MIX_WIDTH, COL_PU // MIX_WIDTH]

    def prev_rows(rows, c):
        per = bm // rows
        return pl.BlockSpec((rows, MIX_WIDTH), lambda i: (jnp.maximum(i * per - 1, 0), c))

    in_specs = [pl.BlockSpec((bm, MIX_WIDTH), functools.partial(lambda i, c: (i, c), c=c)) for c in cblk]
    in_specs += [prev_rows(HALO, cblk[1]), prev_rows(HALO, cblk[2]), prev_rows(POOL_BLOCK, cblk[3])]
    in_specs += [pl.BlockSpec((META_ROWS, MIX_WIDTH), functools.partial(lambda i, c: (0, c), c=c))
                 for c in cblk]
    in_specs += [pl.BlockSpec((None, CONV_WIDTH, MIX_WIDTH), lambda i: (layer, 0, 0)),
                 pl.BlockSpec((None, len(POOL_WINDOWS), POOL_GROUP, POOL_GROUP), lambda i: (layer, 0, 0, 0)),
                 pl.BlockSpec((1, MIX_WIDTH), lambda i: (0, 0)),
                 pl.BlockSpec((len(POOL_WINDOWS), POOL_BLOCK, 2 * POOL_BLOCK), lambda i: (0, 0, 0))]
    out_spec = pl.BlockSpec((bm, MIX_WIDTH), lambda i: (i, 0))
    meta_spec = pl.BlockSpec((META_ROWS, MIX_WIDTH), lambda i: (0, 0))
    return pl.pallas_call(
        functools.partial(_conv_pool_kernel, bm=bm, tiles_per_batch=tiles_per_batch),
        grid=(m // bm,),
        in_specs=in_specs,
        out_specs=[out_spec, out_spec, meta_spec, meta_spec],
        out_shape=[jax.ShapeDtypeStruct((m, MIX_WIDTH), BF16)] * 2
        + [jax.ShapeDtypeStruct((META_ROWS, MIX_WIDTH), BF16)] * 2,
        scratch_shapes=[pltpu.VMEM((len(POOL_WINDOWS), POOL_GROUP, POOL_GROUP), BF16),
                        pltpu.VMEM((HALO + max(bm, META_ROWS), MIX_WIDTH), F32),
                        pltpu.VMEM((HALO + META_ROWS, MIX_WIDTH), F32),
                        pltpu.VMEM((POOL_BLOCK + bm, MIX_WIDTH), BF16)],
        compiler_params=_params(("arbitrary",), 14 * bm * MIX_WIDTH * 4 + (10 << 20)),
        name="conv_pool",
    )(big, big, big, big, big, big, big, big_m, big_m, big_m, big_m, conv_w, pool_w, pool_scale, _pool_bands())


def _branch_tile(ya_ref, yb_ref, yc_ref, w_scr, g0_ref, g1_ref, g2_ref, o_ref):
    acc = None
    for y_ref, g_ref, n in ((ya_ref, g0_ref, 0), (yb_ref, g1_ref, 1), (yc_ref, g2_ref, 2)):
        br = jnp.dot(y_ref[...], w_scr[n], preferred_element_type=F32)
        term = jax.nn.sigmoid(g_ref[...].astype(F32)) * br
        acc = term if acc is None else acc + term
    o_ref[...] = acc.astype(BF16)


def _branch_kernel(ya_ref, yb_ref, yc_ref, g0_ref, g1_ref, g2_ref,
                   yam_ref, ybm_ref, ycm_ref, g0m_ref, g1m_ref, g2m_ref, w_ref, o_ref, om_ref, w_scr):
    @pl.when(pl.program_id(1) == 0)
    def _():
        w_scr[...] = w_ref[...].astype(BF16)
        _branch_tile(yam_ref, ybm_ref, ycm_ref, w_scr, g0m_ref, g1m_ref, g2m_ref, om_ref)

    _branch_tile(ya_ref, yb_ref, yc_ref, w_scr, g0_ref, g1_ref, g2_ref, o_ref)


def _branch(ys, ys_m, w_branch, big, big_m, layer, *, bm, bn):
    m = big.shape[0]
    offs = [(COL_GATES + n * D_MODEL) // bn for n in range(N_BRANCH)]
    y_spec = pl.BlockSpec((bm, MIX_WIDTH), lambda j, i: (i, 0))
    ym_spec = pl.BlockSpec((META_ROWS, MIX_WIDTH), lambda j, i: (0, 0))
    g_specs = [pl.BlockSpec((bm, bn), functools.partial(lambda j, i, off: (i, off + j), off=off)) for off in offs]
    gm_specs = [pl.BlockSpec((META_ROWS, bn), functools.partial(lambda j, i, off: (0, off + j), off=off))
                for off in offs]
    est = (2 * (3 * bm * MIX_WIDTH * 2 + 3 * MIX_WIDTH * bn * 4 + 4 * bm * bn * 2) + 3 * MIX_WIDTH * bn * 2
           + 6 * bm * bn * 4 + (4 << 20))
    return pl.pallas_call(
        _branch_kernel,
        grid=(D_MODEL // bn, m // bm),
        in_specs=[y_spec] * 3 + g_specs + [ym_spec] * 3 + gm_specs
        + [pl.BlockSpec((None, N_BRANCH, MIX_WIDTH, bn), lambda j, i: (layer, 0, 0, j))],
        out_specs=[pl.BlockSpec((bm, bn), lambda j, i: (i, j)),
                   pl.BlockSpec((META_ROWS, bn), lambda j, i: (0, j))],
        out_shape=[jax.ShapeDtypeStruct((m, D_MODEL), BF16), jax.ShapeDtypeStruct((META_ROWS, D_MODEL), BF16)],
        scratch_shapes=[pltpu.VMEM((N_BRANCH, MIX_WIDTH, bn), BF16)],
        compiler_params=_params(("arbitrary", "arbitrary"), est),
        name="branch_merge",
    )(*ys, big, big, big, *ys_m, big_m, big_m, big_m, w_branch)


FINISH_ROWS = 64


def _finish_rows(y_scr, slot, x_ref, gpost_ref, gnext_ref, xo_ref, ho_ref, *, row0, rows):
    n_n = y_scr.shape[1]
    for s in range(0, rows, FINISH_ROWS):
        if isinstance(row0, int):
            ys = slice(row0 + s, row0 + s + FINISH_ROWS)
        else:
            ys = pl.ds(pl.multiple_of(row0 + s, FINISH_ROWS), FINISH_ROWS)
        y = jnp.concatenate([y_scr[slot, c, ys, :] for c in range(n_n)], axis=1)
        xn = x_ref[s:s + FINISH_ROWS, :] + _rms(y, gpost_ref[...])
        xo_ref[s:s + FINISH_ROWS, :] = xn
        if ho_ref is not None:
            ho_ref[s:s + FINISH_ROWS, :] = _rms(xn, gnext_ref[...]).astype(BF16)


def _proj_residual_kernel(*refs, n_i, n_n, part, emit_h, with_meta):
    a_ref, w_ref, x_ref, gpost_ref = refs[:4]
    refs = refs[4:]
    gnext_ref = None
    if emit_h:
        gnext_ref = refs[0]
        refs = refs[1:]
    if with_meta:
        am_ref, xm_ref = refs[:2]
        refs = refs[2:]
    xo_ref = refs[0]
    refs = refs[1:]
    ho_ref = None
    if emit_h:
        ho_ref = refs[0]
        refs = refs[1:]
    if with_meta:
        xmo_ref = refs[0]
        refs = refs[1:]
        hmo_ref = None
        if emit_h:
            hmo_ref = refs[0]
            refs = refs[1:]
        y_scr, ym_scr = refs
    else:
        y_scr, = refs
    i = pl.program_id(0)
    n = pl.program_id(1) if n_n > 1 else 0

    def step(slot, do_matmul, do_finish):
        if do_finish:
            _finish_rows(y_scr, 1 - slot, x_ref, gpost_ref, gnext_ref, xo_ref, ho_ref, row0=n * part, rows=part)
        if do_matmul:
            y_scr[slot, n] = jnp.dot(a_ref[...], w_ref[...], preferred_element_type=F32)

    @pl.when(i == 0)
    def _():
        if with_meta:
            ym_scr[0, n] = jnp.dot(am_ref[...], w_ref[...], preferred_element_type=F32)

            def finish_meta():
                _finish_rows(ym_scr, 0, xm_ref, gpost_ref, gnext_ref, xmo_ref, hmo_ref, row0=0, rows=META_ROWS)
            if n_n > 1:
                pl.when(n == n_n - 1)(finish_meta)
            else:
                finish_meta()
        step(0, True, False)

    for parity in (0, 1):
        @pl.when(jnp.logical_and(jnp.logical_and(i > 0, i < n_i), lax.rem(i, 2) == parity))
        def _():
            step(parity, True, True)

    @pl.when(i == n_i)
    def _():
        step(n_i % 2, False, True)


def _proj_residual(a, a_m, w, layer, x, x_m, g_post, g_next, *, bm, bn, name):
    m, kdim = a.shape
    n_i = m // bm
    n_n = D_MODEL // bn
    part = bm // n_n
    emit_h = g_next is not None
    with_meta = a_m is not None
    const = lambda i, n: (0, 0)
    prev_part = lambda i, n: (jnp.where(i == 0, 0, (i - 1) * n_n + n), 0)
    vec = pl.BlockSpec((1, D_MODEL), const)
    w_mode = dict(pipeline_mode=pl.Buffered(1)) if n_n == 1 else {}
    in_specs = [pl.BlockSpec((bm, kdim), lambda i, n: (jnp.minimum(i, n_i - 1), 0)),
                pl.BlockSpec((None, kdim, bn), lambda i, n: (layer, 0, n), **w_mode),
                pl.BlockSpec((part, D_MODEL), prev_part), vec]
    args = [a, w, x, g_post]
    if emit_h:
        in_specs.append(vec)
        args.append(g_next)
    if with_meta:
        in_specs += [pl.BlockSpec((META_ROWS, kdim), const), pl.BlockSpec((META_ROWS, D_MODEL), const)]
        args += [a_m, x_m]
    out_specs = [pl.BlockSpec((part, D_MODEL), prev_part)]
    out_shape = [jax.ShapeDtypeStruct((m, D_MODEL), F32)]
    if emit_h:
        out_specs.append(pl.BlockSpec((part, D_MODEL), prev_part))
        out_shape.append(jax.ShapeDtypeStruct((m, D_MODEL), BF16))
    scratch = [pltpu.VMEM((2, n_n, bm, bn), F32)]
    if with_meta:
        out_specs.append(pl.BlockSpec((META_ROWS, D_MODEL), const))
        out_shape.append(jax.ShapeDtypeStruct((META_ROWS, D_MODEL), F32))
        if emit_h:
            out_specs.append(pl.BlockSpec((META_ROWS, D_MODEL), const))
            out_shape.append(jax.ShapeDtypeStruct((META_ROWS, D_MODEL), BF16))
        scratch.append(pltpu.VMEM((1, n_n, META_ROWS, bn), F32))
    w_bufs = 1 if n_n == 1 else 2
    est = (2 * (bm * kdim * 2 + part * D_MODEL * (4 + 4 + 2)) + w_bufs * kdim * bn * 2 + 2 * bm * D_MODEL * 4
           + 2 * bm * bn * 4 + 2 * META_ROWS * (kdim * 2 + 3 * D_MODEL * 4) + (4 << 20))
    outs = pl.pallas_call(
        functools.partial(_proj_residual_kernel, n_i=n_i, n_n=n_n, part=part, emit_h=emit_h, with_meta=with_meta),
        grid=(n_i + 1, n_n),
        in_specs=in_specs,
        out_specs=out_specs,
        out_shape=out_shape,
        scratch_shapes=scratch,
        compiler_params=_params(("arbitrary", "arbitrary"), est),
        name=name,
    )(*args)
    outs = list(outs)
    x_new = outs.pop(0)
    h_new = outs.pop(0) if emit_h else None
    xm_new = outs.pop(0) if with_meta else None
    hm_new = outs.pop(0) if (with_meta and emit_h) else None
    return x_new, h_new, xm_new, hm_new


def _gelu_tanh(x):
    return 0.5 * x * (1.0 + jnp.tanh(0.7978845608028654 * (x + 0.044715 * (x * x * x))))


def _ffn_in_tile(h_ref, wa_scr, wu_scr, cw_ref, halo, o_ref, a_scr, *, rows):
    cols = o_ref.shape[1]
    a_scr[0:SUBLANES, :] = jnp.zeros((SUBLANES, cols), F32) if halo is None else halo
    for r in range(0, rows, MATMUL_ROWS):
        n = min(MATMUL_ROWS, rows - r)
        h = h_ref[r:r + n, :]
        a = jnp.dot(h, wa_scr[...], preferred_element_type=F32)
        u = jnp.dot(h, wu_scr[...], preferred_element_type=F32)
        a_scr[SUBLANES + r:SUBLANES + r + n, :] = a
        conv = cw_ref[2:3, :] * a
        conv = conv + cw_ref[1:2, :] * a_scr[SUBLANES + r - 1:SUBLANES + r - 1 + n, :]
        conv = conv + cw_ref[0:1, :] * a_scr[SUBLANES + r - 2:SUBLANES + r - 2 + n, :]
        o_ref[r:r + n, :] = (_gelu_tanh(conv) * u).astype(BF16)


def _ffn_in_kernel(h_ref, hm_ref, wa_ref, wu_ref, cw_ref, o_ref, om_ref,
                   wa_scr, wu_scr, a_scr, tail_scr, mtail_scr, *, bm, tiles_per_batch):
    @pl.when(pl.program_id(1) == 0)
    def _():
        wa_scr[...] = wa_ref[...].astype(BF16)
        wu_scr[...] = wu_ref[...].astype(BF16)
        _ffn_in_tile(hm_ref, wa_scr, wu_scr, cw_ref, None, om_ref, a_scr, rows=META_ROWS)
        mtail_scr[...] = a_scr[N_META:N_META + SUBLANES, :]

    first = pl.program_id(1) % tiles_per_batch == 0
    halo = jnp.where(first, mtail_scr[...], tail_scr[...])
    _ffn_in_tile(h_ref, wa_scr, wu_scr, cw_ref, halo, o_ref, a_scr, rows=bm)
    tail_scr[...] = a_scr[bm:bm + SUBLANES, :]


def _ffn_in(h, hm, w_ffn_in, conv_w, layer, *, bm, bn, tiles_per_batch):
    m = h.shape[0]
    n_j = D_FF // bn
    est = (2 * (bm * D_MODEL * 2 + 2 * D_MODEL * bn * 4 + bm * bn * 2) + 2 * D_MODEL * bn * 2
           + 9 * bm * bn * 4 + (4 << 20))
    return pl.pallas_call(
        functools.partial(_ffn_in_kernel, bm=bm, tiles_per_batch=tiles_per_batch),
        grid=(n_j, m // bm),
        in_specs=[pl.BlockSpec((bm, D_MODEL), lambda j, i: (i, 0)),
                  pl.BlockSpec((META_ROWS, D_MODEL), lambda j, i: (0, 0)),
                  pl.BlockSpec((None, D_MODEL, bn), lambda j, i: (layer, 0, j)),
                  pl.BlockSpec((None, D_MODEL, bn), lambda j, i: (layer, 0, j + n_j)),
                  pl.BlockSpec((None, CONV_WIDTH, bn), lambda j, i: (layer, 0, j))],
        out_specs=[pl.BlockSpec((bm, bn), lambda j, i: (i, j)),
                   pl.BlockSpec((META_ROWS, bn), lambda j, i: (0, j))],
        out_shape=[jax.ShapeDtypeStruct((m, D_FF), BF16), jax.ShapeDtypeStruct((META_ROWS, D_FF), BF16)],
        scratch_shapes=[pltpu.VMEM((D_MODEL, bn), BF16), pltpu.VMEM((D_MODEL, bn), BF16),
                        pltpu.VMEM((SUBLANES + bm, bn), F32),
                        pltpu.VMEM((SUBLANES, bn), F32), pltpu.VMEM((SUBLANES, bn), F32)],
        compiler_params=_params(("arbitrary", "arbitrary"), est),
        name="ffn_in",
    )(h, hm, w_ffn_in, w_ffn_in, conv_w)


def kernel(x, meta_tokens, norm_pre_mix, norm_post_mix, norm_pre_ffn, norm_post_ffn, w_in, b_if,
           mlstm_head_gain, conv_mix_w, pool_w, pool_scale, w_branch, w_out, w_ffn_in, ffn_conv_w,
           w_ffn_out):
    batch, seq, _ = x.shape
    depth = w_in.shape[0]
    m = batch * seq
    xr = x.reshape(m, D_MODEL)
    xm = jnp.pad(meta_tokens.astype(F32), ((0, META_ROWS - N_META), (0, 0)))

    wo = w_out.astype(BF16)
    wfo = w_ffn_out.astype(BF16)
    w_branch, w_ffn_in, pool_w = (t.astype(F32) for t in (w_branch, w_ffn_in, pool_w))
    w_in_t = jnp.swapaxes(w_in.astype(F32), 1, 2)
    cw = conv_mix_w.astype(F32)
    fcw = ffn_conv_w.astype(F32)

    def vec(v):
        return v.reshape(1, -1).astype(F32)

    hr = _norm(xr, vec(norm_pre_mix[0]), 512)
    hm = _norm(xm, vec(norm_pre_mix[0]), META_ROWS)
    for l in range(depth):
        bias_row = jnp.pad(b_if[l].reshape(1, N_LOGITS).astype(F32), ((0, 0), (0, LANES - N_LOGITS)))
        gain = vec(mlstm_head_gain[l])
        ps = vec(pool_scale[l])
        g_post_mix, g_pre_ffn, g_post_ffn = vec(norm_post_mix[l]), vec(norm_pre_ffn[l]), vec(norm_post_ffn[l])
        last_layer = l + 1 == depth
        g_next = None if last_layer else vec(norm_pre_mix[l + 1])

        big, big_m, logits, logits_m = _in_proj(hr, hm, w_in_t, l, bm=2048, bn=1024)
        ya, ya_m = _mlstm(big, big_m, logits, logits_m, bias_row, gain, batches=batch, group=MLSTM_GROUP,
                          chunk=MLSTM_CHUNK)
        yb, yc, yb_m, yc_m = _conv_pool(big, big_m, cw, pool_w, ps, l, bm=512, tiles_per_batch=seq // 512)
        mg, mg_m = _branch((ya, yb, yc), (ya_m, yb_m, yc_m), w_branch, big, big_m, l, bm=1024, bn=512)
        xr, hf, xm, hf_m = _proj_residual(mg, mg_m, wo, l, xr, xm, g_post_mix, g_pre_ffn,
                                          bm=512, bn=D_MODEL, name="out_proj")
        act, act_m = _ffn_in(hf, hf_m, w_ffn_in, fcw, l, bm=2048, bn=512, tiles_per_batch=seq // 2048)
        xr, hr, xm, hm = _proj_residual(act, None if last_layer else act_m, wfo, l, xr, xm, g_post_ffn, g_next,
                                        bm=256, bn=D_MODEL, name="ffn_out")
    return xr.reshape(batch, seq, D_MODEL)
```

```python
import functools
import math

import jax
import jax.numpy as jnp
import numpy as np
from jax import lax
from jax.experimental import pallas as pl
from jax.experimental.pallas import tpu as pltpu

D_MODEL = 2048
N_META = 16
MIX_WIDTH = 1024
N_BRANCH = 3
HEADS = 8
HEAD_DIM = MIX_WIDTH // HEADS
POOL_WINDOWS = (2, 4, 8, 16)
POOL_GROUP = MIX_WIDTH // len(POOL_WINDOWS)
D_FF = 5632
RMS_EPS = 1e-6
CONV_WIDTH = 3

COL_Q, COL_K, COL_V, COL_O, COL_CB, COL_CC, COL_CX, COL_PU = (n * MIX_WIDTH for n in range(8))
COL_GATES = 8 * MIX_WIDTH
BIG_COLS = COL_GATES + N_BRANCH * D_MODEL
N_LOGITS = 2 * HEADS
LANES = 128
HALO = 16
SUBLANES = 8
META_ROWS = 128
MLSTM_CHUNK = 512
MLSTM_GROUP = 1
VMEM_LIMIT_CAP = 60 * 1024 * 1024
MATMUL_ROWS = 1024

F32 = jnp.float32
BF16 = jnp.bfloat16


def _params(semantics, vmem_bytes):
    return pltpu.CompilerParams(dimension_semantics=semantics,
                                vmem_limit_bytes=min(int(vmem_bytes), VMEM_LIMIT_CAP))


def _rms(x, g):
    return x * lax.rsqrt(jnp.mean(x * x, axis=-1, keepdims=True) + RMS_EPS) * g


def _norm_kernel(x_ref, g_ref, h_ref):
    h_ref[...] = _rms(x_ref[...], g_ref[...]).astype(BF16)


def _norm(x, g, bm):
    m = x.shape[0]
    return pl.pallas_call(
        _norm_kernel,
        grid=(m // bm,),
        in_specs=[pl.BlockSpec((bm, D_MODEL), lambda i: (i, 0)),
                  pl.BlockSpec((1, D_MODEL), lambda i: (0, 0))],
        out_specs=pl.BlockSpec((bm, D_MODEL), lambda i: (i, 0)),
        out_shape=jax.ShapeDtypeStruct((m, D_MODEL), BF16),
        compiler_params=_params(("parallel",), 10 * bm * D_MODEL * 4),
        name="pre_norm",
    )(x, g)


_NT = (((1,), (1,)), ((), ()))


def _in_proj_kernel(h_ref, hm_ref, wt_ref, wl_ref, o_ref, om_ref, lg_ref, lgm_ref, w_scr, wl_scr):
    first_tile = pl.program_id(1) == 0

    @pl.when(first_tile)
    def _():
        w_scr[...] = wt_ref[0].astype(BF16)
        om_ref[...] = lax.dot_general(hm_ref[...], w_scr[...], _NT, preferred_element_type=F32).astype(BF16)

    for r in range(0, o_ref.shape[0], MATMUL_ROWS):
        rs = slice(r, r + MATMUL_ROWS)
        o_ref[rs, :] = lax.dot_general(h_ref[rs, :], w_scr[...], _NT, preferred_element_type=F32).astype(BF16)

    @pl.when(pl.program_id(0) == 0)
    def _():
        @pl.when(first_tile)
        def _():
            wl_scr[...] = wl_ref[0].astype(BF16)
            lgm_ref[...] = lax.dot_general(hm_ref[...], wl_scr[...], _NT, preferred_element_type=F32)

        lg_ref[...] = lax.dot_general(h_ref[...], wl_scr[...], _NT, preferred_element_type=F32)


def _in_proj(h, hm, w_in_t, layer, *, bm, bn):
    m, k = h.shape
    n_i = m // bm

    def w_rows(j, i):
        skip = jnp.where(j >= COL_CB // bn, N_LOGITS // SUBLANES, 0)
        return layer, (j * (bn // SUBLANES) + skip) * SUBLANES, 0

    est = (2 * (bm * k * 2 + k * bn * 4 + k * LANES * 4 + bm * bn * 2 + bm * LANES * 4) + k * bn * 2
           + 2 * bm * bn * 4 + (4 << 20))
    return pl.pallas_call(
        _in_proj_kernel,
        grid=(BIG_COLS // bn, n_i),
        in_specs=[pl.BlockSpec((bm, k), lambda j, i: (i, 0)),
                  pl.BlockSpec((META_ROWS, k), lambda j, i: (0, 0)),
                  pl.BlockSpec((pl.Element(1), pl.Element(bn), pl.Element(k)), w_rows),
                  pl.BlockSpec((pl.Element(1), pl.Element(LANES), pl.Element(k)), lambda j, i: (layer, COL_CB, 0))],
        out_specs=[pl.BlockSpec((bm, bn), lambda j, i: (i, j)),
                   pl.BlockSpec((META_ROWS, bn), lambda j, i: (0, j)),
                   pl.BlockSpec((bm, LANES), lambda j, i: (jnp.where(j == 0, i, n_i - 1), 0)),
                   pl.BlockSpec((META_ROWS, LANES), lambda j, i: (0, 0))],
        out_shape=[jax.ShapeDtypeStruct((m, BIG_COLS), BF16),
                   jax.ShapeDtypeStruct((META_ROWS, BIG_COLS), BF16),
                   jax.ShapeDtypeStruct((m, LANES), F32),
                   jax.ShapeDtypeStruct((META_ROWS, LANES), F32)],
        scratch_shapes=[pltpu.VMEM((bn, k), BF16), pltpu.VMEM((LANES, k), BF16)],
        compiler_params=_params(("arbitrary", "arbitrary"), est),
        name="in_proj",
    )(h, hm, w_in_t, w_in_t)


def _shift_rows(x, d):
    return pltpu.roll(x, d, axis=0)


def _cumsum_rows(x):
    rows = x.shape[0]
    row = lax.broadcasted_iota(jnp.int32, x.shape, 0)
    d = 1
    while d < rows:
        x = x + jnp.where(row >= d, _shift_rows(x, d), 0.0)
        d *= 2
    return x


def _cummax_rows(x):
    rows = x.shape[0]
    row = lax.broadcasted_iota(jnp.int32, x.shape, 0)
    d = 1
    while d < rows:
        x = jnp.maximum(x, jnp.where(row >= d, _shift_rows(x, d), -jnp.inf))
        d *= 2
    return x


def _log_sigmoid(x):
    return jnp.minimum(x, 0.0) - jnp.log1p(jnp.exp(-jnp.abs(x)))


def _mlstm_chunk(streams, bias_ref, gain_ref, *, chunk, n_valid):
    scale = HEAD_DIM ** -0.5
    last = n_valid - 1
    row1 = lax.broadcasted_iota(jnp.int32, (chunk, LANES), 0)
    gates = []
    for q_ref, k_ref, v_ref, o_ref, gl_ref, y_ref, c_scr, m_scr, s_scr, r_scr in streams:
        gl = gl_ref[...] + bias_ref[...]
        logf = pltpu.roll(_log_sigmoid(gl), LANES - HEADS, axis=1)
        b = _cumsum_rows(logf)
        a = gl - b
        m_prev = m_scr[...]
        g = jnp.maximum(_cummax_rows(a), m_prev)
        inter_w = jnp.exp(m_prev - g) * scale
        exp_neg_m = jnp.exp(-(b + g))
        g_last = g[last:last + 1, :]
        decay = jnp.exp(m_prev - g_last)
        ws = jnp.where(row1 <= last, jnp.exp(a - g_last), 0.0)
        m_scr[...] = b[last:last + 1, :] + g_last
        a_t = (a + math.log(scale)).T
        gates.append((g, a_t, inter_w, exp_neg_m, ws, decay))

    rows = lax.broadcasted_iota(jnp.int32, (chunk, chunk), 0)
    cols = lax.broadcasted_iota(jnp.int32, (chunk, chunk), 1)
    causal = rows >= cols
    ones = jnp.ones((chunk, HEAD_DIM), BF16)
    head = [slice(h * HEAD_DIM, (h + 1) * HEAD_DIM) for h in range(HEADS)]
    wide = [slice(h * 2 * HEAD_DIM, (h + 1) * 2 * HEAD_DIM) for h in range(HEADS)]
    work = [(h, st, gt) for h in range(HEADS) for st, gt in zip(streams, gates)]

    for h, (q_ref, k_ref, _, _, _, _, _, _, s_scr, _), (g, a_t, _, _, _, _) in work:
        w = jnp.where(causal, jnp.exp(a_t[h:h + 1, :] - g[:, h:h + 1]), 0.0)
        s = lax.dot_general(q_ref[:, head[h]], k_ref[:, head[h]], _NT, preferred_element_type=F32)
        s_scr[h, 0:chunk, 0:chunk] = (s * w).astype(BF16)
    for h, (q_ref, _, v_ref, _, _, _, c_scr, _, s_scr, r_scr), (_, _, inter_w, _, _, _) in work:
        v_aug = jnp.concatenate([v_ref[:, head[h]], ones], axis=1)
        r_scr[0:chunk, wide[h]] = (
            jnp.dot(q_ref[:, head[h]], c_scr[h].astype(BF16), preferred_element_type=F32)
            * inter_w[:, h:h + 1]
            + jnp.dot(s_scr[h, 0:chunk, 0:chunk], v_aug, preferred_element_type=F32))
    for h, (_, k_ref, v_ref, _, _, _, c_scr, _, _, _), (_, _, _, _, ws, decay) in work:
        v_aug = jnp.concatenate([v_ref[:, head[h]], ones], axis=1)
        kw = (k_ref[:, head[h]].astype(F32) * ws[:, h:h + 1]).astype(BF16)
        upd = lax.dot_general(kw, v_aug, (((0,), (0,)), ((), ())), preferred_element_type=F32)
        c_scr[h] = decay[:, h:h + 1] * c_scr[h] + upd
    for h, (_, _, _, o_ref, _, y_ref, _, _, _, r_scr), (_, _, _, exp_neg_m, _, _) in work:
        r = r_scr[0:chunk, wide[h]]
        num, den = r[:, :HEAD_DIM], r[:, HEAD_DIM:]
        hh = num / jnp.maximum(jnp.abs(den), exp_neg_m[:, h:h + 1])
        hn = hh * lax.rsqrt(jnp.mean(hh * hh, axis=-1, keepdims=True) + RMS_EPS)
        y_ref[:, head[h]] = (hn * gain_ref[:, head[h]]
                             * jax.nn.sigmoid(o_ref[:, head[h]].astype(F32))).astype(BF16)


def _mlstm_kernel(q_ref, k_ref, v_ref, o_ref, gl_ref, qm_ref, km_ref, vm_ref, om_ref, glm_ref,
                  bias_ref, gain_ref, y_ref, ym_ref, c_scr, m_scr, c0_scr, m0_scr, s_scr, r_scr, *, group, chunk):
    first_chunk = pl.program_id(1) == 0

    @pl.when(jnp.logical_and(pl.program_id(0) == 0, first_chunk))
    def _():
        c0_scr[...] = jnp.zeros_like(c0_scr)
        m0_scr[...] = jnp.zeros_like(m0_scr)
        meta = (qm_ref, km_ref, vm_ref, om_ref, glm_ref, ym_ref, c0_scr, m0_scr, s_scr.at[0], r_scr.at[0])
        _mlstm_chunk([meta], bias_ref, gain_ref, chunk=META_ROWS, n_valid=N_META)

    @pl.when(first_chunk)
    def _():
        for bb in range(group):
            c_scr[bb] = c0_scr[...]
            m_scr[bb] = m0_scr[...]

    streams = [(q_ref.at[bb], k_ref.at[bb], v_ref.at[bb], o_ref.at[bb], gl_ref.at[bb], y_ref.at[bb],
                c_scr.at[bb], m_scr.at[bb], s_scr.at[bb], r_scr.at[bb]) for bb in range(group)]
    _mlstm_chunk(streams, bias_ref, gain_ref, chunk=chunk, n_valid=chunk)


def _mlstm(big, big_m, logits, logits_m, bias_row, gain, *, batches, group, chunk):
    m = big.shape[0]
    seq = m // batches
    big3 = big.reshape(batches, seq, BIG_COLS)
    logits3 = logits.reshape(batches, seq, LANES)
    cblk = [COL_Q // MIX_WIDTH, COL_K // MIX_WIDTH, COL_V // MIX_WIDTH, COL_O // MIX_WIDTH]
    main = [pl.BlockSpec((group, chunk, MIX_WIDTH), functools.partial(lambda b, c, cb: (b, c, cb), cb=cb))
            for cb in cblk]
    meta = [pl.BlockSpec((META_ROWS, MIX_WIDTH), functools.partial(lambda b, c, cb: (0, cb), cb=cb))
            for cb in cblk]
    in_specs = (main + [pl.BlockSpec((group, chunk, LANES), lambda b, c: (b, c, 0))]
                + meta + [pl.BlockSpec((META_ROWS, LANES), lambda b, c: (0, 0)),
                          pl.BlockSpec((1, LANES), lambda b, c: (0, 0)),
                          pl.BlockSpec((1, MIX_WIDTH), lambda b, c: (0, 0))])
    c_shape = (HEADS, HEAD_DIM, 2 * HEAD_DIM)
    est = (group * (12 * chunk * MIX_WIDTH * 2 + HEADS * chunk * chunk * 2 + chunk * HEADS * 2 * HEAD_DIM * 4
                    + HEADS * HEAD_DIM * 2 * HEAD_DIM * 4 + 24 * chunk * LANES * 4) + (16 << 20))
    y, y_m = pl.pallas_call(
        functools.partial(_mlstm_kernel, group=group, chunk=chunk),
        grid=(batches // group, seq // chunk),
        in_specs=in_specs,
        out_specs=[pl.BlockSpec((group, chunk, MIX_WIDTH), lambda b, c: (b, c, 0)),
                   pl.BlockSpec((META_ROWS, MIX_WIDTH), lambda b, c: (0, 0))],
        out_shape=[jax.ShapeDtypeStruct((batches, seq, MIX_WIDTH), BF16),
                   jax.ShapeDtypeStruct((META_ROWS, MIX_WIDTH), BF16)],
        scratch_shapes=[pltpu.VMEM((group,) + c_shape, F32), pltpu.VMEM((group, 1, LANES), F32),
                        pltpu.VMEM(c_shape, F32), pltpu.VMEM((1, LANES), F32),
                        pltpu.VMEM((group, HEADS, chunk, chunk), BF16),
                        pltpu.VMEM((group, chunk, HEADS * 2 * HEAD_DIM), F32)],
        compiler_params=_params(("arbitrary", "arbitrary"), est),
        name="mlstm",
    )(big3, big3, big3, big3, logits3, big_m, big_m, big_m, big_m, logits_m, bias_row, gain)
    return y.reshape(m, MIX_WIDTH), y_m


POOL_BLOCK = 128
MERGE_COLS = 512
CONV_ROWS, CONV_COLS = 128, 256


def _pool_bands():
    bands = np.zeros((len(POOL_WINDOWS), POOL_BLOCK, 2 * POOL_BLOCK), np.float32)
    for grp, win in enumerate(POOL_WINDOWS):
        for r in range(POOL_BLOCK):
            for lag in range(win):
                bands[grp, r, POOL_BLOCK + r - lag] += 1.0 / win
            bands[grp, r, POOL_BLOCK + r] -= 1.0
    return jnp.asarray(bands, BF16)


def _conv3(cb_ref, cc_ref, cx_ref, halo, cw_ref, yb_ref, p_scr, *, rows):
    p_scr[0:HALO, :] = halo
    pieces = [(r, c) for r in range(0, rows, CONV_ROWS) for c in range(0, MIX_WIDTH, CONV_COLS)]
    for r, c in pieces:
        rs, cs = slice(r, r + CONV_ROWS), slice(c, c + CONV_COLS)
        p_scr[HALO + r:HALO + r + CONV_ROWS, cs] = cc_ref[rs, cs].astype(F32) * cx_ref[rs, cs].astype(F32)
    for r, c in pieces:
        rs, cs = slice(r, r + CONV_ROWS), slice(c, c + CONV_COLS)
        conv = cw_ref[2:3, cs] * p_scr[HALO + r:HALO + r + CONV_ROWS, cs]
        conv = conv + cw_ref[1:2, cs] * p_scr[HALO + r - 1:HALO + r - 1 + CONV_ROWS, cs]
        conv = conv + cw_ref[0:1, cs] * p_scr[HALO + r - 2:HALO + r - 2 + CONV_ROWS, cs]
        yb_ref[rs, cs] = (cb_ref[rs, cs].astype(F32) * conv).astype(BF16)


def _pool_from_start(pu_ref, pw_scr, ps_ref, yc_ref, u_scr, *, rows):
    u_scr[0:HALO, :] = jnp.zeros((HALO, MIX_WIDTH), F32)
    u_scr[HALO:HALO + rows, :] = pu_ref[...].astype(F32)
    t = lax.broadcasted_iota(jnp.int32, (rows, 1), 0)
    for grp, win in enumerate(POOL_WINDOWS):
        sl = slice(grp * POOL_GROUP, (grp + 1) * POOL_GROUP)
        cur = u_scr[HALO:HALO + rows, sl]
        tot = cur
        for lag in range(1, win):
            tot = tot + u_scr[HALO - lag:HALO - lag + rows, sl]
        pooled = tot / jnp.minimum(t + 1, win).astype(F32) - cur
        y = jnp.dot(pooled.astype(BF16), pw_scr[grp], preferred_element_type=F32)
        yc_ref[:, sl] = (y * ps_ref[:, sl]).astype(BF16)


def _pool_banded(band_ref, pw_scr, ps_ref, yc_ref, ub_scr, *, rows):
    for grp in range(len(POOL_WINDOWS)):
        sl = slice(grp * POOL_GROUP, (grp + 1) * POOL_GROUP)
        pooled = jnp.concatenate(
            [jnp.dot(band_ref[grp], ub_scr[r:r + 2 * POOL_BLOCK, sl], preferred_element_type=F32)
             for r in range(0, rows, POOL_BLOCK)], axis=0)
        y = jnp.dot(pooled.astype(BF16), pw_scr[grp], preferred_element_type=F32)
        yc_ref[:, sl] = (y * ps_ref[:, sl]).astype(BF16)


def _branch_term(y_ref, w_ref, n, g_ref, cs):
    return (jax.nn.sigmoid(g_ref[:, cs].astype(F32))
            * jnp.dot(y_ref[...], w_ref[n, :, cs], preferred_element_type=F32))


def _merge(ya_ref, yb_ref, yc_ref, w_ref, g0_ref, g1_ref, g2_ref, o_ref):
    for c in range(0, o_ref.shape[1], MERGE_COLS):
        cs = slice(c, c + MERGE_COLS)
        acc = _branch_term(ya_ref, w_ref, 0, g0_ref, cs)
        acc = acc + _branch_term(yb_ref, w_ref, 1, g1_ref, cs)
        o_ref[:, cs] = (acc + _branch_term(yc_ref, w_ref, 2, g2_ref, cs)).astype(BF16)


def _mix_merge_kernel(cb_ref, cc_ref, cx_ref, pu_ref, ccp_ref, cxp_ref, pup_ref, ya_ref, g0_ref, g1_ref, g2_ref,
                      cbm_ref, ccm_ref, cxm_ref, pum_ref, yam_ref, g0m_ref, g1m_ref, g2m_ref,
                      cw_ref, pw_ref, ps_ref, band_ref, w_ref, o_ref, om_ref,
                      pw_scr, p_scr, u_scr, ub_scr, yb_scr, yc_scr, ybm_scr, ycm_scr, *, bm, tiles_per_batch):
    @pl.when(pl.program_id(0) == 0)
    def _():
        pw_scr[...] = pw_ref[...].astype(BF16)
        _conv3(cbm_ref, ccm_ref, cxm_ref, jnp.zeros((HALO, MIX_WIDTH), F32), cw_ref, ybm_scr, p_scr,
               rows=META_ROWS)
        _pool_from_start(pum_ref, pw_scr, ps_ref, ycm_scr, u_scr, rows=META_ROWS)
        _merge(yam_ref, ybm_scr, ycm_scr, w_ref, g0m_ref, g1m_ref, g2m_ref, om_ref)

    first = pl.program_id(0) % tiles_per_batch == 0
    cch = jnp.where(first, ccm_ref[0:HALO, :], ccp_ref[...]).astype(F32)
    cxh = jnp.where(first, cxm_ref[0:HALO, :], cxp_ref[...]).astype(F32)
    _conv3(cb_ref, cc_ref, cx_ref, cch * cxh, cw_ref, yb_scr, p_scr, rows=bm)
    meta_tail = jnp.concatenate([jnp.zeros((POOL_BLOCK - N_META, MIX_WIDTH), BF16), pum_ref[0:N_META, :]], axis=0)
    ub_scr[0:POOL_BLOCK, :] = jnp.where(first, meta_tail, pup_ref[...])
    ub_scr[POOL_BLOCK:POOL_BLOCK + bm, :] = pu_ref[...]
    _pool_banded(band_ref, pw_scr, ps_ref, yc_scr, ub_scr, rows=bm)
    _merge(ya_ref, yb_scr, yc_scr, w_ref, g0_ref, g1_ref, g2_ref, o_ref)


def _mix_merge(big, big_m, ya, ya_m, conv_w, pool_w, pool_scale, w_branch, layer, *, bm, tiles_per_batch):
    m = big.shape[0]
    cblk = [COL_CB // MIX_WIDTH, COL_CC // MIX_WIDTH, COL_CX // MIX_WIDTH, COL_PU // MIX_WIDTH]
    gblk = [(COL_GATES + n * D_MODEL) // D_MODEL for n in range(N_BRANCH)]

    def prev_rows(rows, c):
        per = bm // rows
        return pl.BlockSpec((rows, MIX_WIDTH), lambda i: (jnp.maximum(i * per - 1, 0), c))

    def rows_spec(rows, width, c, meta):
        return pl.BlockSpec((rows, width), (lambda i: (0, c)) if meta else (lambda i: (i, c)))

    in_specs = [rows_spec(bm, MIX_WIDTH, c, False) for c in cblk]
    in_specs += [prev_rows(HALO, cblk[1]), prev_rows(HALO, cblk[2]), prev_rows(POOL_BLOCK, cblk[3])]
    in_specs += [rows_spec(bm, MIX_WIDTH, 0, False)] + [rows_spec(bm, D_MODEL, c, False) for c in gblk]
    in_specs += [rows_spec(META_ROWS, MIX_WIDTH, c, True) for c in cblk]
    in_specs += [rows_spec(META_ROWS, MIX_WIDTH, 0, True)] + [rows_spec(META_ROWS, D_MODEL, c, True) for c in gblk]
    in_specs += [pl.BlockSpec((None, CONV_WIDTH, MIX_WIDTH), lambda i: (layer, 0, 0)),
                 pl.BlockSpec((None, len(POOL_WINDOWS), POOL_GROUP, POOL_GROUP), lambda i: (layer, 0, 0, 0)),
                 pl.BlockSpec((1, MIX_WIDTH), lambda i: (0, 0)),
                 pl.BlockSpec((len(POOL_WINDOWS), POOL_BLOCK, 2 * POOL_BLOCK), lambda i: (0, 0, 0)),
                 pl.BlockSpec((None, N_BRANCH, MIX_WIDTH, D_MODEL), lambda i: (layer, 0, 0, 0),
                              pipeline_mode=pl.Buffered(1))]
    est = (N_BRANCH * MIX_WIDTH * D_MODEL * 2 + 2 * bm * (5 * MIX_WIDTH + 4 * D_MODEL) * 2
           + 4 * bm * D_MODEL * 4 + 6 * bm * MIX_WIDTH * 4 + (10 << 20))
    return pl.pallas_call(
        functools.partial(_mix_merge_kernel, bm=bm, tiles_per_batch=tiles_per_batch),
        grid=(m // bm,),
        in_specs=in_specs,
        out_specs=[pl.BlockSpec((bm, D_MODEL), lambda i: (i, 0)),
                   pl.BlockSpec((META_ROWS, D_MODEL), lambda i: (0, 0))],
        out_shape=[jax.ShapeDtypeStruct((m, D_MODEL), BF16), jax.ShapeDtypeStruct((META_ROWS, D_MODEL), BF16)],
        scratch_shapes=[pltpu.VMEM((len(POOL_WINDOWS), POOL_GROUP, POOL_GROUP), BF16),
                        pltpu.VMEM((HALO + max(bm, META_ROWS), MIX_WIDTH), F32),
                        pltpu.VMEM((HALO + META_ROWS, MIX_WIDTH), F32),
                        pltpu.VMEM((POOL_BLOCK + bm, MIX_WIDTH), BF16),
                        pltpu.VMEM((bm, MIX_WIDTH), BF16), pltpu.VMEM((bm, MIX_WIDTH), BF16),
                        pltpu.VMEM((META_ROWS, MIX_WIDTH), BF16), pltpu.VMEM((META_ROWS, MIX_WIDTH), BF16)],
        compiler_params=_params(("arbitrary",), est),
        name="mix_merge",
    )(big, big, big, big, big, big, big, ya, big, big, big,
      big_m, big_m, big_m, big_m, ya_m, big_m, big_m, big_m,
      conv_w, pool_w, pool_scale, _pool_bands(), w_branch)


FINISH_ROWS = 64


def _finish_rows(y_scr, slot, x_ref, gpost_ref, gnext_ref, xo_ref, ho_ref, *, row0, rows):
    n_n = y_scr.shape[1]
    for s in range(0, rows, FINISH_ROWS):
        if isinstance(row0, int):
            ys = slice(row0 + s, row0 + s + FINISH_ROWS)
        else:
            ys = pl.ds(pl.multiple_of(row0 + s, FINISH_ROWS), FINISH_ROWS)
        y = jnp.concatenate([y_scr[slot, c, ys, :] for c in range(n_n)], axis=1)
        xn = x_ref[s:s + FINISH_ROWS, :] + _rms(y, gpost_ref[...])
        xo_ref[s:s + FINISH_ROWS, :] = xn
        if ho_ref is not None:
            ho_ref[s:s + FINISH_ROWS, :] = _rms(xn, gnext_ref[...]).astype(BF16)


def _proj_residual_kernel(*refs, n_i, n_n, part, emit_h, with_meta):
    a_ref, w_ref, x_ref, gpost_ref = refs[:4]
    refs = refs[4:]
    gnext_ref = None
    if emit_h:
        gnext_ref = refs[0]
        refs = refs[1:]
    if with_meta:
        am_ref, xm_ref = refs[:2]
        refs = refs[2:]
    xo_ref = refs[0]
    refs = refs[1:]
    ho_ref = None
    if emit_h:
        ho_ref = refs[0]
        refs = refs[1:]
    if with_meta:
        xmo_ref = refs[0]
        refs = refs[1:]
        hmo_ref = None
        if emit_h:
            hmo_ref = refs[0]
            refs = refs[1:]
        y_scr, ym_scr = refs
    else:
        y_scr, = refs
    i = pl.program_id(0)
    n = pl.program_id(1) if n_n > 1 else 0

    def step(slot, do_matmul, do_finish):
        if do_finish:
            _finish_rows(y_scr, 1 - slot, x_ref, gpost_ref, gnext_ref, xo_ref, ho_ref, row0=n * part, rows=part)
        if do_matmul:
            y_scr[slot, n] = jnp.dot(a_ref[...], w_ref[...], preferred_element_type=F32)

    @pl.when(i == 0)
    def _():
        if with_meta:
            ym_scr[0, n] = jnp.dot(am_ref[...], w_ref[...], preferred_element_type=F32)

            def finish_meta():
                _finish_rows(ym_scr, 0, xm_ref, gpost_ref, gnext_ref, xmo_ref, hmo_ref, row0=0, rows=META_ROWS)
            if n_n > 1:
                pl.when(n == n_n - 1)(finish_meta)
            else:
                finish_meta()
        step(0, True, False)

    for parity in (0, 1):
        @pl.when(jnp.logical_and(jnp.logical_and(i > 0, i < n_i), lax.rem(i, 2) == parity))
        def _():
            step(parity, True, True)

    @pl.when(i == n_i)
    def _():
        step(n_i % 2, False, True)


def _proj_residual(a, a_m, w, layer, x, x_m, g_post, g_next, *, bm, bn, name):
    m, kdim = a.shape
    n_i = m // bm
    n_n = D_MODEL // bn
    part = bm // n_n
    emit_h = g_next is not None
    with_meta = a_m is not None
    const = lambda i, n: (0, 0)
    prev_part = lambda i, n: (jnp.where(i == 0, 0, (i - 1) * n_n + n), 0)
    vec = pl.BlockSpec((1, D_MODEL), const)
    w_mode = dict(pipeline_mode=pl.Buffered(1)) if n_n == 1 else {}
    in_specs = [pl.BlockSpec((bm, kdim), lambda i, n: (jnp.minimum(i, n_i - 1), 0)),
                pl.BlockSpec((None, kdim, bn), lambda i, n: (layer, 0, n), **w_mode),
                pl.BlockSpec((part, D_MODEL), prev_part), vec]
    args = [a, w, x, g_post]
    if emit_h:
        in_specs.append(vec)
        args.append(g_next)
    if with_meta:
        in_specs += [pl.BlockSpec((META_ROWS, kdim), const), pl.BlockSpec((META_ROWS, D_MODEL), const)]
        args += [a_m, x_m]
    out_specs = [pl.BlockSpec((part, D_MODEL), prev_part)]
    out_shape = [jax.ShapeDtypeStruct((m, D_MODEL), F32)]
    if emit_h:
        out_specs.append(pl.BlockSpec((part, D_MODEL), prev_part))
        out_shape.append(jax.ShapeDtypeStruct((m, D_MODEL), BF16))
    scratch = [pltpu.VMEM((2, n_n, bm, bn), F32)]
    if with_meta:
        out_specs.append(pl.BlockSpec((META_ROWS, D_MODEL), const))
        out_shape.append(jax.ShapeDtypeStruct((META_ROWS, D_MODEL), F32))
        if emit_h:
            out_specs.append(pl.BlockSpec((META_ROWS, D_MODEL), const))
            out_shape.append(jax.ShapeDtypeStruct((META_ROWS, D_MODEL), BF16))
        scratch.append(pltpu.VMEM((1, n_n, META_ROWS, bn), F32))
    w_bufs = 1 if n_n == 1 else 2
    est = (2 * (bm * kdim * 2 + part * D_MODEL * (4 + 4 + 2)) + w_bufs * kdim * bn * 2 + 2 * bm * D_MODEL * 4
           + 2 * bm * bn * 4 + 2 * META_ROWS * (kdim * 2 + 3 * D_MODEL * 4) + (4 << 20))
    outs = pl.pallas_call(
        functools.partial(_proj_residual_kernel, n_i=n_i, n_n=n_n, part=part, emit_h=emit_h, with_meta=with_meta),
        grid=(n_i + 1, n_n),
        in_specs=in_specs,
        out_specs=out_specs,
        out_shape=out_shape,
        scratch_shapes=scratch,
        compiler_params=_params(("arbitrary", "arbitrary"), est),
        name=name,
    )(*args)
    outs = list(outs)
    x_new = outs.pop(0)
    h_new = outs.pop(0) if emit_h else None
    xm_new = outs.pop(0) if with_meta else None
    hm_new = outs.pop(0) if (with_meta and emit_h) else None
    return x_new, h_new, xm_new, hm_new


def _gelu_tanh(x):
    return 0.5 * x * (1.0 + jnp.tanh(0.7978845608028654 * (x + 0.044715 * (x * x * x))))


def _ffn_in_tile(h_ref, wa_scr, wu_scr, cw_ref, halo, o_ref, a_scr, *, rows):
    cols = o_ref.shape[1]
    a_scr[0:SUBLANES, :] = jnp.zeros((SUBLANES, cols), F32) if halo is None else halo
    for r in range(0, rows, MATMUL_ROWS):
        n = min(MATMUL_ROWS, rows - r)
        h = h_ref[r:r + n, :]
        a = jnp.dot(h, wa_scr[...], preferred_element_type=F32)
        u = jnp.dot(h, wu_scr[...], preferred_element_type=F32)
        a_scr[SUBLANES + r:SUBLANES + r + n, :] = a
        conv = cw_ref[2:3, :] * a
        conv = conv + cw_ref[1:2, :] * a_scr[SUBLANES + r - 1:SUBLANES + r - 1 + n, :]
        conv = conv + cw_ref[0:1, :] * a_scr[SUBLANES + r - 2:SUBLANES + r - 2 + n, :]
        o_ref[r:r + n, :] = (_gelu_tanh(conv) * u).astype(BF16)


def _ffn_in_kernel(h_ref, hm_ref, wa_ref, wu_ref, cw_ref, o_ref, om_ref,
                   wa_scr, wu_scr, a_scr, tail_scr, mtail_scr, *, bm, tiles_per_batch):
    @pl.when(pl.program_id(1) == 0)
    def _():
        wa_scr[...] = wa_ref[...].astype(BF16)
        wu_scr[...] = wu_ref[...].astype(BF16)
        _ffn_in_tile(hm_ref, wa_scr, wu_scr, cw_ref, None, om_ref, a_scr, rows=META_ROWS)
        mtail_scr[...] = a_scr[N_META:N_META + SUBLANES, :]

    first = pl.program_id(1) % tiles_per_batch == 0
    halo = jnp.where(first, mtail_scr[...], tail_scr[...])
    _ffn_in_tile(h_ref, wa_scr, wu_scr, cw_ref, halo, o_ref, a_scr, rows=bm)
    tail_scr[...] = a_scr[bm:bm + SUBLANES, :]


def _ffn_in(h, hm, w_ffn_in, conv_w, layer, *, bm, bn, tiles_per_batch):
    m = h.shape[0]
    n_j = D_FF // bn
    est = (2 * (bm * D_MODEL * 2 + 2 * D_MODEL * bn * 4 + bm * bn * 2) + 2 * D_MODEL * bn * 2
           + 9 * bm * bn * 4 + (4 << 20))
    return pl.pallas_call(
        functools.partial(_ffn_in_kernel, bm=bm, tiles_per_batch=tiles_per_batch),
        grid=(n_j, m // bm),
        in_specs=[pl.BlockSpec((bm, D_MODEL), lambda j, i: (i, 0)),
                  pl.BlockSpec((META_ROWS, D_MODEL), lambda j, i: (0, 0)),
                  pl.BlockSpec((None, D_MODEL, bn), lambda j, i: (layer, 0, j)),
                  pl.BlockSpec((None, D_MODEL, bn), lambda j, i: (layer, 0, j + n_j)),
                  pl.BlockSpec((None, CONV_WIDTH, bn), lambda j, i: (layer, 0, j))],
        out_specs=[pl.BlockSpec((bm, bn), lambda j, i: (i, j)),
                   pl.BlockSpec((META_ROWS, bn), lambda j, i: (0, j))],
        out_shape=[jax.ShapeDtypeStruct((m, D_FF), BF16), jax.ShapeDtypeStruct((META_ROWS, D_FF), BF16)],
        scratch_shapes=[pltpu.VMEM((D_MODEL, bn), BF16), pltpu.VMEM((D_MODEL, bn), BF16),
                        pltpu.VMEM((SUBLANES + bm, bn), F32),
                        pltpu.VMEM((SUBLANES, bn), F32), pltpu.VMEM((SUBLANES, bn), F32)],
        compiler_params=_params(("arbitrary", "arbitrary"), est),
        name="ffn_in",
    )(h, hm, w_ffn_in, w_ffn_in, conv_w)


def kernel(x, meta_tokens, norm_pre_mix, norm_post_mix, norm_pre_ffn, norm_post_ffn, w_in, b_if,
           mlstm_head_gain, conv_mix_w, pool_w, pool_scale, w_branch, w_out, w_ffn_in, ffn_conv_w,
           w_ffn_out):
    batch, seq, _ = x.shape
    depth = w_in.shape[0]
    m = batch * seq
    xr = x.reshape(m, D_MODEL)
    xm = jnp.pad(meta_tokens.astype(F32), ((0, META_ROWS - N_META), (0, 0)))

    wo = w_out.astype(BF16)
    wfo = w_ffn_out.astype(BF16)
    wbr = w_branch.astype(BF16)
    w_ffn_in, pool_w = (t.astype(F32) for t in (w_ffn_in, pool_w))
    w_in_t = jnp.swapaxes(w_in.astype(F32), 1, 2)
    cw = conv_mix_w.astype(F32)
    fcw = ffn_conv_w.astype(F32)

    def vec(v):
        return v.reshape(1, -1).astype(F32)

    hr = _norm(xr, vec(norm_pre_mix[0]), 512)
    hm = _norm(xm, vec(norm_pre_mix[0]), META_ROWS)
    for l in range(depth):
        bias_row = jnp.pad(b_if[l].reshape(1, N_LOGITS).astype(F32), ((0, 0), (0, LANES - N_LOGITS)))
        gain = vec(mlstm_head_gain[l])
        ps = vec(pool_scale[l])
        g_post_mix, g_pre_ffn, g_post_ffn = vec(norm_post_mix[l]), vec(norm_pre_ffn[l]), vec(norm_post_ffn[l])
        last_layer = l + 1 == depth
        g_next = None if last_layer else vec(norm_pre_mix[l + 1])

        big, big_m, logits, logits_m = _in_proj(hr, hm, w_in_t, l, bm=2048, bn=1024)
        ya, ya_m = _mlstm(big, big_m, logits, logits_m, bias_row, gain, batches=batch, group=MLSTM_GROUP,
                          chunk=MLSTM_CHUNK)
        mg, mg_m = _mix_merge(big, big_m, ya, ya_m, cw, pool_w, ps, wbr, l, bm=256, tiles_per_batch=seq // 256)
        xr, hf, xm, hf_m = _proj_residual(mg, mg_m, wo, l, xr, xm, g_post_mix, g_pre_ffn,
                                          bm=512, bn=D_MODEL, name="out_proj")
        act, act_m = _ffn_in(hf, hf_m, w_ffn_in, fcw, l, bm=2048, bn=512, tiles_per_batch=seq // 2048)
        xr, hr, xm, hm = _proj_residual(act, None if last_layer else act_m, wfo, l, xr, xm, g_post_ffn, g_next,
                                        bm=256, bn=D_MODEL, name="ffn_out")
    return xr.reshape(batch, seq, D_MODEL)
```

```python
import functools
import math

import jax
import jax.numpy as jnp
import numpy as np
from jax import lax
from jax.experimental import pallas as pl
from jax.experimental.pallas import tpu as pltpu

D_MODEL = 2048
N_META = 16
MIX_WIDTH = 1024
N_BRANCH = 3
HEADS = 8
HEAD_DIM = MIX_WIDTH // HEADS
POOL_WINDOWS = (2, 4, 8, 16)
POOL_GROUP = MIX_WIDTH // len(POOL_WINDOWS)
D_FF = 5632
RMS_EPS = 1e-6
CONV_WIDTH = 3

COL_Q, COL_K, COL_V, COL_O, COL_CB, COL_CC, COL_CX, COL_PU = (n * MIX_WIDTH for n in range(8))
COL_GATES = 8 * MIX_WIDTH
BIG_COLS = COL_GATES + N_BRANCH * D_MODEL
N_LOGITS = 2 * HEADS
LANES = 128
HALO = 16
SUBLANES = 8
META_ROWS = 128
MLSTM_CHUNK = 512
MLSTM_GROUP = 1
VMEM_LIMIT_CAP = 60 * 1024 * 1024
MATMUL_ROWS = 1024

F32 = jnp.float32
BF16 = jnp.bfloat16


def _params(semantics, vmem_bytes):
    return pltpu.CompilerParams(dimension_semantics=semantics,
                                vmem_limit_bytes=min(int(vmem_bytes), VMEM_LIMIT_CAP))


def _rms(x, g):
    return x * lax.rsqrt(jnp.mean(x * x, axis=-1, keepdims=True) + RMS_EPS) * g


def _norm_kernel(x_ref, g_ref, h_ref):
    h_ref[...] = _rms(x_ref[...], g_ref[...]).astype(BF16)


def _norm(x, g, bm):
    m = x.shape[0]
    return pl.pallas_call(
        _norm_kernel,
        grid=(m // bm,),
        in_specs=[pl.BlockSpec((bm, D_MODEL), lambda i: (i, 0)),
                  pl.BlockSpec((1, D_MODEL), lambda i: (0, 0))],
        out_specs=pl.BlockSpec((bm, D_MODEL), lambda i: (i, 0)),
        out_shape=jax.ShapeDtypeStruct((m, D_MODEL), BF16),
        compiler_params=_params(("parallel",), 10 * bm * D_MODEL * 4),
        name="pre_norm",
    )(x, g)


_NT = (((1,), (1,)), ((), ()))


def _in_proj_kernel(*refs, n_cast):
    h_ref, hm_ref, wt_ref, wl_ref = refs[:4]
    cast_in = refs[4:4 + n_cast]
    o_ref, om_ref, lg_ref, lgm_ref = refs[4 + n_cast:8 + n_cast]
    cast_out = refs[8 + n_cast:8 + 2 * n_cast]
    w_scr, wl_scr = refs[8 + 2 * n_cast:]
    first_tile = pl.program_id(1) == 0

    @pl.when(first_tile)
    def _():
        w_scr[...] = wt_ref[0].astype(BF16)
        om_ref[...] = lax.dot_general(hm_ref[...], w_scr[...], _NT, preferred_element_type=F32).astype(BF16)

    for r in range(0, o_ref.shape[0], MATMUL_ROWS):
        rs = slice(r, r + MATMUL_ROWS)
        o_ref[rs, :] = lax.dot_general(h_ref[rs, :], w_scr[...], _NT, preferred_element_type=F32).astype(BF16)

    for src_ref, dst_ref in zip(cast_in, cast_out):
        dst_ref[...] = src_ref[...].astype(BF16)

    @pl.when(pl.program_id(0) == 0)
    def _():
        @pl.when(first_tile)
        def _():
            wl_scr[...] = wl_ref[0].astype(BF16)
            lgm_ref[...] = lax.dot_general(hm_ref[...], wl_scr[...], _NT, preferred_element_type=F32)

        lg_ref[...] = lax.dot_general(h_ref[...], wl_scr[...], _NT, preferred_element_type=F32)


def _in_proj(h, hm, w_in_t, casts, layer, *, bm, bn):
    m, k = h.shape
    n_i = m // bm
    n_steps = (BIG_COLS // bn) * n_i

    def w_rows(j, i):
        skip = jnp.where(j >= COL_CB // bn, N_LOGITS // SUBLANES, 0)
        return layer, (j * (bn // SUBLANES) + skip) * SUBLANES, 0

    cast_in_specs, cast_out_specs, cast_shapes = [], [], []
    for wgt in casts:
        rows, cols = wgt.shape[1:]
        blk = -(-rows // n_steps)
        blk = -(-blk // HALO) * HALO
        assert rows % blk == 0, (rows, blk)
        last = rows // blk - 1
        cast_in_specs.append(pl.BlockSpec(
            (None, blk, cols), functools.partial(lambda j, i, last: (layer, jnp.minimum(j * n_i + i, last), 0), last=last)))
        cast_out_specs.append(pl.BlockSpec(
            (blk, cols), functools.partial(lambda j, i, last: (jnp.minimum(j * n_i + i, last), 0), last=last)))
        cast_shapes.append(jax.ShapeDtypeStruct((rows, cols), BF16))

    est = (2 * (bm * k * 2 + k * bn * 4 + k * LANES * 4 + bm * bn * 2 + bm * LANES * 4) + k * bn * 2
           + 2 * bm * bn * 4 + (6 << 20))
    outs = pl.pallas_call(
        functools.partial(_in_proj_kernel, n_cast=len(casts)),
        grid=(BIG_COLS // bn, n_i),
        in_specs=[pl.BlockSpec((bm, k), lambda j, i: (i, 0)),
                  pl.BlockSpec((META_ROWS, k), lambda j, i: (0, 0)),
                  pl.BlockSpec((pl.Element(1), pl.Element(bn), pl.Element(k)), w_rows),
                  pl.BlockSpec((pl.Element(1), pl.Element(LANES), pl.Element(k)), lambda j, i: (layer, COL_CB, 0))]
        + cast_in_specs,
        out_specs=[pl.BlockSpec((bm, bn), lambda j, i: (i, j)),
                   pl.BlockSpec((META_ROWS, bn), lambda j, i: (0, j)),
                   pl.BlockSpec((bm, LANES), lambda j, i: (jnp.where(j == 0, i, n_i - 1), 0)),
                   pl.BlockSpec((META_ROWS, LANES), lambda j, i: (0, 0))] + cast_out_specs,
        out_shape=[jax.ShapeDtypeStruct((m, BIG_COLS), BF16),
                   jax.ShapeDtypeStruct((META_ROWS, BIG_COLS), BF16),
                   jax.ShapeDtypeStruct((m, LANES), F32),
                   jax.ShapeDtypeStruct((META_ROWS, LANES), F32)] + cast_shapes,
        scratch_shapes=[pltpu.VMEM((bn, k), BF16), pltpu.VMEM((LANES, k), BF16)],
        compiler_params=_params(("arbitrary", "arbitrary"), est),
        name="in_proj",
    )(h, hm, w_in_t, w_in_t, *casts)
    return outs[:4], outs[4:]


def _shift_rows(x, d):
    return pltpu.roll(x, d, axis=0)


def _cumsum_rows(x):
    rows = x.shape[0]
    row = lax.broadcasted_iota(jnp.int32, x.shape, 0)
    d = 1
    while d < rows:
        x = x + jnp.where(row >= d, _shift_rows(x, d), 0.0)
        d *= 2
    return x


def _cummax_rows(x):
    rows = x.shape[0]
    row = lax.broadcasted_iota(jnp.int32, x.shape, 0)
    d = 1
    while d < rows:
        x = jnp.maximum(x, jnp.where(row >= d, _shift_rows(x, d), -jnp.inf))
        d *= 2
    return x


def _log_sigmoid(x):
    return jnp.minimum(x, 0.0) - jnp.log1p(jnp.exp(-jnp.abs(x)))


def _mlstm_chunk(streams, bias_ref, gain_ref, *, chunk, n_valid):
    scale = HEAD_DIM ** -0.5
    last = n_valid - 1
    row1 = lax.broadcasted_iota(jnp.int32, (chunk, LANES), 0)
    gates = []
    for q_ref, k_ref, v_ref, o_ref, gl_ref, y_ref, c_scr, m_scr, s_scr, r_scr in streams:
        gl = gl_ref[...] + bias_ref[...]
        logf = pltpu.roll(_log_sigmoid(gl), LANES - HEADS, axis=1)
        b = _cumsum_rows(logf)
        a = gl - b
        m_prev = m_scr[...]
        g = jnp.maximum(_cummax_rows(a), m_prev)
        inter_w = jnp.exp(m_prev - g) * scale
        exp_neg_m = jnp.exp(-(b + g))
        g_last = g[last:last + 1, :]
        decay = jnp.exp(m_prev - g_last)
        ws = jnp.where(row1 <= last, jnp.exp(a - g_last), 0.0)
        m_scr[...] = b[last:last + 1, :] + g_last
        a_t = (a + math.log(scale)).T
        gates.append((g, a_t, inter_w, exp_neg_m, ws, decay))

    rows = lax.broadcasted_iota(jnp.int32, (chunk, chunk), 0)
    cols = lax.broadcasted_iota(jnp.int32, (chunk, chunk), 1)
    causal = rows >= cols
    ones = jnp.ones((chunk, HEAD_DIM), BF16)
    head = [slice(h * HEAD_DIM, (h + 1) * HEAD_DIM) for h in range(HEADS)]
    wide = [slice(h * 2 * HEAD_DIM, (h + 1) * 2 * HEAD_DIM) for h in range(HEADS)]
    work = [(h, st, gt) for h in range(HEADS) for st, gt in zip(streams, gates)]

    for h, (q_ref, k_ref, _, _, _, _, _, _, s_scr, _), (g, a_t, _, _, _, _) in work:
        w = jnp.where(causal, jnp.exp(a_t[h:h + 1, :] - g[:, h:h + 1]), 0.0)
        s = lax.dot_general(q_ref[:, head[h]], k_ref[:, head[h]], _NT, preferred_element_type=F32)
        s_scr[h, 0:chunk, 0:chunk] = (s * w).astype(BF16)
    for h, (q_ref, _, v_ref, _, _, _, c_scr, _, s_scr, r_scr), (_, _, inter_w, _, _, _) in work:
        v_aug = jnp.concatenate([v_ref[:, head[h]], ones], axis=1)
        r_scr[0:chunk, wide[h]] = (
            jnp.dot(q_ref[:, head[h]], c_scr[h].astype(BF16), preferred_element_type=F32)
            * inter_w[:, h:h + 1]
            + jnp.dot(s_scr[h, 0:chunk, 0:chunk], v_aug, preferred_element_type=F32))
    for h, (_, k_ref, v_ref, _, _, _, c_scr, _, _, _), (_, _, _, _, ws, decay) in work:
        v_aug = jnp.concatenate([v_ref[:, head[h]], ones], axis=1)
        kw = (k_ref[:, head[h]].astype(F32) * ws[:, h:h + 1]).astype(BF16)
        upd = lax.dot_general(kw, v_aug, (((0,), (0,)), ((), ())), preferred_element_type=F32)
        c_scr[h] = decay[:, h:h + 1] * c_scr[h] + upd
    for h, (_, _, _, o_ref, _, y_ref, _, _, _, r_scr), (_, _, _, exp_neg_m, _, _) in work:
        r = r_scr[0:chunk, wide[h]]
        num, den = r[:, :HEAD_DIM], r[:, HEAD_DIM:]
        hh = num / jnp.maximum(jnp.abs(den), exp_neg_m[:, h:h + 1])
        hn = hh * lax.rsqrt(jnp.mean(hh * hh, axis=-1, keepdims=True) + RMS_EPS)
        y_ref[:, head[h]] = (hn * gain_ref[:, head[h]]
                             * jax.nn.sigmoid(o_ref[:, head[h]].astype(F32))).astype(BF16)


def _mlstm_kernel(q_ref, k_ref, v_ref, o_ref, gl_ref, qm_ref, km_ref, vm_ref, om_ref, glm_ref,
                  bias_ref, gain_ref, y_ref, ym_ref, c_scr, m_scr, c0_scr, m0_scr, s_scr, r_scr, *, group, chunk):
    first_chunk = pl.program_id(1) == 0

    @pl.when(jnp.logical_and(pl.program_id(0) == 0, first_chunk))
    def _():
        c0_scr[...] = jnp.zeros_like(c0_scr)
        m0_scr[...] = jnp.zeros_like(m0_scr)
        meta = (qm_ref, km_ref, vm_ref, om_ref, glm_ref, ym_ref, c0_scr, m0_scr, s_scr.at[0], r_scr.at[0])
        _mlstm_chunk([meta], bias_ref, gain_ref, chunk=META_ROWS, n_valid=N_META)

    @pl.when(first_chunk)
    def _():
        for bb in range(group):
            c_scr[bb] = c0_scr[...]
            m_scr[bb] = m0_scr[...]

    streams = [(q_ref.at[bb], k_ref.at[bb], v_ref.at[bb], o_ref.at[bb], gl_ref.at[bb], y_ref.at[bb],
                c_scr.at[bb], m_scr.at[bb], s_scr.at[bb], r_scr.at[bb]) for bb in range(group)]
    _mlstm_chunk(streams, bias_ref, gain_ref, chunk=chunk, n_valid=chunk)


def _mlstm(big, big_m, logits, logits_m, bias_row, gain, *, batches, group, chunk):
    m = big.shape[0]
    seq = m // batches
    big3 = big.reshape(batches, seq, BIG_COLS)
    logits3 = logits.reshape(batches, seq, LANES)
    cblk = [COL_Q // MIX_WIDTH, COL_K // MIX_WIDTH, COL_V // MIX_WIDTH, COL_O // MIX_WIDTH]
    main = [pl.BlockSpec((group, chunk, MIX_WIDTH), functools.partial(lambda b, c, cb: (b, c, cb), cb=cb))
            for cb in cblk]
    meta = [pl.BlockSpec((META_ROWS, MIX_WIDTH), functools.partial(lambda b, c, cb: (0, cb), cb=cb))
            for cb in cblk]
    in_specs = (main + [pl.BlockSpec((group, chunk, LANES), lambda b, c: (b, c, 0))]
                + meta + [pl.BlockSpec((META_ROWS, LANES), lambda b, c: (0, 0)),
                          pl.BlockSpec((1, LANES), lambda b, c: (0, 0)),
                          pl.BlockSpec((1, MIX_WIDTH), lambda b, c: (0, 0))])
    c_shape = (HEADS, HEAD_DIM, 2 * HEAD_DIM)
    est = (group * (12 * chunk * MIX_WIDTH * 2 + HEADS * chunk * chunk * 2 + chunk * HEADS * 2 * HEAD_DIM * 4
                    + HEADS * HEAD_DIM * 2 * HEAD_DIM * 4 + 24 * chunk * LANES * 4) + (16 << 20))
    y, y_m = pl.pallas_call(
        functools.partial(_mlstm_kernel, group=group, chunk=chunk),
        grid=(batches // group, seq // chunk),
        in_specs=in_specs,
        out_specs=[pl.BlockSpec((group, chunk, MIX_WIDTH), lambda b, c: (b, c, 0)),
                   pl.BlockSpec((META_ROWS, MIX_WIDTH), lambda b, c: (0, 0))],
        out_shape=[jax.ShapeDtypeStruct((batches, seq, MIX_WIDTH), BF16),
                   jax.ShapeDtypeStruct((META_ROWS, MIX_WIDTH), BF16)],
        scratch_shapes=[pltpu.VMEM((group,) + c_shape, F32), pltpu.VMEM((group, 1, LANES), F32),
                        pltpu.VMEM(c_shape, F32), pltpu.VMEM((1, LANES), F32),
                        pltpu.VMEM((group, HEADS, chunk, chunk), BF16),
                        pltpu.VMEM((group, chunk, HEADS * 2 * HEAD_DIM), F32)],
        compiler_params=_params(("arbitrary", "arbitrary"), est),
        name="mlstm",
    )(big3, big3, big3, big3, logits3, big_m, big_m, big_m, big_m, logits_m, bias_row, gain)
    return y.reshape(m, MIX_WIDTH), y_m


POOL_BLOCK = 128
MERGE_COLS = 512
CONV_ROWS, CONV_COLS = 128, 256


def _pool_bands():
    bands = np.zeros((len(POOL_WINDOWS), POOL_BLOCK, 2 * POOL_BLOCK), np.float32)
    for grp, win in enumerate(POOL_WINDOWS):
        for r in range(POOL_BLOCK):
            for lag in range(win):
                bands[grp, r, POOL_BLOCK + r - lag] += 1.0 / win
            bands[grp, r, POOL_BLOCK + r] -= 1.0
    return jnp.asarray(bands, BF16)


def _conv3(cb_ref, cc_ref, cx_ref, halo, cw_ref, yb_ref, p_scr, *, rows):
    p_scr[0:HALO, :] = halo
    pieces = [(r, c) for r in range(0, rows, CONV_ROWS) for c in range(0, MIX_WIDTH, CONV_COLS)]
    for r, c in pieces:
        rs, cs = slice(r, r + CONV_ROWS), slice(c, c + CONV_COLS)
        p_scr[HALO + r:HALO + r + CONV_ROWS, cs] = cc_ref[rs, cs].astype(F32) * cx_ref[rs, cs].astype(F32)
    for r, c in pieces:
        rs, cs = slice(r, r + CONV_ROWS), slice(c, c + CONV_COLS)
        conv = cw_ref[2:3, cs] * p_scr[HALO + r:HALO + r + CONV_ROWS, cs]
        conv = conv + cw_ref[1:2, cs] * p_scr[HALO + r - 1:HALO + r - 1 + CONV_ROWS, cs]
        conv = conv + cw_ref[0:1, cs] * p_scr[HALO + r - 2:HALO + r - 2 + CONV_ROWS, cs]
        yb_ref[rs, cs] = (cb_ref[rs, cs].astype(F32) * conv).astype(BF16)


def _pool_from_start(pu_ref, pw_scr, ps_ref, yc_ref, u_scr, *, rows):
    u_scr[0:HALO, :] = jnp.zeros((HALO, MIX_WIDTH), F32)
    u_scr[HALO:HALO + rows, :] = pu_ref[...].astype(F32)
    t = lax.broadcasted_iota(jnp.int32, (rows, 1), 0)
    for grp, win in enumerate(POOL_WINDOWS):
        sl = slice(grp * POOL_GROUP, (grp + 1) * POOL_GROUP)
        cur = u_scr[HALO:HALO + rows, sl]
        tot = cur
        for lag in range(1, win):
            tot = tot + u_scr[HALO - lag:HALO - lag + rows, sl]
        pooled = tot / jnp.minimum(t + 1, win).astype(F32) - cur
        y = jnp.dot(pooled.astype(BF16), pw_scr[grp], preferred_element_type=F32)
        yc_ref[:, sl] = (y * ps_ref[:, sl]).astype(BF16)


def _pool_banded(band_ref, pw_scr, ps_ref, yc_ref, ub_scr, *, rows):
    for grp in range(len(POOL_WINDOWS)):
        sl = slice(grp * POOL_GROUP, (grp + 1) * POOL_GROUP)
        pooled = jnp.concatenate(
            [jnp.dot(band_ref[grp], ub_scr[r:r + 2 * POOL_BLOCK, sl], preferred_element_type=F32)
             for r in range(0, rows, POOL_BLOCK)], axis=0)
        y = jnp.dot(pooled.astype(BF16), pw_scr[grp], preferred_element_type=F32)
        yc_ref[:, sl] = (y * ps_ref[:, sl]).astype(BF16)


def _branch_term(y_ref, w_ref, n, g_ref, cs):
    return (jax.nn.sigmoid(g_ref[:, cs].astype(F32))
            * jnp.dot(y_ref[...], w_ref[n, :, cs], preferred_element_type=F32))


def _merge(ya_ref, yb_ref, yc_ref, w_ref, g0_ref, g1_ref, g2_ref, o_ref):
    for c in range(0, o_ref.shape[1], MERGE_COLS):
        cs = slice(c, c + MERGE_COLS)
        acc = _branch_term(ya_ref, w_ref, 0, g0_ref, cs)
        acc = acc + _branch_term(yb_ref, w_ref, 1, g1_ref, cs)
        o_ref[:, cs] = (acc + _branch_term(yc_ref, w_ref, 2, g2_ref, cs)).astype(BF16)


def _mix_merge_kernel(cb_ref, cc_ref, cx_ref, pu_ref, ccp_ref, cxp_ref, pup_ref, ya_ref, g0_ref, g1_ref, g2_ref,
                      cbm_ref, ccm_ref, cxm_ref, pum_ref, yam_ref, g0m_ref, g1m_ref, g2m_ref,
                      cw_ref, pw_ref, ps_ref, band_ref, w_ref, o_ref, om_ref,
                      pw_scr, p_scr, u_scr, ub_scr, yb_scr, yc_scr, ybm_scr, ycm_scr, *, bm, tiles_per_batch):
    @pl.when(pl.program_id(0) == 0)
    def _():
        pw_scr[...] = pw_ref[...].astype(BF16)
        _conv3(cbm_ref, ccm_ref, cxm_ref, jnp.zeros((HALO, MIX_WIDTH), F32), cw_ref, ybm_scr, p_scr,
               rows=META_ROWS)
        _pool_from_start(pum_ref, pw_scr, ps_ref, ycm_scr, u_scr, rows=META_ROWS)
        _merge(yam_ref, ybm_scr, ycm_scr, w_ref, g0m_ref, g1m_ref, g2m_ref, om_ref)

    first = pl.program_id(0) % tiles_per_batch == 0
    cch = jnp.where(first, ccm_ref[0:HALO, :], ccp_ref[...]).astype(F32)
    cxh = jnp.where(first, cxm_ref[0:HALO, :], cxp_ref[...]).astype(F32)
    _conv3(cb_ref, cc_ref, cx_ref, cch * cxh, cw_ref, yb_scr, p_scr, rows=bm)
    meta_tail = jnp.concatenate([jnp.zeros((POOL_BLOCK - N_META, MIX_WIDTH), BF16), pum_ref[0:N_META, :]], axis=0)
    ub_scr[0:POOL_BLOCK, :] = jnp.where(first, meta_tail, pup_ref[...])
    ub_scr[POOL_BLOCK:POOL_BLOCK + bm, :] = pu_ref[...]
    _pool_banded(band_ref, pw_scr, ps_ref, yc_scr, ub_scr, rows=bm)
    _merge(ya_ref, yb_scr, yc_scr, w_ref, g0_ref, g1_ref, g2_ref, o_ref)


def _mix_merge(big, big_m, ya, ya_m, conv_w, pool_w, pool_scale, w_branch, layer, *, bm, tiles_per_batch):
    m = big.shape[0]
    cblk = [COL_CB // MIX_WIDTH, COL_CC // MIX_WIDTH, COL_CX // MIX_WIDTH, COL_PU // MIX_WIDTH]
    gblk = [(COL_GATES + n * D_MODEL) // D_MODEL for n in range(N_BRANCH)]

    def prev_rows(rows, c):
        per = bm // rows
        return pl.BlockSpec((rows, MIX_WIDTH), lambda i: (jnp.maximum(i * per - 1, 0), c))

    def rows_spec(rows, width, c, meta):
        return pl.BlockSpec((rows, width), (lambda i: (0, c)) if meta else (lambda i: (i, c)))

    in_specs = [rows_spec(bm, MIX_WIDTH, c, False) for c in cblk]
    in_specs += [prev_rows(HALO, cblk[1]), prev_rows(HALO, cblk[2]), prev_rows(POOL_BLOCK, cblk[3])]
    in_specs += [rows_spec(bm, MIX_WIDTH, 0, False)] + [rows_spec(bm, D_MODEL, c, False) for c in gblk]
    in_specs += [rows_spec(META_ROWS, MIX_WIDTH, c, True) for c in cblk]
    in_specs += [rows_spec(META_ROWS, MIX_WIDTH, 0, True)] + [rows_spec(META_ROWS, D_MODEL, c, True) for c in gblk]
    in_specs += [pl.BlockSpec((None, CONV_WIDTH, MIX_WIDTH), lambda i: (layer, 0, 0)),
                 pl.BlockSpec((None, len(POOL_WINDOWS), POOL_GROUP, POOL_GROUP), lambda i: (layer, 0, 0, 0)),
                 pl.BlockSpec((1, MIX_WIDTH), lambda i: (0, 0)),
                 pl.BlockSpec((len(POOL_WINDOWS), POOL_BLOCK, 2 * POOL_BLOCK), lambda i: (0, 0, 0)),
                 pl.BlockSpec((N_BRANCH, MIX_WIDTH, D_MODEL), lambda i: (0, 0, 0), pipeline_mode=pl.Buffered(1))]
    est = (N_BRANCH * MIX_WIDTH * D_MODEL * 2 + 2 * bm * (5 * MIX_WIDTH + 4 * D_MODEL) * 2
           + 4 * bm * D_MODEL * 4 + 6 * bm * MIX_WIDTH * 4 + (10 << 20))
    return pl.pallas_call(
        functools.partial(_mix_merge_kernel, bm=bm, tiles_per_batch=tiles_per_batch),
        grid=(m // bm,),
        in_specs=in_specs,
        out_specs=[pl.BlockSpec((bm, D_MODEL), lambda i: (i, 0)),
                   pl.BlockSpec((META_ROWS, D_MODEL), lambda i: (0, 0))],
        out_shape=[jax.ShapeDtypeStruct((m, D_MODEL), BF16), jax.ShapeDtypeStruct((META_ROWS, D_MODEL), BF16)],
        scratch_shapes=[pltpu.VMEM((len(POOL_WINDOWS), POOL_GROUP, POOL_GROUP), BF16),
                        pltpu.VMEM((HALO + max(bm, META_ROWS), MIX_WIDTH), F32),
                        pltpu.VMEM((HALO + META_ROWS, MIX_WIDTH), F32),
                        pltpu.VMEM((POOL_BLOCK + bm, MIX_WIDTH), BF16),
                        pltpu.VMEM((bm, MIX_WIDTH), BF16), pltpu.VMEM((bm, MIX_WIDTH), BF16),
                        pltpu.VMEM((META_ROWS, MIX_WIDTH), BF16), pltpu.VMEM((META_ROWS, MIX_WIDTH), BF16)],
        compiler_params=_params(("arbitrary",), est),
        name="mix_merge",
    )(big, big, big, big, big, big, big, ya, big, big, big,
      big_m, big_m, big_m, big_m, ya_m, big_m, big_m, big_m,
      conv_w, pool_w, pool_scale, _pool_bands(), w_branch)


FINISH_ROWS = 64


def _finish_rows(y_scr, slot, x_ref, gpost_ref, gnext_ref, xo_ref, ho_ref, *, row0, rows):
    n_n = y_scr.shape[1]
    for s in range(0, rows, FINISH_ROWS):
        if isinstance(row0, int):
            ys = slice(row0 + s, row0 + s + FINISH_ROWS)
        else:
            ys = pl.ds(pl.multiple_of(row0 + s, FINISH_ROWS), FINISH_ROWS)
        y = jnp.concatenate([y_scr[slot, c, ys, :] for c in range(n_n)], axis=1)
        xn = x_ref[s:s + FINISH_ROWS, :] + _rms(y, gpost_ref[...])
        xo_ref[s:s + FINISH_ROWS, :] = xn
        if ho_ref is not None:
            ho_ref[s:s + FINISH_ROWS, :] = _rms(xn, gnext_ref[...]).astype(BF16)


def _proj_residual_kernel(*refs, n_i, n_n, part, emit_h, with_meta):
    a_ref, w_ref, x_ref, gpost_ref = refs[:4]
    refs = refs[4:]
    gnext_ref = None
    if emit_h:
        gnext_ref = refs[0]
        refs = refs[1:]
    if with_meta:
        am_ref, xm_ref = refs[:2]
        refs = refs[2:]
    xo_ref = refs[0]
    refs = refs[1:]
    ho_ref = None
    if emit_h:
        ho_ref = refs[0]
        refs = refs[1:]
    if with_meta:
        xmo_ref = refs[0]
        refs = refs[1:]
        hmo_ref = None
        if emit_h:
            hmo_ref = refs[0]
            refs = refs[1:]
        y_scr, ym_scr = refs
    else:
        y_scr, = refs
    i = pl.program_id(0)
    n = pl.program_id(1) if n_n > 1 else 0

    def step(slot, do_matmul, do_finish):
        if do_finish:
            _finish_rows(y_scr, 1 - slot, x_ref, gpost_ref, gnext_ref, xo_ref, ho_ref, row0=n * part, rows=part)
        if do_matmul:
            y_scr[slot, n] = jnp.dot(a_ref[...], w_ref[...], preferred_element_type=F32)

    @pl.when(i == 0)
    def _():
        if with_meta:
            ym_scr[0, n] = jnp.dot(am_ref[...], w_ref[...], preferred_element_type=F32)

            def finish_meta():
                _finish_rows(ym_scr, 0, xm_ref, gpost_ref, gnext_ref, xmo_ref, hmo_ref, row0=0, rows=META_ROWS)
            if n_n > 1:
                pl.when(n == n_n - 1)(finish_meta)
            else:
                finish_meta()
        step(0, True, False)

    for parity in (0, 1):
        @pl.when(jnp.logical_and(jnp.logical_and(i > 0, i < n_i), lax.rem(i, 2) == parity))
        def _():
            step(parity, True, True)

    @pl.when(i == n_i)
    def _():
        step(n_i % 2, False, True)


def _proj_residual(a, a_m, w, x, x_m, g_post, g_next, *, bm, bn, name):
    m, kdim = a.shape
    n_i = m // bm
    n_n = D_MODEL // bn
    part = bm // n_n
    emit_h = g_next is not None
    with_meta = a_m is not None
    const = lambda i, n: (0, 0)
    prev_part = lambda i, n: (jnp.where(i == 0, 0, (i - 1) * n_n + n), 0)
    vec = pl.BlockSpec((1, D_MODEL), const)
    w_mode = dict(pipeline_mode=pl.Buffered(1)) if n_n == 1 else {}
    in_specs = [pl.BlockSpec((bm, kdim), lambda i, n: (jnp.minimum(i, n_i - 1), 0)),
                pl.BlockSpec((kdim, bn), lambda i, n: (0, n), **w_mode),
                pl.BlockSpec((part, D_MODEL), prev_part), vec]
    args = [a, w, x, g_post]
    if emit_h:
        in_specs.append(vec)
        args.append(g_next)
    if with_meta:
        in_specs += [pl.BlockSpec((META_ROWS, kdim), const), pl.BlockSpec((META_ROWS, D_MODEL), const)]
        args += [a_m, x_m]
    out_specs = [pl.BlockSpec((part, D_MODEL), prev_part)]
    out_shape = [jax.ShapeDtypeStruct((m, D_MODEL), F32)]
    if emit_h:
        out_specs.append(pl.BlockSpec((part, D_MODEL), prev_part))
        out_shape.append(jax.ShapeDtypeStruct((m, D_MODEL), BF16))
    scratch = [pltpu.VMEM((2, n_n, bm, bn), F32)]
    if with_meta:
        out_specs.append(pl.BlockSpec((META_ROWS, D_MODEL), const))
        out_shape.append(jax.ShapeDtypeStruct((META_ROWS, D_MODEL), F32))
        if emit_h:
            out_specs.append(pl.BlockSpec((META_ROWS, D_MODEL), const))
            out_shape.append(jax.ShapeDtypeStruct((META_ROWS, D_MODEL), BF16))
        scratch.append(pltpu.VMEM((1, n_n, META_ROWS, bn), F32))
    w_bufs = 1 if n_n == 1 else 2
    est = (2 * (bm * kdim * 2 + part * D_MODEL * (4 + 4 + 2)) + w_bufs * kdim * bn * 2 + 2 * bm * D_MODEL * 4
           + 2 * bm * bn * 4 + 2 * META_ROWS * (kdim * 2 + 3 * D_MODEL * 4) + (4 << 20))
    outs = pl.pallas_call(
        functools.partial(_proj_residual_kernel, n_i=n_i, n_n=n_n, part=part, emit_h=emit_h, with_meta=with_meta),
        grid=(n_i + 1, n_n),
        in_specs=in_specs,
        out_specs=out_specs,
        out_shape=out_shape,
        scratch_shapes=scratch,
        compiler_params=_params(("arbitrary", "arbitrary"), est),
        name=name,
    )(*args)
    outs = list(outs)
    x_new = outs.pop(0)
    h_new = outs.pop(0) if emit_h else None
    xm_new = outs.pop(0) if with_meta else None
    hm_new = outs.pop(0) if (with_meta and emit_h) else None
    return x_new, h_new, xm_new, hm_new


def _gelu_tanh(x):
    return 0.5 * x * (1.0 + jnp.tanh(0.7978845608028654 * (x + 0.044715 * (x * x * x))))


def _ffn_in_tile(h_ref, wa_scr, wu_scr, cw_ref, halo, o_ref, a_scr, *, rows):
    cols = o_ref.shape[1]
    a_scr[0:SUBLANES, :] = jnp.zeros((SUBLANES, cols), F32) if halo is None else halo
    for r in range(0, rows, MATMUL_ROWS):
        n = min(MATMUL_ROWS, rows - r)
        h = h_ref[r:r + n, :]
        a = jnp.dot(h, wa_scr[...], preferred_element_type=F32)
        u = jnp.dot(h, wu_scr[...], preferred_element_type=F32)
        a_scr[SUBLANES + r:SUBLANES + r + n, :] = a
        conv = cw_ref[2:3, :] * a
        conv = conv + cw_ref[1:2, :] * a_scr[SUBLANES + r - 1:SUBLANES + r - 1 + n, :]
        conv = conv + cw_ref[0:1, :] * a_scr[SUBLANES + r - 2:SUBLANES + r - 2 + n, :]
        o_ref[r:r + n, :] = (_gelu_tanh(conv) * u).astype(BF16)


def _ffn_in_kernel(h_ref, hm_ref, wa_ref, wu_ref, cw_ref, o_ref, om_ref,
                   wa_scr, wu_scr, a_scr, tail_scr, mtail_scr, *, bm, tiles_per_batch):
    @pl.when(pl.program_id(1) == 0)
    def _():
        wa_scr[...] = wa_ref[...].astype(BF16)
        wu_scr[...] = wu_ref[...].astype(BF16)
        _ffn_in_tile(hm_ref, wa_scr, wu_scr, cw_ref, None, om_ref, a_scr, rows=META_ROWS)
        mtail_scr[...] = a_scr[N_META:N_META + SUBLANES, :]

    first = pl.program_id(1) % tiles_per_batch == 0
    halo = jnp.where(first, mtail_scr[...], tail_scr[...])
    _ffn_in_tile(h_ref, wa_scr, wu_scr, cw_ref, halo, o_ref, a_scr, rows=bm)
    tail_scr[...] = a_scr[bm:bm + SUBLANES, :]


def _ffn_in(h, hm, w_ffn_in, conv_w, layer, *, bm, bn, tiles_per_batch):
    m = h.shape[0]
    n_j = D_FF // bn
    est = (2 * (bm * D_MODEL * 2 + 2 * D_MODEL * bn * 4 + bm * bn * 2) + 2 * D_MODEL * bn * 2
           + 9 * bm * bn * 4 + (4 << 20))
    return pl.pallas_call(
        functools.partial(_ffn_in_kernel, bm=bm, tiles_per_batch=tiles_per_batch),
        grid=(n_j, m // bm),
        in_specs=[pl.BlockSpec((bm, D_MODEL), lambda j, i: (i, 0)),
                  pl.BlockSpec((META_ROWS, D_MODEL), lambda j, i: (0, 0)),
                  pl.BlockSpec((None, D_MODEL, bn), lambda j, i: (layer, 0, j)),
                  pl.BlockSpec((None, D_MODEL, bn), lambda j, i: (layer, 0, j + n_j)),
                  pl.BlockSpec((None, CONV_WIDTH, bn), lambda j, i: (layer, 0, j))],
        out_specs=[pl.BlockSpec((bm, bn), lambda j, i: (i, j)),
                   pl.BlockSpec((META_ROWS, bn), lambda j, i: (0, j))],
        out_shape=[jax.ShapeDtypeStruct((m, D_FF), BF16), jax.ShapeDtypeStruct((META_ROWS, D_FF), BF16)],
        scratch_shapes=[pltpu.VMEM((D_MODEL, bn), BF16), pltpu.VMEM((D_MODEL, bn), BF16),
                        pltpu.VMEM((SUBLANES + bm, bn), F32),
                        pltpu.VMEM((SUBLANES, bn), F32), pltpu.VMEM((SUBLANES, bn), F32)],
        compiler_params=_params(("arbitrary", "arbitrary"), est),
        name="ffn_in",
    )(h, hm, w_ffn_in, w_ffn_in, conv_w)


def kernel(x, meta_tokens, norm_pre_mix, norm_post_mix, norm_pre_ffn, norm_post_ffn, w_in, b_if,
           mlstm_head_gain, conv_mix_w, pool_w, pool_scale, w_branch, w_out, w_ffn_in, ffn_conv_w,
           w_ffn_out):
    batch, seq, _ = x.shape
    depth = w_in.shape[0]
    m = batch * seq
    xr = x.reshape(m, D_MODEL)
    xm = jnp.pad(meta_tokens.astype(F32), ((0, META_ROWS - N_META), (0, 0)))

    side_casts = [w_branch.astype(F32).reshape(depth, N_BRANCH * MIX_WIDTH, D_MODEL),
                  w_out.astype(F32), w_ffn_out.astype(F32)]
    w_ffn_in, pool_w = (t.astype(F32) for t in (w_ffn_in, pool_w))
    w_in_t = jnp.swapaxes(w_in.astype(F32), 1, 2)
    cw = conv_mix_w.astype(F32)
    fcw = ffn_conv_w.astype(F32)

    def vec(v):
        return v.reshape(1, -1).astype(F32)

    hr = _norm(xr, vec(norm_pre_mix[0]), 512)
    hm = _norm(xm, vec(norm_pre_mix[0]), META_ROWS)
    for l in range(depth):
        bias_row = jnp.pad(b_if[l].reshape(1, N_LOGITS).astype(F32), ((0, 0), (0, LANES - N_LOGITS)))
        gain = vec(mlstm_head_gain[l])
        ps = vec(pool_scale[l])
        g_post_mix, g_pre_ffn, g_post_ffn = vec(norm_post_mix[l]), vec(norm_pre_ffn[l]), vec(norm_post_ffn[l])
        last_layer = l + 1 == depth
        g_next = None if last_layer else vec(norm_pre_mix[l + 1])

        (big, big_m, logits, logits_m), (wbr, wo, wfo) = _in_proj(hr, hm, w_in_t, side_casts, l, bm=2048, bn=1024)
        wbr = wbr.reshape(N_BRANCH, MIX_WIDTH, D_MODEL)
        ya, ya_m = _mlstm(big, big_m, logits, logits_m, bias_row, gain, batches=batch, group=MLSTM_GROUP,
                          chunk=MLSTM_CHUNK)
        mg, mg_m = _mix_merge(big, big_m, ya, ya_m, cw, pool_w, ps, wbr, l, bm=256, tiles_per_batch=seq // 256)
        xr, hf, xm, hf_m = _proj_residual(mg, mg_m, wo, xr, xm, g_post_mix, g_pre_ffn,
                                          bm=512, bn=D_MODEL, name="out_proj")
        act, act_m = _ffn_in(hf, hf_m, w_ffn_in, fcw, l, bm=2048, bn=512, tiles_per_batch=seq // 2048)
        xr, hr, xm, hm = _proj_residual(act, None if last_layer else act_m, wfo, xr, xm, g_post_ffn, g_next,
                                        bm=256, bn=D_MODEL, name="ffn_out")
    return xr.reshape(batch, seq, D_MODEL)
```

```python
import functools
import math

import jax
import jax.numpy as jnp
import numpy as np
from jax import lax
from jax.experimental import pallas as pl
from jax.experimental.pallas import tpu as pltpu

D_MODEL = 2048
N_META = 16
MIX_WIDTH = 1024
N_BRANCH = 3
HEADS = 8
HEAD_DIM = MIX_WIDTH // HEADS
POOL_WINDOWS = (2, 4, 8, 16)
POOL_GROUP = MIX_WIDTH // len(POOL_WINDOWS)
D_FF = 5632
RMS_EPS = 1e-6
CONV_WIDTH = 3

COL_Q, COL_K, COL_V, COL_O, COL_CB, COL_CC, COL_CX, COL_PU = (n * MIX_WIDTH for n in range(8))
COL_GATES = 8 * MIX_WIDTH
BIG_COLS = COL_GATES + N_BRANCH * D_MODEL
N_LOGITS = 2 * HEADS
LANES = 128
HALO = 16
SUBLANES = 8
META_ROWS = 128
MLSTM_CHUNK = 512
MLSTM_GROUP = 1
VMEM_LIMIT_CAP = 60 * 1024 * 1024
MATMUL_ROWS = 1024

F32 = jnp.float32
BF16 = jnp.bfloat16


def _params(semantics, vmem_bytes):
    return pltpu.CompilerParams(dimension_semantics=semantics,
                                vmem_limit_bytes=min(int(vmem_bytes), VMEM_LIMIT_CAP))


def _rms(x, g):
    return x * lax.rsqrt(jnp.mean(x * x, axis=-1, keepdims=True) + RMS_EPS) * g


def _norm_kernel(x_ref, g_ref, h_ref):
    h_ref[...] = _rms(x_ref[...], g_ref[...]).astype(BF16)


def _norm(x, g, bm):
    m = x.shape[0]
    return pl.pallas_call(
        _norm_kernel,
        grid=(m // bm,),
        in_specs=[pl.BlockSpec((bm, D_MODEL), lambda i: (i, 0)),
                  pl.BlockSpec((1, D_MODEL), lambda i: (0, 0))],
        out_specs=pl.BlockSpec((bm, D_MODEL), lambda i: (i, 0)),
        out_shape=jax.ShapeDtypeStruct((m, D_MODEL), BF16),
        compiler_params=_params(("parallel",), 10 * bm * D_MODEL * 4),
        name="pre_norm",
    )(x, g)


_NT = (((1,), (1,)), ((), ()))


def _in_proj_kernel(*refs, n_cast):
    h_ref, hm_ref, wt_ref, wl_ref = refs[:4]
    cast_in = refs[4:4 + n_cast]
    o_ref, om_ref, lg_ref, lgm_ref = refs[4 + n_cast:8 + n_cast]
    cast_out = refs[8 + n_cast:8 + 2 * n_cast]
    w_scr, wl_scr = refs[8 + 2 * n_cast:]
    first_tile = pl.program_id(1) == 0

    @pl.when(first_tile)
    def _():
        w_scr[...] = wt_ref[0].astype(BF16)
        om_ref[...] = lax.dot_general(hm_ref[...], w_scr[...], _NT, preferred_element_type=F32).astype(BF16)

    for r in range(0, o_ref.shape[0], MATMUL_ROWS):
        rs = slice(r, r + MATMUL_ROWS)
        o_ref[rs, :] = lax.dot_general(h_ref[rs, :], w_scr[...], _NT, preferred_element_type=F32).astype(BF16)

    for src_ref, dst_ref in zip(cast_in, cast_out):
        dst_ref[...] = src_ref[...].astype(BF16)

    @pl.when(pl.program_id(0) == 0)
    def _():
        @pl.when(first_tile)
        def _():
            wl_scr[...] = wl_ref[0].astype(BF16)
            lgm_ref[...] = lax.dot_general(hm_ref[...], wl_scr[...], _NT, preferred_element_type=F32)

        lg_ref[...] = lax.dot_general(h_ref[...], wl_scr[...], _NT, preferred_element_type=F32)


def _in_proj(h, hm, w_in_t, casts, layer, *, bm, bn):
    m, k = h.shape
    n_i = m // bm
    n_steps = (BIG_COLS // bn) * n_i

    def w_rows(j, i):
        skip = jnp.where(j >= COL_CB // bn, N_LOGITS // SUBLANES, 0)
        return layer, (j * (bn // SUBLANES) + skip) * SUBLANES, 0

    cast_in_specs, cast_out_specs, cast_shapes = [], [], []
    for wgt in casts:
        rows, cols = wgt.shape[1:]
        blk = -(-rows // n_steps)
        blk = -(-blk // HALO) * HALO
        assert rows % blk == 0, (rows, blk)
        last = rows // blk - 1
        cast_in_specs.append(pl.BlockSpec(
            (None, blk, cols), functools.partial(lambda j, i, last: (layer, jnp.minimum(j * n_i + i, last), 0), last=last)))
        cast_out_specs.append(pl.BlockSpec(
            (blk, cols), functools.partial(lambda j, i, last: (jnp.minimum(j * n_i + i, last), 0), last=last)))
        cast_shapes.append(jax.ShapeDtypeStruct((rows, cols), BF16))

    est = (2 * (bm * k * 2 + k * bn * 4 + k * LANES * 4 + bm * bn * 2 + bm * LANES * 4) + k * bn * 2
           + 2 * bm * bn * 4 + (6 << 20))
    outs = pl.pallas_call(
        functools.partial(_in_proj_kernel, n_cast=len(casts)),
        grid=(BIG_COLS // bn, n_i),
        in_specs=[pl.BlockSpec((bm, k), lambda j, i: (i, 0)),
                  pl.BlockSpec((META_ROWS, k), lambda j, i: (0, 0)),
                  pl.BlockSpec((pl.Element(1), pl.Element(bn), pl.Element(k)), w_rows),
                  pl.BlockSpec((pl.Element(1), pl.Element(LANES), pl.Element(k)), lambda j, i: (layer, COL_CB, 0))]
        + cast_in_specs,
        out_specs=[pl.BlockSpec((bm, bn), lambda j, i: (i, j)),
                   pl.BlockSpec((META_ROWS, bn), lambda j, i: (0, j)),
                   pl.BlockSpec((bm, LANES), lambda j, i: (jnp.where(j == 0, i, n_i - 1), 0)),
                   pl.BlockSpec((META_ROWS, LANES), lambda j, i: (0, 0))] + cast_out_specs,
        out_shape=[jax.ShapeDtypeStruct((m, BIG_COLS), BF16),
                   jax.ShapeDtypeStruct((META_ROWS, BIG_COLS), BF16),
                   jax.ShapeDtypeStruct((m, LANES), F32),
                   jax.ShapeDtypeStruct((META_ROWS, LANES), F32)] + cast_shapes,
        scratch_shapes=[pltpu.VMEM((bn, k), BF16), pltpu.VMEM((LANES, k), BF16)],
        compiler_params=_params(("arbitrary", "arbitrary"), est),
        name="in_proj",
    )(h, hm, w_in_t, w_in_t, *casts)
    return outs[:4], outs[4:]


def _shift_rows(x, d):
    return pltpu.roll(x, d, axis=0)


def _cumsum_rows(x):
    rows = x.shape[0]
    row = lax.broadcasted_iota(jnp.int32, x.shape, 0)
    d = 1
    while d < rows:
        x = x + jnp.where(row >= d, _shift_rows(x, d), 0.0)
        d *= 2
    return x


def _cummax_rows(x):
    rows = x.shape[0]
    row = lax.broadcasted_iota(jnp.int32, x.shape, 0)
    d = 1
    while d < rows:
        x = jnp.maximum(x, jnp.where(row >= d, _shift_rows(x, d), -jnp.inf))
        d *= 2
    return x


def _log_sigmoid(x):
    return jnp.minimum(x, 0.0) - jnp.log1p(jnp.exp(-jnp.abs(x)))


def _mlstm_chunk(streams, bias_ref, gain_ref, *, chunk, n_valid):
    scale = HEAD_DIM ** -0.5
    last = n_valid - 1
    row1 = lax.broadcasted_iota(jnp.int32, (chunk, LANES), 0)
    gates = []
    for q_ref, k_ref, v_ref, o_ref, gl_ref, y_ref, c_scr, m_scr, s_scr, r_scr in streams:
        gl = gl_ref[...] + bias_ref[...]
        logf = pltpu.roll(_log_sigmoid(gl), LANES - HEADS, axis=1)
        b = _cumsum_rows(logf)
        a = gl - b
        m_prev = m_scr[...]
        g = jnp.maximum(_cummax_rows(a), m_prev)
        inter_w = jnp.exp(m_prev - g) * scale
        exp_neg_m = jnp.exp(-(b + g))
        g_last = g[last:last + 1, :]
        decay = jnp.exp(m_prev - g_last)
        ws = jnp.where(row1 <= last, jnp.exp(a - g_last), 0.0)
        m_scr[...] = b[last:last + 1, :] + g_last
        a_t = (a + math.log(scale)).T
        gates.append((g, a_t, inter_w, exp_neg_m, ws, decay))

    rows = lax.broadcasted_iota(jnp.int32, (chunk, chunk), 0)
    cols = lax.broadcasted_iota(jnp.int32, (chunk, chunk), 1)
    causal = rows >= cols
    ones = jnp.ones((chunk, HEAD_DIM), BF16)
    head = [slice(h * HEAD_DIM, (h + 1) * HEAD_DIM) for h in range(HEADS)]
    wide = [slice(h * 2 * HEAD_DIM, (h + 1) * 2 * HEAD_DIM) for h in range(HEADS)]
    work = [(h, st, gt) for h in range(HEADS) for st, gt in zip(streams, gates)]

    for h, (q_ref, k_ref, _, _, _, _, _, _, s_scr, _), (g, a_t, _, _, _, _) in work:
        w = jnp.where(causal, jnp.exp(a_t[h:h + 1, :] - g[:, h:h + 1]), 0.0)
        s = lax.dot_general(q_ref[:, head[h]], k_ref[:, head[h]], _NT, preferred_element_type=F32)
        s_scr[h, 0:chunk, 0:chunk] = (s * w).astype(BF16)
    for h, (q_ref, _, v_ref, _, _, _, c_scr, _, s_scr, r_scr), (_, _, inter_w, _, _, _) in work:
        v_aug = jnp.concatenate([v_ref[:, head[h]], ones], axis=1)
        r_scr[0:chunk, wide[h]] = (
            jnp.dot(q_ref[:, head[h]], c_scr[h].astype(BF16), preferred_element_type=F32)
            * inter_w[:, h:h + 1]
            + jnp.dot(s_scr[h, 0:chunk, 0:chunk], v_aug, preferred_element_type=F32))
    for h, (_, k_ref, v_ref, _, _, _, c_scr, _, _, _), (_, _, _, _, ws, decay) in work:
        v_aug = jnp.concatenate([v_ref[:, head[h]], ones], axis=1)
        kw = (k_ref[:, head[h]].astype(F32) * ws[:, h:h + 1]).astype(BF16)
        upd = lax.dot_general(kw, v_aug, (((0,), (0,)), ((), ())), preferred_element_type=F32)
        c_scr[h] = decay[:, h:h + 1] * c_scr[h] + upd
    for h, (_, _, _, o_ref, _, y_ref, _, _, _, r_scr), (_, _, _, exp_neg_m, _, _) in work:
        r = r_scr[0:chunk, wide[h]]
        num, den = r[:, :HEAD_DIM], r[:, HEAD_DIM:]
        hh = num / jnp.maximum(jnp.abs(den), exp_neg_m[:, h:h + 1])
        hn = hh * lax.rsqrt(jnp.mean(hh * hh, axis=-1, keepdims=True) + RMS_EPS)
        y_ref[:, head[h]] = (hn * gain_ref[:, head[h]]
                             * jax.nn.sigmoid(o_ref[:, head[h]].astype(F32))).astype(BF16)


def _mlstm_kernel(q_ref, k_ref, v_ref, o_ref, gl_ref, qm_ref, km_ref, vm_ref, om_ref, glm_ref,
                  bias_ref, gain_ref, y_ref, ym_ref, c_scr, m_scr, c0_scr, m0_scr, s_scr, r_scr, *, group, chunk):
    first_chunk = pl.program_id(1) == 0

    @pl.when(jnp.logical_and(pl.program_id(0) == 0, first_chunk))
    def _():
        c0_scr[...] = jnp.zeros_like(c0_scr)
        m0_scr[...] = jnp.zeros_like(m0_scr)
        meta = (qm_ref, km_ref, vm_ref, om_ref, glm_ref, ym_ref, c0_scr, m0_scr, s_scr.at[0], r_scr.at[0])
        _mlstm_chunk([meta], bias_ref, gain_ref, chunk=META_ROWS, n_valid=N_META)

    @pl.when(first_chunk)
    def _():
        for bb in range(group):
            c_scr[bb] = c0_scr[...]
            m_scr[bb] = m0_scr[...]

    streams = [(q_ref.at[bb], k_ref.at[bb], v_ref.at[bb], o_ref.at[bb], gl_ref.at[bb], y_ref.at[bb],
                c_scr.at[bb], m_scr.at[bb], s_scr.at[bb], r_scr.at[bb]) for bb in range(group)]
    _mlstm_chunk(streams, bias_ref, gain_ref, chunk=chunk, n_valid=chunk)


def _mlstm(big, big_m, logits, logits_m, bias_row, gain, *, batches, group, chunk):
    m = big.shape[0]
    seq = m // batches
    big3 = big.reshape(batches, seq, BIG_COLS)
    logits3 = logits.reshape(batches, seq, LANES)
    cblk = [COL_Q // MIX_WIDTH, COL_K // MIX_WIDTH, COL_V // MIX_WIDTH, COL_O // MIX_WIDTH]
    main = [pl.BlockSpec((group, chunk, MIX_WIDTH), functools.partial(lambda b, c, cb: (b, c, cb), cb=cb))
            for cb in cblk]
    meta = [pl.BlockSpec((META_ROWS, MIX_WIDTH), functools.partial(lambda b, c, cb: (0, cb), cb=cb))
            for cb in cblk]
    in_specs = (main + [pl.BlockSpec((group, chunk, LANES), lambda b, c: (b, c, 0))]
                + meta + [pl.BlockSpec((META_ROWS, LANES), lambda b, c: (0, 0)),
                          pl.BlockSpec((1, LANES), lambda b, c: (0, 0)),
                          pl.BlockSpec((1, MIX_WIDTH), lambda b, c: (0, 0))])
    c_shape = (HEADS, HEAD_DIM, 2 * HEAD_DIM)
    est = (group * (12 * chunk * MIX_WIDTH * 2 + HEADS * chunk * chunk * 2 + chunk * HEADS * 2 * HEAD_DIM * 4
                    + HEADS * HEAD_DIM * 2 * HEAD_DIM * 4 + 24 * chunk * LANES * 4) + (16 << 20))
    y, y_m = pl.pallas_call(
        functools.partial(_mlstm_kernel, group=group, chunk=chunk),
        grid=(batches // group, seq // chunk),
        in_specs=in_specs,
        out_specs=[pl.BlockSpec((group, chunk, MIX_WIDTH), lambda b, c: (b, c, 0)),
                   pl.BlockSpec((META_ROWS, MIX_WIDTH), lambda b, c: (0, 0))],
        out_shape=[jax.ShapeDtypeStruct((batches, seq, MIX_WIDTH), BF16),
                   jax.ShapeDtypeStruct((META_ROWS, MIX_WIDTH), BF16)],
        scratch_shapes=[pltpu.VMEM((group,) + c_shape, F32), pltpu.VMEM((group, 1, LANES), F32),
                        pltpu.VMEM(c_shape, F32), pltpu.VMEM((1, LANES), F32),
                        pltpu.VMEM((group, HEADS, chunk, chunk), BF16),
                        pltpu.VMEM((group, chunk, HEADS * 2 * HEAD_DIM), F32)],
        compiler_params=_params(("arbitrary", "arbitrary"), est),
        name="mlstm",
    )(big3, big3, big3, big3, logits3, big_m, big_m, big_m, big_m, logits_m, bias_row, gain)
    return y.reshape(m, MIX_WIDTH), y_m


POOL_BLOCK = 128
MERGE_COLS = 512
CONV_ROWS, CONV_COLS = 128, 256


def _pool_bands():
    bands = np.zeros((len(POOL_WINDOWS), POOL_BLOCK, 2 * POOL_BLOCK), np.float32)
    for grp, win in enumerate(POOL_WINDOWS):
        for r in range(POOL_BLOCK):
            for lag in range(win):
                bands[grp, r, POOL_BLOCK + r - lag] += 1.0 / win
            bands[grp, r, POOL_BLOCK + r] -= 1.0
    return jnp.asarray(bands, BF16)


def _conv3(cb_ref, cc_ref, cx_ref, halo, cw_ref, yb_ref, p_scr, *, rows):
    p_scr[0:HALO, :] = halo
    pieces = [(r, c) for r in range(0, rows, CONV_ROWS) for c in range(0, MIX_WIDTH, CONV_COLS)]
    for r, c in pieces:
        rs, cs = slice(r, r + CONV_ROWS), slice(c, c + CONV_COLS)
        p_scr[HALO + r:HALO + r + CONV_ROWS, cs] = cc_ref[rs, cs].astype(F32) * cx_ref[rs, cs].astype(F32)
    for r, c in pieces:
        rs, cs = slice(r, r + CONV_ROWS), slice(c, c + CONV_COLS)
        conv = cw_ref[2:3, cs] * p_scr[HALO + r:HALO + r + CONV_ROWS, cs]
        conv = conv + cw_ref[1:2, cs] * p_scr[HALO + r - 1:HALO + r - 1 + CONV_ROWS, cs]
        conv = conv + cw_ref[0:1, cs] * p_scr[HALO + r - 2:HALO + r - 2 + CONV_ROWS, cs]
        yb_ref[rs, cs] = (cb_ref[rs, cs].astype(F32) * conv).astype(BF16)


def _pool_from_start(pu_ref, pw_scr, ps_ref, yc_ref, u_scr, *, rows):
    u_scr[0:HALO, :] = jnp.zeros((HALO, MIX_WIDTH), F32)
    u_scr[HALO:HALO + rows, :] = pu_ref[...].astype(F32)
    t = lax.broadcasted_iota(jnp.int32, (rows, 1), 0)
    for grp, win in enumerate(POOL_WINDOWS):
        sl = slice(grp * POOL_GROUP, (grp + 1) * POOL_GROUP)
        cur = u_scr[HALO:HALO + rows, sl]
        tot = cur
        for lag in range(1, win):
            tot = tot + u_scr[HALO - lag:HALO - lag + rows, sl]
        pooled = tot / jnp.minimum(t + 1, win).astype(F32) - cur
        y = jnp.dot(pooled.astype(BF16), pw_scr[grp], preferred_element_type=F32)
        yc_ref[:, sl] = (y * ps_ref[:, sl]).astype(BF16)


def _pool_banded(band_ref, pw_scr, ps_ref, yc_ref, ub_scr, *, rows):
    for grp in range(len(POOL_WINDOWS)):
        sl = slice(grp * POOL_GROUP, (grp + 1) * POOL_GROUP)
        pooled = jnp.concatenate(
            [jnp.dot(band_ref[grp], ub_scr[r:r + 2 * POOL_BLOCK, sl], preferred_element_type=F32)
             for r in range(0, rows, POOL_BLOCK)], axis=0)
        y = jnp.dot(pooled.astype(BF16), pw_scr[grp], preferred_element_type=F32)
        yc_ref[:, sl] = (y * ps_ref[:, sl]).astype(BF16)


def _branch_term(y_ref, w_ref, n, g_ref, cs):
    return (jax.nn.sigmoid(g_ref[:, cs].astype(F32))
            * jnp.dot(y_ref[...], w_ref[n, :, cs], preferred_element_type=F32))


def _merge(ya_ref, yb_ref, yc_ref, w_ref, g0_ref, g1_ref, g2_ref, o_ref):
    for c in range(0, o_ref.shape[1], MERGE_COLS):
        cs = slice(c, c + MERGE_COLS)
        acc = _branch_term(ya_ref, w_ref, 0, g0_ref, cs)
        acc = acc + _branch_term(yb_ref, w_ref, 1, g1_ref, cs)
        o_ref[:, cs] = (acc + _branch_term(yc_ref, w_ref, 2, g2_ref, cs)).astype(BF16)


def _mix_merge_kernel(cb_ref, cc_ref, cx_ref, pu_ref, ccp_ref, cxp_ref, pup_ref, ya_ref, g0_ref, g1_ref, g2_ref,
                      cbm_ref, ccm_ref, cxm_ref, pum_ref, yam_ref, g0m_ref, g1m_ref, g2m_ref,
                      cw_ref, pw_ref, ps_ref, band_ref, w_ref, o_ref, om_ref,
                      pw_scr, p_scr, u_scr, ub_scr, yb_scr, yc_scr, ybm_scr, ycm_scr, *, bm, tiles_per_batch):
    @pl.when(pl.program_id(0) == 0)
    def _():
        pw_scr[...] = pw_ref[...].astype(BF16)
        _conv3(cbm_ref, ccm_ref, cxm_ref, jnp.zeros((HALO, MIX_WIDTH), F32), cw_ref, ybm_scr, p_scr,
               rows=META_ROWS)
        _pool_from_start(pum_ref, pw_scr, ps_ref, ycm_scr, u_scr, rows=META_ROWS)
        _merge(yam_ref, ybm_scr, ycm_scr, w_ref, g0m_ref, g1m_ref, g2m_ref, om_ref)

    first = pl.program_id(0) % tiles_per_batch == 0
    cch = jnp.where(first, ccm_ref[0:HALO, :], ccp_ref[...]).astype(F32)
    cxh = jnp.where(first, cxm_ref[0:HALO, :], cxp_ref[...]).astype(F32)
    _conv3(cb_ref, cc_ref, cx_ref, cch * cxh, cw_ref, yb_scr, p_scr, rows=bm)
    meta_tail = jnp.concatenate([jnp.zeros((POOL_BLOCK - N_META, MIX_WIDTH), BF16), pum_ref[0:N_META, :]], axis=0)
    ub_scr[0:POOL_BLOCK, :] = jnp.where(first, meta_tail, pup_ref[...])
    ub_scr[POOL_BLOCK:POOL_BLOCK + bm, :] = pu_ref[...]
    _pool_banded(band_ref, pw_scr, ps_ref, yc_scr, ub_scr, rows=bm)
    _merge(ya_ref, yb_scr, yc_scr, w_ref, g0_ref, g1_ref, g2_ref, o_ref)


def _mix_merge(big, big_m, ya, ya_m, conv_w, pool_w, pool_scale, w_branch, layer, *, bm, tiles_per_batch):
    m = big.shape[0]
    cblk = [COL_CB // MIX_WIDTH, COL_CC // MIX_WIDTH, COL_CX // MIX_WIDTH, COL_PU // MIX_WIDTH]
    gblk = [(COL_GATES + n * D_MODEL) // D_MODEL for n in range(N_BRANCH)]

    def prev_rows(rows, c):
        per = bm // rows
        return pl.BlockSpec((rows, MIX_WIDTH), lambda i: (jnp.maximum(i * per - 1, 0), c))

    def rows_spec(rows, width, c, meta):
        return pl.BlockSpec((rows, width), (lambda i: (0, c)) if meta else (lambda i: (i, c)))

    in_specs = [rows_spec(bm, MIX_WIDTH, c, False) for c in cblk]
    in_specs += [prev_rows(HALO, cblk[1]), prev_rows(HALO, cblk[2]), prev_rows(POOL_BLOCK, cblk[3])]
    in_specs += [rows_spec(bm, MIX_WIDTH, 0, False)] + [rows_spec(bm, D_MODEL, c, False) for c in gblk]
    in_specs += [rows_spec(META_ROWS, MIX_WIDTH, c, True) for c in cblk]
    in_specs += [rows_spec(META_ROWS, MIX_WIDTH, 0, True)] + [rows_spec(META_ROWS, D_MODEL, c, True) for c in gblk]
    in_specs += [pl.BlockSpec((None, CONV_WIDTH, MIX_WIDTH), lambda i: (layer, 0, 0)),
                 pl.BlockSpec((None, len(POOL_WINDOWS), POOL_GROUP, POOL_GROUP), lambda i: (layer, 0, 0, 0)),
                 pl.BlockSpec((1, MIX_WIDTH), lambda i: (0, 0)),
                 pl.BlockSpec((len(POOL_WINDOWS), POOL_BLOCK, 2 * POOL_BLOCK), lambda i: (0, 0, 0)),
                 pl.BlockSpec((N_BRANCH, MIX_WIDTH, D_MODEL), lambda i: (0, 0, 0), pipeline_mode=pl.Buffered(1))]
    est = (N_BRANCH * MIX_WIDTH * D_MODEL * 2 + 2 * bm * (5 * MIX_WIDTH + 4 * D_MODEL) * 2
           + 4 * bm * D_MODEL * 4 + 6 * bm * MIX_WIDTH * 4 + (10 << 20))
    return pl.pallas_call(
        functools.partial(_mix_merge_kernel, bm=bm, tiles_per_batch=tiles_per_batch),
        grid=(m // bm,),
        in_specs=in_specs,
        out_specs=[pl.BlockSpec((bm, D_MODEL), lambda i: (i, 0)),
                   pl.BlockSpec((META_ROWS, D_MODEL), lambda i: (0, 0))],
        out_shape=[jax.ShapeDtypeStruct((m, D_MODEL), BF16), jax.ShapeDtypeStruct((META_ROWS, D_MODEL), BF16)],
        scratch_shapes=[pltpu.VMEM((len(POOL_WINDOWS), POOL_GROUP, POOL_GROUP), BF16),
                        pltpu.VMEM((HALO + max(bm, META_ROWS), MIX_WIDTH), F32),
                        pltpu.VMEM((HALO + META_ROWS, MIX_WIDTH), F32),
                        pltpu.VMEM((POOL_BLOCK + bm, MIX_WIDTH), BF16),
                        pltpu.VMEM((bm, MIX_WIDTH), BF16), pltpu.VMEM((bm, MIX_WIDTH), BF16),
                        pltpu.VMEM((META_ROWS, MIX_WIDTH), BF16), pltpu.VMEM((META_ROWS, MIX_WIDTH), BF16)],
        compiler_params=_params(("arbitrary",), est),
        name="mix_merge",
    )(big, big, big, big, big, big, big, ya, big, big, big,
      big_m, big_m, big_m, big_m, ya_m, big_m, big_m, big_m,
      conv_w, pool_w, pool_scale, _pool_bands(), w_branch)


FINISH_ROWS = 64


def _finish_rows(y_scr, slot, x_ref, gpost_ref, gnext_ref, xo_ref, ho_ref, *, row0, rows):
    n_n = y_scr.shape[1]
    for s in range(0, rows, FINISH_ROWS):
        if isinstance(row0, int):
            ys = slice(row0 + s, row0 + s + FINISH_ROWS)
        else:
            ys = pl.ds(pl.multiple_of(row0 + s, FINISH_ROWS), FINISH_ROWS)
        y = jnp.concatenate([y_scr[slot, c, ys, :] for c in range(n_n)], axis=1)
        xn = x_ref[s:s + FINISH_ROWS, :] + _rms(y, gpost_ref[...])
        xo_ref[s:s + FINISH_ROWS, :] = xn
        if ho_ref is not None:
            ho_ref[s:s + FINISH_ROWS, :] = _rms(xn, gnext_ref[...]).astype(BF16)


def _proj_residual_kernel(*refs, n_i, n_n, part, emit_h, with_meta):
    a_ref, w_ref, x_ref, gpost_ref = refs[:4]
    refs = refs[4:]
    gnext_ref = None
    if emit_h:
        gnext_ref = refs[0]
        refs = refs[1:]
    if with_meta:
        am_ref, xm_ref = refs[:2]
        refs = refs[2:]
    xo_ref = refs[0]
    refs = refs[1:]
    ho_ref = None
    if emit_h:
        ho_ref = refs[0]
        refs = refs[1:]
    if with_meta:
        xmo_ref = refs[0]
        refs = refs[1:]
        hmo_ref = None
        if emit_h:
            hmo_ref = refs[0]
            refs = refs[1:]
        y_scr, ym_scr = refs
    else:
        y_scr, = refs
    i = pl.program_id(0)
    n = pl.program_id(1) if n_n > 1 else 0

    def step(slot, do_matmul, do_finish):
        if do_finish:
            _finish_rows(y_scr, 1 - slot, x_ref, gpost_ref, gnext_ref, xo_ref, ho_ref, row0=n * part, rows=part)
        if do_matmul:
            y_scr[slot, n] = jnp.dot(a_ref[...], w_ref[...], preferred_element_type=F32)

    @pl.when(i == 0)
    def _():
        if with_meta:
            ym_scr[0, n] = jnp.dot(am_ref[...], w_ref[...], preferred_element_type=F32)

            def finish_meta():
                _finish_rows(ym_scr, 0, xm_ref, gpost_ref, gnext_ref, xmo_ref, hmo_ref, row0=0, rows=META_ROWS)
            if n_n > 1:
                pl.when(n == n_n - 1)(finish_meta)
            else:
                finish_meta()
        step(0, True, False)

    for parity in (0, 1):
        @pl.when(jnp.logical_and(jnp.logical_and(i > 0, i < n_i), lax.rem(i, 2) == parity))
        def _():
            step(parity, True, True)

    @pl.when(i == n_i)
    def _():
        step(n_i % 2, False, True)


def _proj_residual(a, a_m, w, x, x_m, g_post, g_next, *, bm, bn, name):
    m, kdim = a.shape
    n_i = m // bm
    n_n = D_MODEL // bn
    part = bm // n_n
    emit_h = g_next is not None
    with_meta = a_m is not None
    const = lambda i, n: (0, 0)
    prev_part = lambda i, n: (jnp.where(i == 0, 0, (i - 1) * n_n + n), 0)
    vec = pl.BlockSpec((1, D_MODEL), const)
    w_mode = dict(pipeline_mode=pl.Buffered(1)) if n_n == 1 else {}
    in_specs = [pl.BlockSpec((bm, kdim), lambda i, n: (jnp.minimum(i, n_i - 1), 0)),
                pl.BlockSpec((kdim, bn), lambda i, n: (0, n), **w_mode),
                pl.BlockSpec((part, D_MODEL), prev_part), vec]
    args = [a, w, x, g_post]
    if emit_h:
        in_specs.append(vec)
        args.append(g_next)
    if with_meta:
        in_specs += [pl.BlockSpec((META_ROWS, kdim), const), pl.BlockSpec((META_ROWS, D_MODEL), const)]
        args += [a_m, x_m]
    out_specs = [pl.BlockSpec((part, D_MODEL), prev_part)]
    out_shape = [jax.ShapeDtypeStruct((m, D_MODEL), F32)]
    if emit_h:
        out_specs.append(pl.BlockSpec((part, D_MODEL), prev_part))
        out_shape.append(jax.ShapeDtypeStruct((m, D_MODEL), BF16))
    scratch = [pltpu.VMEM((2, n_n, bm, bn), F32)]
    if with_meta:
        out_specs.append(pl.BlockSpec((META_ROWS, D_MODEL), const))
        out_shape.append(jax.ShapeDtypeStruct((META_ROWS, D_MODEL), F32))
        if emit_h:
            out_specs.append(pl.BlockSpec((META_ROWS, D_MODEL), const))
            out_shape.append(jax.ShapeDtypeStruct((META_ROWS, D_MODEL), BF16))
        scratch.append(pltpu.VMEM((1, n_n, META_ROWS, bn), F32))
    w_bufs = 1 if n_n == 1 else 2
    est = (2 * (bm * kdim * 2 + part * D_MODEL * (4 + 4 + 2)) + w_bufs * kdim * bn * 2 + 2 * bm * D_MODEL * 4
           + 2 * bm * bn * 4 + 2 * META_ROWS * (kdim * 2 + 3 * D_MODEL * 4) + (4 << 20))
    outs = pl.pallas_call(
        functools.partial(_proj_residual_kernel, n_i=n_i, n_n=n_n, part=part, emit_h=emit_h, with_meta=with_meta),
        grid=(n_i + 1, n_n),
        in_specs=in_specs,
        out_specs=out_specs,
        out_shape=out_shape,
        scratch_shapes=scratch,
        compiler_params=_params(("arbitrary", "arbitrary"), est),
        name=name,
    )(*args)
    outs = list(outs)
    x_new = outs.pop(0)
    h_new = outs.pop(0) if emit_h else None
    xm_new = outs.pop(0) if with_meta else None
    hm_new = outs.pop(0) if (with_meta and emit_h) else None
    return x_new, h_new, xm_new, hm_new


def _gelu_tanh(x):
    return 0.5 * x * (1.0 + jnp.tanh(0.7978845608028654 * (x + 0.044715 * (x * x * x))))


def _ffn_in_tile(h_ref, wa_scr, wu_scr, cw_ref, halo, o_ref, a_scr, *, rows):
    cols = o_ref.shape[1]
    a_scr[0:SUBLANES, :] = jnp.zeros((SUBLANES, cols), F32) if halo is None else halo
    for r in range(0, rows, MATMUL_ROWS):
        n = min(MATMUL_ROWS, rows - r)
        h = h_ref[r:r + n, :]
        a = jnp.dot(h, wa_scr[...], preferred_element_type=F32)
        u = jnp.dot(h, wu_scr[...], preferred_element_type=F32)
        a_scr[SUBLANES + r:SUBLANES + r + n, :] = a
        conv = cw_ref[2:3, :] * a
        conv = conv + cw_ref[1:2, :] * a_scr[SUBLANES + r - 1:SUBLANES + r - 1 + n, :]
        conv = conv + cw_ref[0:1, :] * a_scr[SUBLANES + r - 2:SUBLANES + r - 2 + n, :]
        o_ref[r:r + n, :] = (_gelu_tanh(conv) * u).astype(BF16)


def _ffn_in_kernel(h_ref, hm_ref, wa_ref, wu_ref, cw_ref, o_ref, om_ref,
                   wa_scr, wu_scr, a_scr, tail_scr, mtail_scr, *, bm, tiles_per_batch):
    @pl.when(pl.program_id(1) == 0)
    def _():
        wa_scr[...] = wa_ref[...].astype(BF16)
        wu_scr[...] = wu_ref[...].astype(BF16)
        _ffn_in_tile(hm_ref, wa_scr, wu_scr, cw_ref, None, om_ref, a_scr, rows=META_ROWS)
        mtail_scr[...] = a_scr[N_META:N_META + SUBLANES, :]

    first = pl.program_id(1) % tiles_per_batch == 0
    halo = jnp.where(first, mtail_scr[...], tail_scr[...])
    _ffn_in_tile(h_ref, wa_scr, wu_scr, cw_ref, halo, o_ref, a_scr, rows=bm)
    tail_scr[...] = a_scr[bm:bm + SUBLANES, :]


def _ffn_in(h, hm, w_ffn_in, conv_w, layer, *, bm, bn, tiles_per_batch):
    m = h.shape[0]
    n_j = D_FF // bn
    est = (2 * (bm * D_MODEL * 2 + 2 * D_MODEL * bn * 4 + bm * bn * 2) + 2 * D_MODEL * bn * 2
           + 9 * bm * bn * 4 + (4 << 20))
    return pl.pallas_call(
        functools.partial(_ffn_in_kernel, bm=bm, tiles_per_batch=tiles_per_batch),
        grid=(n_j, m // bm),
        in_specs=[pl.BlockSpec((bm, D_MODEL), lambda j, i: (i, 0)),
                  pl.BlockSpec((META_ROWS, D_MODEL), lambda j, i: (0, 0)),
                  pl.BlockSpec((None, D_MODEL, bn), lambda j, i: (layer, 0, j)),
                  pl.BlockSpec((None, D_MODEL, bn), lambda j, i: (layer, 0, j + n_j)),
                  pl.BlockSpec((None, CONV_WIDTH, bn), lambda j, i: (layer, 0, j))],
        out_specs=[pl.BlockSpec((bm, bn), lambda j, i: (i, j)),
                   pl.BlockSpec((META_ROWS, bn), lambda j, i: (0, j))],
        out_shape=[jax.ShapeDtypeStruct((m, D_FF), BF16), jax.ShapeDtypeStruct((META_ROWS, D_FF), BF16)],
        scratch_shapes=[pltpu.VMEM((D_MODEL, bn), BF16), pltpu.VMEM((D_MODEL, bn), BF16),
                        pltpu.VMEM((SUBLANES + bm, bn), F32),
                        pltpu.VMEM((SUBLANES, bn), F32), pltpu.VMEM((SUBLANES, bn), F32)],
        compiler_params=_params(("arbitrary", "arbitrary"), est),
        name="ffn_in",
    )(h, hm, w_ffn_in, w_ffn_in, conv_w)


def kernel(x, meta_tokens, norm_pre_mix, norm_post_mix, norm_pre_ffn, norm_post_ffn, w_in, b_if,
           mlstm_head_gain, conv_mix_w, pool_w, pool_scale, w_branch, w_out, w_ffn_in, ffn_conv_w,
           w_ffn_out):
    batch, seq, _ = x.shape
    depth = w_in.shape[0]
    m = batch * seq
    xr = x.reshape(m, D_MODEL)
    xm = jnp.pad(meta_tokens.astype(F32), ((0, META_ROWS - N_META), (0, 0)))

    side_casts = [w_branch.astype(F32).reshape(depth, N_BRANCH * MIX_WIDTH, D_MODEL),
                  w_out.astype(F32), w_ffn_out.astype(F32)]
    w_ffn_in, pool_w = (t.astype(F32) for t in (w_ffn_in, pool_w))
    w_in_t = jnp.swapaxes(w_in.astype(F32), 1, 2)
    cw = conv_mix_w.astype(F32)
    fcw = ffn_conv_w.astype(F32)

    def vec(v):
        return v.reshape(1, -1).astype(F32)

    hr = _norm(xr, vec(norm_pre_mix[0]), 512)
    hm = _norm(xm, vec(norm_pre_mix[0]), META_ROWS)
    for l in range(depth):
        bias_row = jnp.pad(b_if[l].reshape(1, N_LOGITS).astype(F32), ((0, 0), (0, LANES - N_LOGITS)))
        gain = vec(mlstm_head_gain[l])
        ps = vec(pool_scale[l])
        g_post_mix, g_pre_ffn, g_post_ffn = vec(norm_post_mix[l]), vec(norm_pre_ffn[l]), vec(norm_post_ffn[l])
        last_layer = l + 1 == depth
        g_next = None if last_layer else vec(norm_pre_mix[l + 1])

        (big, big_m, logits, logits_m), (wbr, wo, wfo) = _in_proj(hr, hm, w_in_t, side_casts, l, bm=2048, bn=1024)
        wbr = wbr.reshape(N_BRANCH, MIX_WIDTH, D_MODEL)
        ya, ya_m = _mlstm(big, big_m, logits, logits_m, bias_row, gain, batches=batch, group=MLSTM_GROUP,
                          chunk=MLSTM_CHUNK)
        mg, mg_m = _mix_merge(big, big_m, ya, ya_m, cw, pool_w, ps, wbr, l, bm=512, tiles_per_batch=seq // 512)
        xr, hf, xm, hf_m = _proj_residual(mg, mg_m, wo, xr, xm, g_post_mix, g_pre_ffn,
                                          bm=512, bn=D_MODEL, name="out_proj")
        act, act_m = _ffn_in(hf, hf_m, w_ffn_in, fcw, l, bm=2048, bn=512, tiles_per_batch=seq // 2048)
        xr, hr, xm, hm = _proj_residual(act, None if last_layer else act_m, wfo, xr, xm, g_post_ffn, g_next,
                                        bm=256, bn=D_MODEL, name="ffn_out")
    return xr.reshape(batch, seq, D_MODEL)
```

```python
import functools
import math

import jax
import jax.numpy as jnp
import numpy as np
from jax import lax
from jax.experimental import pallas as pl
from jax.experimental.pallas import tpu as pltpu

D_MODEL = 2048
N_META = 16
MIX_WIDTH = 1024
N_BRANCH = 3
HEADS = 8
HEAD_DIM = MIX_WIDTH // HEADS
POOL_WINDOWS = (2, 4, 8, 16)
POOL_GROUP = MIX_WIDTH // len(POOL_WINDOWS)
D_FF = 5632
RMS_EPS = 1e-6
CONV_WIDTH = 3

COL_Q, COL_K, COL_V, COL_O, COL_CB, COL_CC, COL_CX, COL_PU = (n * MIX_WIDTH for n in range(8))
COL_GATES = 8 * MIX_WIDTH
BIG_COLS = COL_GATES + N_BRANCH * D_MODEL
N_LOGITS = 2 * HEADS
LANES = 128
HALO = 16
SUBLANES = 8
META_ROWS = 128
MLSTM_CHUNK = 512
MLSTM_GROUP = 1
VMEM_LIMIT_CAP = 60 * 1024 * 1024
MATMUL_ROWS = 1024

F32 = jnp.float32
BF16 = jnp.bfloat16


def _params(semantics, vmem_bytes):
    return pltpu.CompilerParams(dimension_semantics=semantics,
                                vmem_limit_bytes=min(int(vmem_bytes), VMEM_LIMIT_CAP))


def _rms(x, g):
    return x * lax.rsqrt(jnp.mean(x * x, axis=-1, keepdims=True) + RMS_EPS) * g


def _norm_kernel(x_ref, g_ref, h_ref):
    h_ref[...] = _rms(x_ref[...], g_ref[...]).astype(BF16)


def _norm(x, g, bm):
    m = x.shape[0]
    return pl.pallas_call(
        _norm_kernel,
        grid=(m // bm,),
        in_specs=[pl.BlockSpec((bm, D_MODEL), lambda i: (i, 0)),
                  pl.BlockSpec((1, D_MODEL), lambda i: (0, 0))],
        out_specs=pl.BlockSpec((bm, D_MODEL), lambda i: (i, 0)),
        out_shape=jax.ShapeDtypeStruct((m, D_MODEL), BF16),
        compiler_params=_params(("parallel",), 10 * bm * D_MODEL * 4),
        name="pre_norm",
    )(x, g)


_NT = (((1,), (1,)), ((), ()))


def _in_proj_kernel(*refs, n_cast):
    h_ref, hm_ref, wt_ref, wl_ref = refs[:4]
    cast_in = refs[4:4 + n_cast]
    o_ref, om_ref, lg_ref, lgm_ref = refs[4 + n_cast:8 + n_cast]
    cast_out = refs[8 + n_cast:8 + 2 * n_cast]
    w_scr, wl_scr = refs[8 + 2 * n_cast:]
    first_tile = pl.program_id(1) == 0

    @pl.when(first_tile)
    def _():
        w_scr[...] = wt_ref[0].astype(BF16)
        om_ref[...] = lax.dot_general(hm_ref[...], w_scr[...], _NT, preferred_element_type=F32).astype(BF16)

    for r in range(0, o_ref.shape[0], MATMUL_ROWS):
        rs = slice(r, r + MATMUL_ROWS)
        o_ref[rs, :] = lax.dot_general(h_ref[rs, :], w_scr[...], _NT, preferred_element_type=F32).astype(BF16)

    for src_ref, dst_ref in zip(cast_in, cast_out):
        dst_ref[...] = src_ref[...].astype(BF16)

    @pl.when(pl.program_id(0) == 0)
    def _():
        @pl.when(first_tile)
        def _():
            wl_scr[...] = wl_ref[0].astype(BF16)
            lgm_ref[...] = lax.dot_general(hm_ref[...], wl_scr[...], _NT, preferred_element_type=F32)

        lg_ref[...] = lax.dot_general(h_ref[...], wl_scr[...], _NT, preferred_element_type=F32)


def _in_proj(h, hm, w_in_t, casts, layer, *, bm, bn):
    m, k = h.shape
    n_i = m // bm
    n_steps = (BIG_COLS // bn) * n_i

    def w_rows(j, i):
        skip = jnp.where(j >= COL_CB // bn, N_LOGITS // SUBLANES, 0)
        return layer, (j * (bn // SUBLANES) + skip) * SUBLANES, 0

    cast_in_specs, cast_out_specs, cast_shapes = [], [], []
    for wgt in casts:
        rows, cols = wgt.shape[1:]
        blk = -(-rows // n_steps)
        blk = -(-blk // HALO) * HALO
        assert rows % blk == 0, (rows, blk)
        last = rows // blk - 1
        cast_in_specs.append(pl.BlockSpec(
            (None, blk, cols), functools.partial(lambda j, i, last: (layer, jnp.minimum(j * n_i + i, last), 0), last=last)))
        cast_out_specs.append(pl.BlockSpec(
            (blk, cols), functools.partial(lambda j, i, last: (jnp.minimum(j * n_i + i, last), 0), last=last)))
        cast_shapes.append(jax.ShapeDtypeStruct((rows, cols), BF16))

    est = (2 * (bm * k * 2 + k * bn * 4 + k * LANES * 4 + bm * bn * 2 + bm * LANES * 4) + k * bn * 2
           + 2 * bm * bn * 4 + (6 << 20))
    outs = pl.pallas_call(
        functools.partial(_in_proj_kernel, n_cast=len(casts)),
        grid=(BIG_COLS // bn, n_i),
        in_specs=[pl.BlockSpec((bm, k), lambda j, i: (i, 0)),
                  pl.BlockSpec((META_ROWS, k), lambda j, i: (0, 0)),
                  pl.BlockSpec((pl.Element(1), pl.Element(bn), pl.Element(k)), w_rows),
                  pl.BlockSpec((pl.Element(1), pl.Element(LANES), pl.Element(k)), lambda j, i: (layer, COL_CB, 0))]
        + cast_in_specs,
        out_specs=[pl.BlockSpec((bm, bn), lambda j, i: (i, j)),
                   pl.BlockSpec((META_ROWS, bn), lambda j, i: (0, j)),
                   pl.BlockSpec((bm, LANES), lambda j, i: (jnp.where(j == 0, i, n_i - 1), 0)),
                   pl.BlockSpec((META_ROWS, LANES), lambda j, i: (0, 0))] + cast_out_specs,
        out_shape=[jax.ShapeDtypeStruct((m, BIG_COLS), BF16),
                   jax.ShapeDtypeStruct((META_ROWS, BIG_COLS), BF16),
                   jax.ShapeDtypeStruct((m, LANES), F32),
                   jax.ShapeDtypeStruct((META_ROWS, LANES), F32)] + cast_shapes,
        scratch_shapes=[pltpu.VMEM((bn, k), BF16), pltpu.VMEM((LANES, k), BF16)],
        compiler_params=_params(("arbitrary", "arbitrary"), est),
        name="in_proj",
    )(h, hm, w_in_t, w_in_t, *casts)
    return outs[:4], outs[4:]


def _shift_rows(x, d):
    return pltpu.roll(x, d, axis=0)


def _cumsum_rows(x):
    rows = x.shape[0]
    row = lax.broadcasted_iota(jnp.int32, x.shape, 0)
    d = 1
    while d < rows:
        x = x + jnp.where(row >= d, _shift_rows(x, d), 0.0)
        d *= 2
    return x


def _cummax_rows(x):
    rows = x.shape[0]
    row = lax.broadcasted_iota(jnp.int32, x.shape, 0)
    d = 1
    while d < rows:
        x = jnp.maximum(x, jnp.where(row >= d, _shift_rows(x, d), -jnp.inf))
        d *= 2
    return x


def _log_sigmoid(x):
    return jnp.minimum(x, 0.0) - jnp.log1p(jnp.exp(-jnp.abs(x)))


def _mlstm_chunk(streams, bias_ref, gain_ref, *, chunk, n_valid):
    scale = HEAD_DIM ** -0.5
    last = n_valid - 1
    row1 = lax.broadcasted_iota(jnp.int32, (chunk, LANES), 0)
    gates = []
    for q_ref, k_ref, v_ref, o_ref, gl_ref, y_ref, c_scr, m_scr, s_scr, r_scr in streams:
        gl = gl_ref[...] + bias_ref[...]
        logf = pltpu.roll(_log_sigmoid(gl), LANES - HEADS, axis=1)
        b = _cumsum_rows(logf)
        a = gl - b
        m_prev = m_scr[...]
        g = jnp.maximum(_cummax_rows(a), m_prev)
        inter_w = jnp.exp(m_prev - g) * scale
        exp_neg_m = jnp.exp(-(b + g))
        g_last = g[last:last + 1, :]
        decay = jnp.exp(m_prev - g_last)
        ws = jnp.where(row1 <= last, jnp.exp(a - g_last), 0.0)
        m_scr[...] = b[last:last + 1, :] + g_last
        a_t = (a + math.log(scale)).T
        gates.append((g, a_t, inter_w, exp_neg_m, ws, decay))

    rows = lax.broadcasted_iota(jnp.int32, (chunk, chunk), 0)
    cols = lax.broadcasted_iota(jnp.int32, (chunk, chunk), 1)
    causal = rows >= cols
    ones = jnp.ones((chunk, HEAD_DIM), BF16)
    head = [slice(h * HEAD_DIM, (h + 1) * HEAD_DIM) for h in range(HEADS)]
    wide = [slice(h * 2 * HEAD_DIM, (h + 1) * 2 * HEAD_DIM) for h in range(HEADS)]
    work = [(h, st, gt) for h in range(HEADS) for st, gt in zip(streams, gates)]

    for h, (q_ref, k_ref, _, _, _, _, _, _, s_scr, _), (g, a_t, _, _, _, _) in work:
        w = jnp.where(causal, jnp.exp(a_t[h:h + 1, :] - g[:, h:h + 1]), 0.0)
        s = lax.dot_general(q_ref[:, head[h]], k_ref[:, head[h]], _NT, preferred_element_type=F32)
        s_scr[h, 0:chunk, 0:chunk] = (s * w).astype(BF16)
    for h, (q_ref, _, v_ref, _, _, _, c_scr, _, s_scr, r_scr), (_, _, inter_w, _, _, _) in work:
        v_aug = jnp.concatenate([v_ref[:, head[h]], ones], axis=1)
        r_scr[0:chunk, wide[h]] = (
            jnp.dot(q_ref[:, head[h]], c_scr[h].astype(BF16), preferred_element_type=F32)
            * inter_w[:, h:h + 1]
            + jnp.dot(s_scr[h, 0:chunk, 0:chunk], v_aug, preferred_element_type=F32))
    for h, (_, k_ref, v_ref, _, _, _, c_scr, _, _, _), (_, _, _, _, ws, decay) in work:
        v_aug = jnp.concatenate([v_ref[:, head[h]], ones], axis=1)
        kw = (k_ref[:, head[h]].astype(F32) * ws[:, h:h + 1]).astype(BF16)
        upd = lax.dot_general(kw, v_aug, (((0,), (0,)), ((), ())), preferred_element_type=F32)
        c_scr[h] = decay[:, h:h + 1] * c_scr[h] + upd
    for h, (_, _, _, o_ref, _, y_ref, _, _, _, r_scr), (_, _, _, exp_neg_m, _, _) in work:
        r = r_scr[0:chunk, wide[h]]
        num, den = r[:, :HEAD_DIM], r[:, HEAD_DIM:]
        hh = num / jnp.maximum(jnp.abs(den), exp_neg_m[:, h:h + 1])
        hn = hh * lax.rsqrt(jnp.mean(hh * hh, axis=-1, keepdims=True) + RMS_EPS)
        y_ref[:, head[h]] = (hn * gain_ref[:, head[h]]
                             * jax.nn.sigmoid(o_ref[:, head[h]].astype(F32))).astype(BF16)


def _mlstm_kernel(q_ref, k_ref, v_ref, o_ref, gl_ref, qm_ref, km_ref, vm_ref, om_ref, glm_ref,
                  bias_ref, gain_ref, y_ref, ym_ref, c_scr, m_scr, c0_scr, m0_scr, s_scr, r_scr, *, group, chunk):
    first_chunk = pl.program_id(1) == 0

    @pl.when(jnp.logical_and(pl.program_id(0) == 0, first_chunk))
    def _():
        c0_scr[...] = jnp.zeros_like(c0_scr)
        m0_scr[...] = jnp.zeros_like(m0_scr)
        meta = (qm_ref, km_ref, vm_ref, om_ref, glm_ref, ym_ref, c0_scr, m0_scr, s_scr.at[0], r_scr.at[0])
        _mlstm_chunk([meta], bias_ref, gain_ref, chunk=META_ROWS, n_valid=N_META)

    @pl.when(first_chunk)
    def _():
        for bb in range(group):
            c_scr[bb] = c0_scr[...]
            m_scr[bb] = m0_scr[...]

    streams = [(q_ref.at[bb], k_ref.at[bb], v_ref.at[bb], o_ref.at[bb], gl_ref.at[bb], y_ref.at[bb],
                c_scr.at[bb], m_scr.at[bb], s_scr.at[bb], r_scr.at[bb]) for bb in range(group)]
    _mlstm_chunk(streams, bias_ref, gain_ref, chunk=chunk, n_valid=chunk)


def _mlstm(big, big_m, logits, logits_m, bias_row, gain, *, batches, group, chunk):
    m = big.shape[0]
    seq = m // batches
    big3 = big.reshape(batches, seq, BIG_COLS)
    logits3 = logits.reshape(batches, seq, LANES)
    cblk = [COL_Q // MIX_WIDTH, COL_K // MIX_WIDTH, COL_V // MIX_WIDTH, COL_O // MIX_WIDTH]
    main = [pl.BlockSpec((group, chunk, MIX_WIDTH), functools.partial(lambda b, c, cb: (b, c, cb), cb=cb))
            for cb in cblk]
    meta = [pl.BlockSpec((META_ROWS, MIX_WIDTH), functools.partial(lambda b, c, cb: (0, cb), cb=cb))
            for cb in cblk]
    in_specs = (main + [pl.BlockSpec((group, chunk, LANES), lambda b, c: (b, c, 0))]
                + meta + [pl.BlockSpec((META_ROWS, LANES), lambda b, c: (0, 0)),
                          pl.BlockSpec((1, LANES), lambda b, c: (0, 0)),
                          pl.BlockSpec((1, MIX_WIDTH), lambda b, c: (0, 0))])
    c_shape = (HEADS, HEAD_DIM, 2 * HEAD_DIM)
    est = (group * (12 * chunk * MIX_WIDTH * 2 + HEADS * chunk * chunk * 2 + chunk * HEADS * 2 * HEAD_DIM * 4
                    + HEADS * HEAD_DIM * 2 * HEAD_DIM * 4 + 24 * chunk * LANES * 4) + (16 << 20))
    y, y_m = pl.pallas_call(
        functools.partial(_mlstm_kernel, group=group, chunk=chunk),
        grid=(batches // group, seq // chunk),
        in_specs=in_specs,
        out_specs=[pl.BlockSpec((group, chunk, MIX_WIDTH), lambda b, c: (b, c, 0)),
                   pl.BlockSpec((META_ROWS, MIX_WIDTH), lambda b, c: (0, 0))],
        out_shape=[jax.ShapeDtypeStruct((batches, seq, MIX_WIDTH), BF16),
                   jax.ShapeDtypeStruct((META_ROWS, MIX_WIDTH), BF16)],
        scratch_shapes=[pltpu.VMEM((group,) + c_shape, F32), pltpu.VMEM((group, 1, LANES), F32),
                        pltpu.VMEM(c_shape, F32), pltpu.VMEM((1, LANES), F32),
                        pltpu.VMEM((group, HEADS, chunk, chunk), BF16),
                        pltpu.VMEM((group, chunk, HEADS * 2 * HEAD_DIM), F32)],
        compiler_params=_params(("arbitrary", "arbitrary"), est),
        name="mlstm",
    )(big3, big3, big3, big3, logits3, big_m, big_m, big_m, big_m, logits_m, bias_row, gain)
    return y.reshape(m, MIX_WIDTH), y_m


POOL_BLOCK = 128
MERGE_COLS = 512
CONV_ROWS, CONV_COLS = 128, 256


def _pool_bands():
    bands = np.zeros((len(POOL_WINDOWS), POOL_BLOCK, 2 * POOL_BLOCK), np.float32)
    for grp, win in enumerate(POOL_WINDOWS):
        for r in range(POOL_BLOCK):
            for lag in range(win):
                bands[grp, r, POOL_BLOCK + r - lag] += 1.0 / win
            bands[grp, r, POOL_BLOCK + r] -= 1.0
    return jnp.asarray(bands, BF16)


def _conv3(cb_ref, cc_ref, cx_ref, halo, cw_ref, yb_ref, p_scr, *, rows):
    p_scr[0:HALO, :] = halo
    pieces = [(r, c) for r in range(0, rows, CONV_ROWS) for c in range(0, MIX_WIDTH, CONV_COLS)]
    for r, c in pieces:
        rs, cs = slice(r, r + CONV_ROWS), slice(c, c + CONV_COLS)
        p_scr[HALO + r:HALO + r + CONV_ROWS, cs] = cc_ref[rs, cs].astype(F32) * cx_ref[rs, cs].astype(F32)
    for r, c in pieces:
        rs, cs = slice(r, r + CONV_ROWS), slice(c, c + CONV_COLS)
        conv = cw_ref[2:3, cs] * p_scr[HALO + r:HALO + r + CONV_ROWS, cs]
        conv = conv + cw_ref[1:2, cs] * p_scr[HALO + r - 1:HALO + r - 1 + CONV_ROWS, cs]
        conv = conv + cw_ref[0:1, cs] * p_scr[HALO + r - 2:HALO + r - 2 + CONV_ROWS, cs]
        yb_ref[rs, cs] = (cb_ref[rs, cs].astype(F32) * conv).astype(BF16)


def _pool_from_start(pu_ref, pw_scr, ps_ref, yc_ref, u_scr, *, rows):
    u_scr[0:HALO, :] = jnp.zeros((HALO, MIX_WIDTH), F32)
    u_scr[HALO:HALO + rows, :] = pu_ref[...].astype(F32)
    t = lax.broadcasted_iota(jnp.int32, (rows, 1), 0)
    for grp, win in enumerate(POOL_WINDOWS):
        sl = slice(grp * POOL_GROUP, (grp + 1) * POOL_GROUP)
        cur = u_scr[HALO:HALO + rows, sl]
        tot = cur
        for lag in range(1, win):
            tot = tot + u_scr[HALO - lag:HALO - lag + rows, sl]
        pooled = tot / jnp.minimum(t + 1, win).astype(F32) - cur
        y = jnp.dot(pooled.astype(BF16), pw_scr[grp], preferred_element_type=F32)
        yc_ref[:, sl] = (y * ps_ref[:, sl]).astype(BF16)


def _pool_banded(band_ref, pw_scr, ps_ref, yc_ref, ub_scr, *, rows):
    for grp in range(len(POOL_WINDOWS)):
        sl = slice(grp * POOL_GROUP, (grp + 1) * POOL_GROUP)
        pooled = jnp.concatenate(
            [jnp.dot(band_ref[grp], ub_scr[r:r + 2 * POOL_BLOCK, sl], preferred_element_type=F32)
             for r in range(0, rows, POOL_BLOCK)], axis=0)
        y = jnp.dot(pooled.astype(BF16), pw_scr[grp], preferred_element_type=F32)
        yc_ref[:, sl] = (y * ps_ref[:, sl]).astype(BF16)


def _branch_term(y_ref, w_ref, n, g_ref, cs):
    return (jax.nn.sigmoid(g_ref[:, cs].astype(F32))
            * jnp.dot(y_ref[...], w_ref[n, :, cs], preferred_element_type=F32))


def _merge(ya_ref, yb_ref, yc_ref, w_ref, g0_ref, g1_ref, g2_ref, o_ref):
    for c in range(0, o_ref.shape[1], MERGE_COLS):
        cs = slice(c, c + MERGE_COLS)
        acc = _branch_term(ya_ref, w_ref, 0, g0_ref, cs)
        acc = acc + _branch_term(yb_ref, w_ref, 1, g1_ref, cs)
        o_ref[:, cs] = (acc + _branch_term(yc_ref, w_ref, 2, g2_ref, cs)).astype(BF16)


def _mix_merge_kernel(cb_ref, cc_ref, cx_ref, pu_ref, ccp_ref, cxp_ref, pup_ref, ya_ref, g0_ref, g1_ref, g2_ref,
                      cbm_ref, ccm_ref, cxm_ref, pum_ref, yam_ref, g0m_ref, g1m_ref, g2m_ref,
                      cw_ref, pw_ref, ps_ref, band_ref, w_ref, o_ref, om_ref,
                      pw_scr, p_scr, u_scr, ub_scr, yb_scr, yc_scr, ybm_scr, ycm_scr, *, bm, tiles_per_batch):
    @pl.when(pl.program_id(0) == 0)
    def _():
        pw_scr[...] = pw_ref[...].astype(BF16)
        _conv3(cbm_ref, ccm_ref, cxm_ref, jnp.zeros((HALO, MIX_WIDTH), F32), cw_ref, ybm_scr, p_scr,
               rows=META_ROWS)
        _pool_from_start(pum_ref, pw_scr, ps_ref, ycm_scr, u_scr, rows=META_ROWS)
        _merge(yam_ref, ybm_scr, ycm_scr, w_ref, g0m_ref, g1m_ref, g2m_ref, om_ref)

    first = pl.program_id(0) % tiles_per_batch == 0
    cch = jnp.where(first, ccm_ref[0:HALO, :], ccp_ref[...]).astype(F32)
    cxh = jnp.where(first, cxm_ref[0:HALO, :], cxp_ref[...]).astype(F32)
    _conv3(cb_ref, cc_ref, cx_ref, cch * cxh, cw_ref, yb_scr, p_scr, rows=bm)
    meta_tail = jnp.concatenate([jnp.zeros((POOL_BLOCK - N_META, MIX_WIDTH), BF16), pum_ref[0:N_META, :]], axis=0)
    ub_scr[0:POOL_BLOCK, :] = jnp.where(first, meta_tail, pup_ref[...])
    ub_scr[POOL_BLOCK:POOL_BLOCK + bm, :] = pu_ref[...]
    _pool_banded(band_ref, pw_scr, ps_ref, yc_scr, ub_scr, rows=bm)
    _merge(ya_ref, yb_scr, yc_scr, w_ref, g0_ref, g1_ref, g2_ref, o_ref)


def _mix_merge(big, big_m, ya, ya_m, conv_w, pool_w, pool_scale, w_branch, layer, *, bm, tiles_per_batch):
    m = big.shape[0]
    cblk = [COL_CB // MIX_WIDTH, COL_CC // MIX_WIDTH, COL_CX // MIX_WIDTH, COL_PU // MIX_WIDTH]
    gblk = [(COL_GATES + n * D_MODEL) // D_MODEL for n in range(N_BRANCH)]

    def prev_rows(rows, c):
        per = bm // rows
        return pl.BlockSpec((rows, MIX_WIDTH), lambda i: (jnp.maximum(i * per - 1, 0), c))

    def rows_spec(rows, width, c, meta):
        return pl.BlockSpec((rows, width), (lambda i: (0, c)) if meta else (lambda i: (i, c)))

    in_specs = [rows_spec(bm, MIX_WIDTH, c, False) for c in cblk]
    in_specs += [prev_rows(HALO, cblk[1]), prev_rows(HALO, cblk[2]), prev_rows(POOL_BLOCK, cblk[3])]
    in_specs += [rows_spec(bm, MIX_WIDTH, 0, False)] + [rows_spec(bm, D_MODEL, c, False) for c in gblk]
    in_specs += [rows_spec(META_ROWS, MIX_WIDTH, c, True) for c in cblk]
    in_specs += [rows_spec(META_ROWS, MIX_WIDTH, 0, True)] + [rows_spec(META_ROWS, D_MODEL, c, True) for c in gblk]
    in_specs += [pl.BlockSpec((None, CONV_WIDTH, MIX_WIDTH), lambda i: (layer, 0, 0)),
                 pl.BlockSpec((None, len(POOL_WINDOWS), POOL_GROUP, POOL_GROUP), lambda i: (layer, 0, 0, 0)),
                 pl.BlockSpec((1, MIX_WIDTH), lambda i: (0, 0)),
                 pl.BlockSpec((len(POOL_WINDOWS), POOL_BLOCK, 2 * POOL_BLOCK), lambda i: (0, 0, 0)),
                 pl.BlockSpec((N_BRANCH, MIX_WIDTH, D_MODEL), lambda i: (0, 0, 0), pipeline_mode=pl.Buffered(1))]
    est = (N_BRANCH * MIX_WIDTH * D_MODEL * 2 + 2 * bm * (5 * MIX_WIDTH + 4 * D_MODEL) * 2
           + 4 * bm * D_MODEL * 4 + 6 * bm * MIX_WIDTH * 4 + (10 << 20))
    return pl.pallas_call(
        functools.partial(_mix_merge_kernel, bm=bm, tiles_per_batch=tiles_per_batch),
        grid=(m // bm,),
        in_specs=in_specs,
        out_specs=[pl.BlockSpec((bm, D_MODEL), lambda i: (i, 0)),
                   pl.BlockSpec((META_ROWS, D_MODEL), lambda i: (0, 0))],
        out_shape=[jax.ShapeDtypeStruct((m, D_MODEL), BF16), jax.ShapeDtypeStruct((META_ROWS, D_MODEL), BF16)],
        scratch_shapes=[pltpu.VMEM((len(POOL_WINDOWS), POOL_GROUP, POOL_GROUP), BF16),
                        pltpu.VMEM((HALO + max(bm, META_ROWS), MIX_WIDTH), F32),
                        pltpu.VMEM((HALO + META_ROWS, MIX_WIDTH), F32),
                        pltpu.VMEM((POOL_BLOCK + bm, MIX_WIDTH), BF16),
                        pltpu.VMEM((bm, MIX_WIDTH), BF16), pltpu.VMEM((bm, MIX_WIDTH), BF16),
                        pltpu.VMEM((META_ROWS, MIX_WIDTH), BF16), pltpu.VMEM((META_ROWS, MIX_WIDTH), BF16)],
        compiler_params=_params(("arbitrary",), est),
        name="mix_merge",
    )(big, big, big, big, big, big, big, ya, big, big, big,
      big_m, big_m, big_m, big_m, ya_m, big_m, big_m, big_m,
      conv_w, pool_w, pool_scale, _pool_bands(), w_branch)


FINISH_ROWS = 64


def _finish_rows(y_scr, slot, x_ref, gpost_ref, gnext_ref, xo_ref, ho_ref, *, row0, rows):
    n_n = y_scr.shape[1]
    for s in range(0, rows, FINISH_ROWS):
        if isinstance(row0, int):
            ys = slice(row0 + s, row0 + s + FINISH_ROWS)
        else:
            ys = pl.ds(pl.multiple_of(row0 + s, FINISH_ROWS), FINISH_ROWS)
        y = jnp.concatenate([y_scr[slot, c, ys, :] for c in range(n_n)], axis=1)
        xn = x_ref[s:s + FINISH_ROWS, :] + _rms(y, gpost_ref[...])
        xo_ref[s:s + FINISH_ROWS, :] = xn
        if ho_ref is not None:
            ho_ref[s:s + FINISH_ROWS, :] = _rms(xn, gnext_ref[...]).astype(BF16)


def _proj_residual_kernel(*refs, n_i, n_n, part, emit_h, with_meta):
    a_ref, w_ref, x_ref, gpost_ref = refs[:4]
    refs = refs[4:]
    gnext_ref = None
    if emit_h:
        gnext_ref = refs[0]
        refs = refs[1:]
    if with_meta:
        am_ref, xm_ref = refs[:2]
        refs = refs[2:]
    xo_ref = refs[0]
    refs = refs[1:]
    ho_ref = None
    if emit_h:
        ho_ref = refs[0]
        refs = refs[1:]
    if with_meta:
        xmo_ref = refs[0]
        refs = refs[1:]
        hmo_ref = None
        if emit_h:
            hmo_ref = refs[0]
            refs = refs[1:]
        y_scr, ym_scr = refs
    else:
        y_scr, = refs
    i = pl.program_id(0)
    n = pl.program_id(1) if n_n > 1 else 0

    def step(slot, do_matmul, do_finish):
        if do_finish:
            _finish_rows(y_scr, 1 - slot, x_ref, gpost_ref, gnext_ref, xo_ref, ho_ref, row0=n * part, rows=part)
        if do_matmul:
            y_scr[slot, n] = jnp.dot(a_ref[...], w_ref[...], preferred_element_type=F32)

    @pl.when(i == 0)
    def _():
        if with_meta:
            ym_scr[0, n] = jnp.dot(am_ref[...], w_ref[...], preferred_element_type=F32)

            def finish_meta():
                _finish_rows(ym_scr, 0, xm_ref, gpost_ref, gnext_ref, xmo_ref, hmo_ref, row0=0, rows=META_ROWS)
            if n_n > 1:
                pl.when(n == n_n - 1)(finish_meta)
            else:
                finish_meta()
        step(0, True, False)

    for parity in (0, 1):
        @pl.when(jnp.logical_and(jnp.logical_and(i > 0, i < n_i), lax.rem(i, 2) == parity))
        def _():
            step(parity, True, True)

    @pl.when(i == n_i)
    def _():
        step(n_i % 2, False, True)


def _proj_residual(a, a_m, w, x, x_m, g_post, g_next, *, bm, bn, name):
    m, kdim = a.shape
    n_i = m // bm
    n_n = D_MODEL // bn
    part = bm // n_n
    emit_h = g_next is not None
    with_meta = a_m is not None
    const = lambda i, n: (0, 0)
    prev_part = lambda i, n: (jnp.where(i == 0, 0, (i - 1) * n_n + n), 0)
    vec = pl.BlockSpec((1, D_MODEL), const)
    w_mode = dict(pipeline_mode=pl.Buffered(1)) if n_n == 1 else {}
    in_specs = [pl.BlockSpec((bm, kdim), lambda i, n: (jnp.minimum(i, n_i - 1), 0)),
                pl.BlockSpec((kdim, bn), lambda i, n: (0, n), **w_mode),
                pl.BlockSpec((part, D_MODEL), prev_part), vec]
    args = [a, w, x, g_post]
    if emit_h:
        in_specs.append(vec)
        args.append(g_next)
    if with_meta:
        in_specs += [pl.BlockSpec((META_ROWS, kdim), const), pl.BlockSpec((META_ROWS, D_MODEL), const)]
        args += [a_m, x_m]
    out_specs = [pl.BlockSpec((part, D_MODEL), prev_part)]
    out_shape = [jax.ShapeDtypeStruct((m, D_MODEL), F32)]
    if emit_h:
        out_specs.append(pl.BlockSpec((part, D_MODEL), prev_part))
        out_shape.append(jax.ShapeDtypeStruct((m, D_MODEL), BF16))
    scratch = [pltpu.VMEM((2, n_n, bm, bn), F32)]
    if with_meta:
        out_specs.append(pl.BlockSpec((META_ROWS, D_MODEL), const))
        out_shape.append(jax.ShapeDtypeStruct((META_ROWS, D_MODEL), F32))
        if emit_h:
            out_specs.append(pl.BlockSpec((META_ROWS, D_MODEL), const))
            out_shape.append(jax.ShapeDtypeStruct((META_ROWS, D_MODEL), BF16))
        scratch.append(pltpu.VMEM((1, n_n, META_ROWS, bn), F32))
    w_bufs = 1 if n_n == 1 else 2
    est = (2 * (bm * kdim * 2 + part * D_MODEL * (4 + 4 + 2)) + w_bufs * kdim * bn * 2 + 2 * bm * D_MODEL * 4
           + 2 * bm * bn * 4 + 2 * META_ROWS * (kdim * 2 + 3 * D_MODEL * 4) + (4 << 20))
    outs = pl.pallas_call(
        functools.partial(_proj_residual_kernel, n_i=n_i, n_n=n_n, part=part, emit_h=emit_h, with_meta=with_meta),
        grid=(n_i + 1, n_n),
        in_specs=in_specs,
        out_specs=out_specs,
        out_shape=out_shape,
        scratch_shapes=scratch,
        compiler_params=_params(("arbitrary", "arbitrary"), est),
        name=name,
    )(*args)
    outs = list(outs)
    x_new = outs.pop(0)
    h_new = outs.pop(0) if emit_h else None
    xm_new = outs.pop(0) if with_meta else None
    hm_new = outs.pop(0) if (with_meta and emit_h) else None
    return x_new, h_new, xm_new, hm_new


def _gelu_tanh(x):
    return 0.5 * x * (1.0 + jnp.tanh(0.7978845608028654 * (x + 0.044715 * (x * x * x))))


def _ffn_in_tile(h_ref, wa_scr, wu_scr, cw_ref, halo, o_ref, a_scr, *, rows):
    cols = o_ref.shape[1]
    a_scr[0:SUBLANES, :] = jnp.zeros((SUBLANES, cols), F32) if halo is None else halo
    pieces = [(r, min(MATMUL_ROWS, rows - r)) for r in range(0, rows, MATMUL_ROWS)]
    for r, n in pieces:
        a_scr[SUBLANES + r:SUBLANES + r + n, :] = jnp.dot(h_ref[r:r + n, :], wa_scr[...],
                                                         preferred_element_type=F32)
    for r, n in pieces:
        u = jnp.dot(h_ref[r:r + n, :], wu_scr[...], preferred_element_type=F32)
        a = a_scr[SUBLANES + r:SUBLANES + r + n, :]
        conv = cw_ref[2:3, :] * a
        conv = conv + cw_ref[1:2, :] * a_scr[SUBLANES + r - 1:SUBLANES + r - 1 + n, :]
        conv = conv + cw_ref[0:1, :] * a_scr[SUBLANES + r - 2:SUBLANES + r - 2 + n, :]
        o_ref[r:r + n, :] = (_gelu_tanh(conv) * u).astype(BF16)


def _ffn_in_kernel(h_ref, hm_ref, wa_ref, wu_ref, cw_ref, o_ref, om_ref,
                   wa_scr, wu_scr, a_scr, tail_scr, mtail_scr, *, bm, tiles_per_batch):
    @pl.when(pl.program_id(1) == 0)
    def _():
        wa_scr[...] = wa_ref[...].astype(BF16)
        wu_scr[...] = wu_ref[...].astype(BF16)
        _ffn_in_tile(hm_ref, wa_scr, wu_scr, cw_ref, None, om_ref, a_scr, rows=META_ROWS)
        mtail_scr[...] = a_scr[N_META:N_META + SUBLANES, :]

    first = pl.program_id(1) % tiles_per_batch == 0
    halo = jnp.where(first, mtail_scr[...], tail_scr[...])
    _ffn_in_tile(h_ref, wa_scr, wu_scr, cw_ref, halo, o_ref, a_scr, rows=bm)
    tail_scr[...] = a_scr[bm:bm + SUBLANES, :]


def _ffn_in(h, hm, w_ffn_in, conv_w, layer, *, bm, bn, tiles_per_batch):
    m = h.shape[0]
    n_j = D_FF // bn
    est = (2 * (bm * D_MODEL * 2 + 2 * D_MODEL * bn * 4 + bm * bn * 2) + 2 * D_MODEL * bn * 2
           + 9 * bm * bn * 4 + (4 << 20))
    return pl.pallas_call(
        functools.partial(_ffn_in_kernel, bm=bm, tiles_per_batch=tiles_per_batch),
        grid=(n_j, m // bm),
        in_specs=[pl.BlockSpec((bm, D_MODEL), lambda j, i: (i, 0)),
                  pl.BlockSpec((META_ROWS, D_MODEL), lambda j, i: (0, 0)),
                  pl.BlockSpec((None, D_MODEL, bn), lambda j, i: (layer, 0, j)),
                  pl.BlockSpec((None, D_MODEL, bn), lambda j, i: (layer, 0, j + n_j)),
                  pl.BlockSpec((None, CONV_WIDTH, bn), lambda j, i: (layer, 0, j))],
        out_specs=[pl.BlockSpec((bm, bn), lambda j, i: (i, j)),
                   pl.BlockSpec((META_ROWS, bn), lambda j, i: (0, j))],
        out_shape=[jax.ShapeDtypeStruct((m, D_FF), BF16), jax.ShapeDtypeStruct((META_ROWS, D_FF), BF16)],
        scratch_shapes=[pltpu.VMEM((D_MODEL, bn), BF16), pltpu.VMEM((D_MODEL, bn), BF16),
                        pltpu.VMEM((SUBLANES + bm, bn), F32),
                        pltpu.VMEM((SUBLANES, bn), F32), pltpu.VMEM((SUBLANES, bn), F32)],
        compiler_params=_params(("arbitrary", "arbitrary"), est),
        name="ffn_in",
    )(h, hm, w_ffn_in, w_ffn_in, conv_w)


def kernel(x, meta_tokens, norm_pre_mix, norm_post_mix, norm_pre_ffn, norm_post_ffn, w_in, b_if,
           mlstm_head_gain, conv_mix_w, pool_w, pool_scale, w_branch, w_out, w_ffn_in, ffn_conv_w,
           w_ffn_out):
    batch, seq, _ = x.shape
    depth = w_in.shape[0]
    m = batch * seq
    xr = x.reshape(m, D_MODEL)
    xm = jnp.pad(meta_tokens.astype(F32), ((0, META_ROWS - N_META), (0, 0)))

    side_casts = [w_branch.astype(F32).reshape(depth, N_BRANCH * MIX_WIDTH, D_MODEL),
                  w_out.astype(F32), w_ffn_out.astype(F32)]
    w_ffn_in, pool_w = (t.astype(F32) for t in (w_ffn_in, pool_w))
    w_in_t = jnp.swapaxes(w_in.astype(F32), 1, 2)
    cw = conv_mix_w.astype(F32)
    fcw = ffn_conv_w.astype(F32)

    def vec(v):
        return v.reshape(1, -1).astype(F32)

    hr = _norm(xr, vec(norm_pre_mix[0]), 512)
    hm = _norm(xm, vec(norm_pre_mix[0]), META_ROWS)
    for l in range(depth):
        bias_row = jnp.pad(b_if[l].reshape(1, N_LOGITS).astype(F32), ((0, 0), (0, LANES - N_LOGITS)))
        gain = vec(mlstm_head_gain[l])
        ps = vec(pool_scale[l])
        g_post_mix, g_pre_ffn, g_post_ffn = vec(norm_post_mix[l]), vec(norm_pre_ffn[l]), vec(norm_post_ffn[l])
        last_layer = l + 1 == depth
        g_next = None if last_layer else vec(norm_pre_mix[l + 1])

        (big, big_m, logits, logits_m), (wbr, wo, wfo) = _in_proj(hr, hm, w_in_t, side_casts, l, bm=2048, bn=1024)
        wbr = wbr.reshape(N_BRANCH, MIX_WIDTH, D_MODEL)
        ya, ya_m = _mlstm(big, big_m, logits, logits_m, bias_row, gain, batches=batch, group=MLSTM_GROUP,
                          chunk=MLSTM_CHUNK)
        mg, mg_m = _mix_merge(big, big_m, ya, ya_m, cw, pool_w, ps, wbr, l, bm=512, tiles_per_batch=seq // 512)
        xr, hf, xm, hf_m = _proj_residual(mg, mg_m, wo, xr, xm, g_post_mix, g_pre_ffn,
                                          bm=512, bn=D_MODEL, name="out_proj")
        act, act_m = _ffn_in(hf, hf_m, w_ffn_in, fcw, l, bm=2048, bn=512, tiles_per_batch=seq // 2048)
        xr, hr, xm, hm = _proj_residual(act, None if last_layer else act_m, wfo, xr, xm, g_post_ffn, g_next,
                                        bm=256, bn=D_MODEL, name="ffn_out")
    return xr.reshape(batch, seq, D_MODEL)
```
